```python
import jax, jax.numpy as jnp
from jax import lax
import numpy as np


D_MODEL = 4096
BATCH = 4
SEQ = 2048
DEPTH = 1
DEC_BATCH = 128
DEC_SEQ = 4
PAST_LEN = 16384
PAGE_SIZE = 128

HGRN_HEADS = 16
HGRN_KEY_DIM = 128
HGRN_HEAD_DIM = 128
HGRN_WIDTH = HGRN_HEADS * HGRN_KEY_DIM
HGRN_CHUNK = 64
GMLP_CHUNK = 128
GMLP_GROUPS = 16
GMLP_GROUP_DIM = 128
GMLP_WIDTH = GMLP_GROUPS * GMLP_GROUP_DIM
SPLITS = (HGRN_WIDTH, 2 * HGRN_WIDTH, 3 * HGRN_WIDTH, 4 * HGRN_WIDTH,
          4 * HGRN_WIDTH + GMLP_WIDTH, 4 * HGRN_WIDTH + 2 * GMLP_WIDTH,
          4 * HGRN_WIDTH + 2 * GMLP_WIDTH + D_MODEL)
W_IN_COLS = 4 * HGRN_WIDTH + 2 * GMLP_WIDTH + 2 * D_MODEL
PEER_HEADS = 8
PEER_N_KEYS = 128
PEER_EXPERTS = PEER_N_KEYS * PEER_N_KEYS
PEER_TOPK = 16
PEER_QUERY_DIM = 256
PEER_HALF = PEER_QUERY_DIM // 2
PEER_BLOCK = 64
N_MOD = 6
EPS = 1e-6

kernel_name = 'hgrn2_chunkmlp_peer_adaln_step'


def rmsnorm(x, g):
    xf = x.astype(jnp.float32)
    r = lax.rsqrt(jnp.mean(xf * xf, axis=-1, keepdims=True) + EPS)
    return (xf * r).astype(x.dtype) * g


def hgrn2_chunked(q, k, v, logf, s0):
    B, T, H, _ = q.shape
    DV = v.shape[-1]
    C = min(HGRN_CHUNK, T)
    n = -(-T // C)
    pad = n * C - T

    def prep(a):
        a = jnp.pad(a, ((0, 0), (0, pad), (0, 0), (0, 0)))
        return a.reshape(B, n, C, H, a.shape[-1]).transpose(1, 0, 3, 2, 4)

    qc, kc, vc, fc = prep(q), prep(k), prep(v), prep(logf)
    causal = jnp.tril(jnp.ones((C, C), bool))[:, :, None]

    def step(S, inp):
        qb, kb, vb, fb = inp
        b = jnp.cumsum(fb, axis=2)
        diff = b[:, :, :, None, :] - b[:, :, None, :, :]
        decay = jnp.exp(jnp.where(causal, diff, -jnp.inf))
        scores = jnp.einsum('bhtd,bhtsd,bhsd->bhts', qb, decay, kb)
        o = jnp.einsum('bhts,bhsv->bhtv', scores, vb) + jnp.einsum('bhtd,bhdv->bhtv', qb * jnp.exp(b), S)
        b_last = b[:, :, -1:, :]
        S = jnp.exp(b_last[:, :, 0, :, None]) * S + jnp.einsum('bhsd,bhsv->bhdv', kb * jnp.exp(b_last - b), vb)
        return S, o

    S, o = lax.scan(step, s0, (qc, kc, vc, fc))
    o = o.transpose(1, 0, 3, 2, 4).reshape(B, n * C, H, DV)[:, :T]
    return o, S


def chunk_gmlp(u, v, w_s, b_s):
    B, T, _ = v.shape
    C = GMLP_CHUNK
    n = -(-T // C)
    vp = jnp.pad(v, ((0, 0), (0, n * C - T), (0, 0))).reshape(B, n, C, GMLP_GROUPS, GMLP_GROUP_DIM)
    w = w_s * jnp.tril(jnp.ones((C, C), w_s.dtype))
    z = jnp.einsum('gts,bnsgc->bntgc', w, vp) + b_s.T[None, None, :, :, None]
    z = z.reshape(B, n * C, GMLP_WIDTH)[:, :T]
    return u * z


def peer(h, w_q, sub_keys, u_tab, v_tab):
    T, D = h.shape
    q = (h @ w_q).reshape(T, PEER_HEADS, 2, PEER_HALF)
    s = jnp.einsum('thpd,hpnd->thpn', q, sub_keys)
    s1, i1 = lax.top_k(s[:, :, 0], PEER_TOPK)
    s2, i2 = lax.top_k(s[:, :, 1], PEER_TOPK)
    cand_s = (s1[..., :, None] + s2[..., None, :]).reshape(T, PEER_HEADS, PEER_TOPK * PEER_TOPK)
    cand_i = (i1[..., :, None] * PEER_N_KEYS + i2[..., None, :]).reshape(T, PEER_HEADS, PEER_TOPK * PEER_TOPK)
    top_s, pos = lax.top_k(cand_s, PEER_TOPK)
    idx = jnp.take_along_axis(cand_i, pos, axis=-1)
    g = jax.nn.softmax(top_s.astype(jnp.float32), axis=-1).astype(h.dtype)
    E = PEER_HEADS * PEER_TOPK
    nb = -(-T // PEER_BLOCK)
    pad = nb * PEER_BLOCK - T
    hb = jnp.pad(h, ((0, pad), (0, 0))).reshape(nb, PEER_BLOCK, D)
    ib = jnp.pad(idx.reshape(T, E), ((0, pad), (0, 0))).reshape(nb, PEER_BLOCK, E)
    gb = jnp.pad(g.reshape(T, E), ((0, pad), (0, 0))).reshape(nb, PEER_BLOCK, E)

    def block(args):
        hx, ix, gx = args
        act = jax.nn.gelu(jnp.einsum('bd,bed->be', hx, u_tab[ix]))
        return jnp.einsum('be,bed->bd', gx * act, v_tab[ix])

    out = lax.map(block, (hb, ib, gb))
    return out.reshape(nb * PEER_BLOCK, D)[:T]


def layer(x, c, s0, lb, w_ada, b_ada, g_norm1, w_in, g_hgrn_out, g_gmlp_v, w_spatial, b_spatial,
          w_branch_a, w_branch_b, w_out, g_norm2, w_peer_q, peer_sub_keys, peer_u, peer_v):
    B, T, D = x.shape
    mod = (jax.nn.silu(c) @ w_ada + b_ada).reshape(B, N_MOD, 1, D)
    shift1, scale1, gate1, shift2, scale2, gate2 = (mod[:, i] for i in range(N_MOD))

    h = rmsnorm(x, g_norm1) * (1 + scale1) + shift1
    proj = h @ w_in
    q, fp, inp, og, gu, gv, ga, gb = jnp.split(proj, SPLITS, axis=-1)

    def heads(a):
        return a.reshape(B, T, HGRN_HEADS, -1).astype(jnp.float32)

    lbh = lb.reshape(HGRN_HEADS, HGRN_KEY_DIM)
    f = lbh + (1.0 - lbh) * jax.nn.sigmoid(heads(fp))
    o, s_new = hgrn2_chunked(jax.nn.silu(heads(q)), 1.0 - f, heads(inp), jnp.log(f), s0.astype(jnp.float32))
    o = o * lax.rsqrt(jnp.mean(o * o, axis=-1, keepdims=True) + EPS)
    o = o.astype(x.dtype).reshape(B, T, HGRN_WIDTH) * g_hgrn_out
    y_a = (o * jax.nn.silu(og)) @ w_branch_a

    u = jax.nn.gelu(gu)
    v = rmsnorm(jax.nn.gelu(gv), g_gmlp_v)
    y_b = chunk_gmlp(u, v, w_spatial, b_spatial) @ w_branch_b

    mix = (jax.nn.sigmoid(ga) * y_a + jax.nn.sigmoid(gb) * y_b) @ w_out
    x = x + gate1 * mix

    h2 = rmsnorm(x, g_norm2) * (1 + scale2) + shift2
    x = x + gate2 * peer(h2.reshape(B * T, D), w_peer_q, peer_sub_keys, peer_u, peer_v).reshape(B, T, D)
    return x, s_new.astype(x.dtype), v


def decoder(x, c, s0, hgrn_lb_logits, w_ada, b_ada, g_norm1, w_in, g_hgrn_out, g_gmlp_v, w_spatial,
            b_spatial, w_branch_a, w_branch_b, w_out, g_norm2, w_peer_q, peer_sub_keys, peer_u, peer_v, g_final):
    lb_all = jnp.cumsum(jax.nn.softmax(hgrn_lb_logits.astype(jnp.float32), axis=0), axis=0)[:DEPTH]
    states, vrows = [], []
    for l in range(DEPTH):
        x, s, vr = layer(x, c, s0[l], lb_all[l], w_ada[l], b_ada[l], g_norm1[l], w_in[l], g_hgrn_out[l],
                         g_gmlp_v[l], w_spatial[l], b_spatial[l], w_branch_a[l], w_branch_b[l], w_out[l],
                         g_norm2[l], w_peer_q[l], peer_sub_keys[l], peer_u[l], peer_v[l])
        states.append(s)
        vrows.append(vr)
    return rmsnorm(x, g_final), jnp.stack(states), jnp.stack(vrows)


def setup_inputs(seed: int = 0) -> dict:
    key = jax.random.key(seed)
    ks = jax.random.split(key, 24)

    def nrm(k, shape, scale):
        return jax.random.normal(k, shape, jnp.float32) * scale

    def gain(k, shape):
        return 1.0 + nrm(k, shape, 0.02)

    return {
        'x_prompt': nrm(ks[0], (BATCH, SEQ, D_MODEL), 1.0),
        'x_sample': nrm(ks[1], (DEC_BATCH, DEC_SEQ, D_MODEL), 1.0),
        'c_prompt': nrm(ks[2], (BATCH, D_MODEL), 1.0),
        'c_sample': nrm(ks[3], (DEC_BATCH, D_MODEL), 1.0),
        'state_hgrn': nrm(ks[4], (DEPTH, DEC_BATCH, HGRN_HEADS, HGRN_KEY_DIM, HGRN_HEAD_DIM), 0.5),
        'w_ada': nrm(ks[5], (DEPTH, D_MODEL, N_MOD * D_MODEL), 0.5 * D_MODEL ** -0.5),
        'b_ada': nrm(ks[6], (DEPTH, N_MOD * D_MODEL), 0.02),
        'g_norm1': gain(ks[7], (DEPTH, D_MODEL)),
        'w_in': nrm(ks[8], (DEPTH, D_MODEL, W_IN_COLS), D_MODEL ** -0.5),
        'hgrn_lb_logits': nrm(ks[9], (DEPTH + 1, HGRN_WIDTH), 1.0),
        'g_hgrn_out': gain(ks[10], (DEPTH, HGRN_WIDTH)),
        'g_gmlp_v': gain(ks[11], (DEPTH, GMLP_WIDTH)),
        'w_spatial': nrm(ks[12], (DEPTH, GMLP_GROUPS, GMLP_CHUNK, GMLP_CHUNK), GMLP_CHUNK ** -0.5),
        'b_spatial': gain(ks[13], (DEPTH, GMLP_GROUPS, GMLP_CHUNK)),
        'w_branch_a': nrm(ks[14], (DEPTH, HGRN_WIDTH, D_MODEL), HGRN_WIDTH ** -0.5),
        'w_branch_b': nrm(ks[15], (DEPTH, GMLP_WIDTH, D_MODEL), GMLP_WIDTH ** -0.5),
        'w_out': nrm(ks[16], (DEPTH, D_MODEL, D_MODEL), D_MODEL ** -0.5),
        'g_norm2': gain(ks[17], (DEPTH, D_MODEL)),
        'w_peer_q': nrm(ks[18], (DEPTH, D_MODEL, PEER_HEADS * PEER_QUERY_DIM), D_MODEL ** -0.5),
        'peer_sub_keys': nrm(ks[19], (DEPTH, PEER_HEADS, 2, PEER_N_KEYS, PEER_HALF), PEER_HALF ** -0.5),
        'peer_u': nrm(ks[20], (DEPTH, PEER_EXPERTS, D_MODEL), D_MODEL ** -0.5),
        'peer_v': nrm(ks[21], (DEPTH, PEER_EXPERTS, D_MODEL), PEER_HEADS ** -0.5),
        'g_final': gain(ks[22], (D_MODEL,)),
    }


def reference(x_prompt, x_sample, c_prompt, c_sample, state_hgrn, w_ada, b_ada, g_norm1, w_in,
              hgrn_lb_logits, g_hgrn_out, g_gmlp_v, w_spatial, b_spatial, w_branch_a, w_branch_b, w_out,
              g_norm2, w_peer_q, peer_sub_keys, peer_u, peer_v, g_final):
    s0_prompt = jnp.zeros((DEPTH, x_prompt.shape[0], HGRN_HEADS, HGRN_KEY_DIM, HGRN_HEAD_DIM), x_prompt.dtype)
    y_prompt, state_hgrn_prompt, _ = decoder(
        x_prompt, c_prompt, s0_prompt, hgrn_lb_logits, w_ada, b_ada, g_norm1, w_in, g_hgrn_out, g_gmlp_v,
        w_spatial, b_spatial, w_branch_a, w_branch_b, w_out, g_norm2, w_peer_q, peer_sub_keys, peer_u, peer_v,
        g_final)
    y_sample, state_hgrn_sample, chunk_v_sample = decoder(
        x_sample, c_sample, state_hgrn, hgrn_lb_logits, w_ada, b_ada, g_norm1, w_in, g_hgrn_out, g_gmlp_v,
        w_spatial, b_spatial, w_branch_a, w_branch_b, w_out, g_norm2, w_peer_q, peer_sub_keys, peer_u, peer_v,
        g_final)
    return (y_prompt, y_sample, state_hgrn_prompt, state_hgrn_sample, chunk_v_sample)
```

```python
import functools
import math

import numpy as np
import jax
import jax.numpy as jnp
from jax import lax
from jax.experimental import pallas as pl
from jax.experimental.pallas import tpu as pltpu

EPS = 1e-6
N_MOD = 6
PEER_TOPK = 16
LANE_V7X = 128
VMEM_LIMIT_V7X = 56 * 1024 * 1024
F32 = jnp.float32
BF16 = jnp.bfloat16
NEG_INF = float("-inf")


def _params(*sem):
    return pltpu.CompilerParams(dimension_semantics=sem, vmem_limit_bytes=VMEM_LIMIT_V7X)


def _pow2_tile(target, *sizes):
    g = 0
    for s in sizes:
        g = math.gcd(g, s)
    t = 1
    while t * 2 <= target and g % (t * 2) == 0:
        t *= 2
    return t


def _silu(x):
    return x * jax.nn.sigmoid(x)


def _ada_kernel(c_ref, w_ref, b_ref, o_ref):
    s = _silu(c_ref[...]).astype(BF16)
    o_ref[...] = jnp.dot(s, w_ref[...].astype(BF16), preferred_element_type=F32) + b_ref[...]


def _ada(c_all, w_ada, b_ada):
    m, d = c_all.shape
    n = w_ada.shape[1]
    tn = _pow2_tile(512, n)
    return pl.pallas_call(
        _ada_kernel,
        grid=(n // tn,),
        in_specs=[pl.BlockSpec((m, d), lambda j: (0, 0)),
                  pl.BlockSpec((d, tn), lambda j: (0, j)),
                  pl.BlockSpec((1, tn), lambda j: (0, j))],
        out_specs=pl.BlockSpec((m, tn), lambda j: (0, j)),
        out_shape=jax.ShapeDtypeStruct((m, n), F32),
        compiler_params=_params("arbitrary"),
        name="ada_mod",
    )(c_all, w_ada, b_ada.reshape(1, n))


def _norm_mod_kernel(x_ref, g_ref, scp_ref, shp_ref, scs_ref, shs_ref, o_ref, *, n_prompt_tiles):
    i = pl.program_id(0)
    x = x_ref[...]
    r = lax.rsqrt(jnp.mean(x * x, axis=-1, keepdims=True) + EPS)
    xn = x * r * g_ref[...]

    @pl.when(i < n_prompt_tiles)
    def _():
        o_ref[...] = (xn * (1.0 + scp_ref[0]) + shp_ref[0]).astype(o_ref.dtype)

    @pl.when(i >= n_prompt_tiles)
    def _():
        o_ref[...] = (xn * (1.0 + scs_ref[...]) + shs_ref[...]).astype(o_ref.dtype)


def _norm_mod(x, g, mod_p, mod_s, k_scale, k_shift, n_prompt, seq):
    n, d = x.shape
    tm = _pow2_tile(256, seq, n - n_prompt)
    npt = n_prompt // tm
    per_b = seq // tm

    def pidx(k):
        return lambda i: (jnp.minimum(i, npt - 1) // per_b, 0, k)

    def sidx(k):
        return lambda i: (jnp.maximum(i - npt, 0), k)

    return pl.pallas_call(
        functools.partial(_norm_mod_kernel, n_prompt_tiles=npt),
        grid=(n // tm,),
        in_specs=[pl.BlockSpec((tm, d), lambda i: (i, 0)),
                  pl.BlockSpec((1, d), lambda i: (0, 0)),
                  pl.BlockSpec((1, 1, d), pidx(k_scale)),
                  pl.BlockSpec((1, 1, d), pidx(k_shift)),
                  pl.BlockSpec((tm, d), sidx(k_scale)),
                  pl.BlockSpec((tm, d), sidx(k_shift))],
        out_specs=pl.BlockSpec((tm, d), lambda i: (i, 0)),
        out_shape=jax.ShapeDtypeStruct((n, d), BF16),
        compiler_params=_params("arbitrary"),
        name="norm_mod",
    )(x, g.reshape(1, d), mod_p, mod_p, mod_s, mod_s)


def _mm_kernel(a_ref, w_ref, o_ref):
    o_ref[...] = jnp.dot(a_ref[...], w_ref[...], preferred_element_type=F32).astype(o_ref.dtype)


def _matmul(a, w, out_dtype, n_prompt, name):
    n, k = a.shape
    nc = w.shape[1]
    tm = _pow2_tile(512, n_prompt, n - n_prompt)
    tn = _pow2_tile(1024, nc)
    return pl.pallas_call(
        _mm_kernel,
        grid=(nc // tn, n // tm),
        in_specs=[pl.BlockSpec((tm, k), lambda j, i: (i, 0)),
                  pl.BlockSpec((k, tn), lambda j, i: (0, j))],
        out_specs=pl.BlockSpec((tm, tn), lambda j, i: (i, j)),
        out_shape=jax.ShapeDtypeStruct((n, nc), out_dtype),
        compiler_params=_params("arbitrary", "arbitrary"),
        name=name,
    )(a, w)


def _hgrn_tables(c):
    nl = int(math.log2(c))
    r = np.arange(c)[:, None]
    j = np.arange(c)[None, :]
    mats = []
    lvl = np.full((c, c), -1, np.int32)
    for l in range(nl):
        m = c >> (l + 1)
        mid = (r // (2 * m)) * 2 * m + m
        upper = r >= mid
        t = np.where(upper, (j >= mid) & (j <= r), (j > r) & (j <= mid - 1))
        mats.append(t)
        same = (r // (2 * m)) == (j // (2 * m))
        lvl = np.where(same & upper & (j < mid), l, lvl)
    lvl = np.where(r == j, nl, lvl)
    mats.append(j <= r)
    mats.append(j > r)
    t_all = np.concatenate(mats, axis=0).astype(np.float32)
    return jnp.asarray(t_all, BF16), jnp.asarray(lvl), nl


def _hgrn_prompt_kernel(q_ref, f_ref, v_ref, og_ref, lb_ref, g_ref, tall_ref, lvl_ref,
                        a_ref, st_ref, s_scr, *, chunk, n_levels, n_chunks):
    c = chunk
    s_scr[...] = jnp.zeros_like(s_scr)
    lb = lb_ref[...]
    lvl = lvl_ref[...]
    rows = lax.broadcasted_iota(jnp.int32, (c, q_ref.shape[1]), 0)

    def body(ci, carry):
        sl = pl.ds(pl.multiple_of(ci * c, c), c)
        qs = _silu(q_ref[sl, :])
        f = lb + (1.0 - lb) * jax.nn.sigmoid(f_ref[sl, :])
        k = 1.0 - f
        lf = jnp.log(f)
        v = v_ref[sl, :].astype(BF16)
        lf_hi = lf.astype(BF16)
        lf_lo = (lf - lf_hi.astype(F32)).astype(BF16)
        ex = (jnp.dot(tall_ref[...], lf_hi, preferred_element_type=F32)
              + jnp.dot(tall_ref[...], lf_lo, preferred_element_type=F32))
        qs_b = qs.astype(BF16)
        k_b = k.astype(BF16)
        nt = (((1,), (1,)), ((), ()))
        scores = jnp.where(lvl == n_levels,
                           lax.dot_general(qs_b, k_b, nt, preferred_element_type=F32), 0.0)
        for l in range(n_levels):
            m = c >> (l + 1)
            e_l = jnp.exp(ex[l * c:(l + 1) * c, :])
            y = (e_l * jnp.where((rows & m) != 0, qs, k)).astype(BF16)
            p = lax.dot_general(y, y, nt, preferred_element_type=F32)
            scores = scores + jnp.where(lvl == l, p, 0.0)
        bcum = ex[n_levels * c:(n_levels + 1) * c, :]
        brev = ex[(n_levels + 1) * c:(n_levels + 2) * c, :]
        st = s_scr[...]
        o = jnp.dot(scores.astype(BF16), v, preferred_element_type=F32)
        o = o + lax.dot_general((qs * jnp.exp(bcum)).astype(BF16), st.astype(BF16), nt,
                                preferred_element_type=F32)
        kh = (k * jnp.exp(brev)).astype(BF16)
        kv_t = lax.dot_general(v, kh, (((0,), (0,)), ((), ())), preferred_element_type=F32)
        s_scr[...] = st * jnp.exp(bcum[c - 1:c, :]) + kv_t
        on = o * lax.rsqrt(jnp.mean(o * o, axis=-1, keepdims=True) + EPS)
        a_ref[sl, :] = (on * g_ref[...] * _silu(og_ref[sl, :])).astype(a_ref.dtype)
        return carry

    lax.fori_loop(0, n_chunks, body, 0)
    st_ref[0, 0, 0] = s_scr[...]


def _hgrn_prompt(proj, lb, g_out, batch, seq, heads, dk, n_rows):
    c = _pow2_tile(128, seq)
    t_all, lvl, nl = _hgrn_tables(c)
    col = lambda off: (lambda b, h: (b, off + h))
    kern = functools.partial(_hgrn_prompt_kernel, chunk=c, n_levels=nl, n_chunks=seq // c)
    return pl.pallas_call(
        kern,
        grid=(batch, heads),
        in_specs=[pl.BlockSpec((seq, dk), col(0)),
                  pl.BlockSpec((seq, dk), col(heads)),
                  pl.BlockSpec((seq, dk), col(2 * heads)),
                  pl.BlockSpec((seq, dk), col(3 * heads)),
                  pl.BlockSpec((1, dk), lambda b, h: (0, h)),
                  pl.BlockSpec((1, dk), lambda b, h: (0, h)),
                  pl.BlockSpec(t_all.shape, lambda b, h: (0, 0)),
                  pl.BlockSpec(lvl.shape, lambda b, h: (0, 0))],
        out_specs=[pl.BlockSpec((seq, dk), lambda b, h: (b, h)),
                   pl.BlockSpec((1, 1, 1, dk, dk), lambda b, h: (0, b, h, 0, 0))],
        out_shape=[jax.ShapeDtypeStruct((n_rows, heads * dk), BF16),
                   jax.ShapeDtypeStruct((1, batch, heads, dk, dk), F32)],
        scratch_shapes=[pltpu.VMEM((dk, dk), F32)],
        compiler_params=_params("arbitrary", "arbitrary"),
        name="hgrn_prompt",
    )(proj, proj, proj, proj, lb, g_out, t_all, lvl)


def _hgrn_sample_kernel(q_ref, f_ref, v_ref, og_ref, lb_ref, g_ref, s0_ref, a_ref, s_ref, *, bt, steps):
    lb = lb_ref[...]
    g = g_ref[...]
    rows = lax.broadcasted_iota(jnp.int32, (steps, q_ref.shape[2]), 0)
    nt = (((1,), (1,)), ((), ()))

    def body(b, carry):
        qs = _silu(q_ref[b])
        f = lb + (1.0 - lb) * jax.nn.sigmoid(f_ref[b])
        k = 1.0 - f
        lf = jnp.log(f)
        v = v_ref[b]
        cum = []
        run = None
        for t in range(steps):
            run = lf[t:t + 1, :] if run is None else run + lf[t:t + 1, :]
            cum.append(run)
        bmat = jnp.broadcast_to(cum[0], lf.shape)
        for t in range(1, steps):
            bmat = jnp.where(rows == t, cum[t], bmat)
        o = jnp.zeros_like(qs)
        for s in range(steps):
            x = jnp.where(rows >= s, qs * k[s:s + 1, :] * jnp.exp(jnp.minimum(bmat - cum[s], 0.0)), 0.0)
            o = o + jnp.sum(x, axis=-1, keepdims=True) * v[s:s + 1, :]
        st = s0_ref[0, b, 0].T
        o = o + lax.dot_general((qs * jnp.exp(bmat)).astype(BF16), st.astype(BF16), nt,
                                preferred_element_type=F32)
        kh = (k * jnp.exp(cum[steps - 1] - bmat)).astype(BF16)
        kv_t = lax.dot_general(v.astype(BF16), kh, (((0,), (0,)), ((), ())), preferred_element_type=F32)
        s_ref[0, b, 0] = (st * jnp.exp(cum[steps - 1]) + kv_t).T
        on = o * lax.rsqrt(jnp.mean(o * o, axis=-1, keepdims=True) + EPS)
        a_ref[b] = on * g * _silu(og_ref[b])
        return carry

    lax.fori_loop(0, bt, body, 0)


def _hgrn_sample(proj_s, lb, g_out, s0, heads, dk):
    db, steps, _ = proj_s.shape
    bt = _pow2_tile(32, db)
    col = lambda off: (lambda j, h: (j, 0, off + h))
    kern = functools.partial(_hgrn_sample_kernel, bt=bt, steps=steps)
    return pl.pallas_call(
        kern,
        grid=(db // bt, heads),
        in_specs=[pl.BlockSpec((bt, steps, dk), col(0)),
                  pl.BlockSpec((bt, steps, dk), col(heads)),
                  pl.BlockSpec((bt, steps, dk), col(2 * heads)),
                  pl.BlockSpec((bt, steps, dk), col(3 * heads)),
                  pl.BlockSpec((1, dk), lambda j, h: (0, h)),
                  pl.BlockSpec((1, dk), lambda j, h: (0, h)),
                  pl.BlockSpec((1, bt, 1, dk, dk), lambda j, h: (0, j, h, 0, 0))],
        out_specs=[pl.BlockSpec((bt, steps, dk), lambda j, h: (j, 0, h)),
                   pl.BlockSpec((1, bt, 1, dk, dk), lambda j, h: (0, j, h, 0, 0))],
        out_shape=[jax.ShapeDtypeStruct((db, steps, heads * dk), F32),
                   jax.ShapeDtypeStruct(s0.shape, F32)],
        compiler_params=_params("arbitrary", "arbitrary"),
        name="hgrn_sample",
    )(proj_s, proj_s, proj_s, proj_s, lb, g_out, s0)


def _gmlp_kernel(gu_ref, gv_ref, w_ref, bias_ref, g_ref, o_ref, v_ref, *, groups, gd):
    u = jax.nn.gelu(gu_ref[...])
    vv = jax.nn.gelu(gv_ref[...])
    r = lax.rsqrt(jnp.mean(vv * vv, axis=-1, keepdims=True) + EPS)
    v = vv * r * g_ref[...]
    v_ref[...] = v
    for gi in range(groups):
        sl = slice(gi * gd, (gi + 1) * gd)
        z = jnp.dot(w_ref[0, gi], v[:, sl].astype(BF16), preferred_element_type=F32) + bias_ref[0, :, sl]
        o_ref[:, sl] = (u[:, sl] * z).astype(o_ref.dtype)


def _gmlp(proj, w_st, bias_st, g_v, n_prompt, gu_blk, gv_blk):
    n = proj.shape[0]
    _, groups, c, _ = w_st.shape
    gw = bias_st.shape[2]
    npc = n_prompt // c
    sel = lambda i: jnp.where(i >= npc, 1, 0)
    kern = functools.partial(_gmlp_kernel, groups=groups, gd=gw // groups)
    return pl.pallas_call(
        kern,
        grid=(n // c,),
        in_specs=[pl.BlockSpec((c, gw), lambda i: (i, gu_blk)),
                  pl.BlockSpec((c, gw), lambda i: (i, gv_blk)),
                  pl.BlockSpec((1, groups, c, c), lambda i: (sel(i), 0, 0, 0)),
                  pl.BlockSpec((1, c, gw), lambda i: (sel(i), 0, 0)),
                  pl.BlockSpec((1, gw), lambda i: (0, 0))],
        out_specs=[pl.BlockSpec((c, gw), lambda i: (i, 0)),
                   pl.BlockSpec((c, gw), lambda i: (jnp.maximum(i - npc, 0), 0))],
        out_shape=[jax.ShapeDtypeStruct((n, gw), BF16),
                   jax.ShapeDtypeStruct((n - n_prompt, gw), F32)],
        compiler_params=_params("arbitrary"),
        name="chunk_mlp",
    )(proj, proj, w_st, bias_st, g_v.reshape(1, gw))


def _branch_kernel(a_ref, b_ref, wa_ref, wb_ref, ga_ref, gb_ref, o_ref):
    ya = jnp.dot(a_ref[...], wa_ref[...], preferred_element_type=F32)
    yb = jnp.dot(b_ref[...], wb_ref[...], preferred_element_type=F32)
    o_ref[...] = (jax.nn.sigmoid(ga_ref[...]) * ya + jax.nn.sigmoid(gb_ref[...]) * yb).astype(o_ref.dtype)


def _branches(a, bm, wa, wb, proj, ga_off, gb_off, n_prompt):
    n, ka = a.shape
    kb = bm.shape[1]
    d = wa.shape[1]
    tm = _pow2_tile(512, n_prompt, n - n_prompt)
    tn = _pow2_tile(1024, d, ga_off, gb_off)
    return pl.pallas_call(
        _branch_kernel,
        grid=(d // tn, n // tm),
        in_specs=[pl.BlockSpec((tm, ka), lambda j, i: (i, 0)),
                  pl.BlockSpec((tm, kb), lambda j, i: (i, 0)),
                  pl.BlockSpec((ka, tn), lambda j, i: (0, j)),
                  pl.BlockSpec((kb, tn), lambda j, i: (0, j)),
                  pl.BlockSpec((tm, tn), lambda j, i: (i, ga_off // tn + j)),
                  pl.BlockSpec((tm, tn), lambda j, i: (i, gb_off // tn + j))],
        out_specs=pl.BlockSpec((tm, tn), lambda j, i: (i, j)),
        out_shape=jax.ShapeDtypeStruct((n, d), BF16),
        compiler_params=_params("arbitrary", "arbitrary"),
        name="branches",
    )(a, bm, wa, wb, proj, proj)


def _out_kernel(m_ref, w_ref, x_ref, gp_ref, gs_ref, o_ref, *, n_prompt_tiles):
    i = pl.program_id(1)
    y = jnp.dot(m_ref[...], w_ref[...], preferred_element_type=F32)

    @pl.when(i < n_prompt_tiles)
    def _():
        o_ref[...] = x_ref[...] + gp_ref[0] * y

    @pl.when(i >= n_prompt_tiles)
    def _():
        o_ref[...] = x_ref[...] + gs_ref[...] * y


def _out_proj(mix, w, x, mod_p, mod_s, k_gate, n_prompt, seq):
    n, d = x.shape
    tm = _pow2_tile(512, seq, n - n_prompt)
    tn = _pow2_tile(1024, d)
    npt = n_prompt // tm
    per_b = seq // tm
    nj = d // tn
    return pl.pallas_call(
        functools.partial(_out_kernel, n_prompt_tiles=npt),
        grid=(nj, n // tm),
        in_specs=[pl.BlockSpec((tm, d), lambda j, i: (i, 0)),
                  pl.BlockSpec((d, tn), lambda j, i: (0, j)),
                  pl.BlockSpec((tm, tn), lambda j, i: (i, j)),
                  pl.BlockSpec((1, 1, tn), lambda j, i: (jnp.minimum(i, npt - 1) // per_b, 0, k_gate * nj + j)),
                  pl.BlockSpec((tm, tn), lambda j, i: (jnp.maximum(i - npt, 0), k_gate * nj + j))],
        out_specs=pl.BlockSpec((tm, tn), lambda j, i: (i, j)),
        out_shape=jax.ShapeDtypeStruct((n, d), F32),
        compiler_params=_params("arbitrary", "arbitrary"),
        name="out_proj",
    )(mix, w, x, mod_p, mod_s)


def _cand_pairs(k):
    return [(i, j) for i in range(k) for j in range(k) if (i + 1) * (j + 1) <= k]


def _top_rows(s, n_iota, k):
    rank = jnp.full(s.shape, float(k), F32)
    vals = []
    big = float(s.shape[0])
    for i in range(k):
        m = jnp.max(s, axis=0, keepdims=True)
        idx = jnp.min(jnp.where(s == m, n_iota, big), axis=0, keepdims=True)
        sel = n_iota == idx
        rank = jnp.where(sel, float(i), rank)
        s = jnp.where(sel, NEG_INF, s)
        vals.append(m)
    return vals, rank


def _peer_topk_kernel(q_ref, sk_ref, jsum_ref, eaz_ref, jd_ref, eb_ref, r2_ref, cand_scr, *, half, pairs, k):
    nt = (((1,), (1,)), ((), ()))
    q = q_ref[...].astype(BF16)
    s1 = lax.dot_general(sk_ref[0, 0], q[:, :half], nt, preferred_element_type=F32)
    s2 = lax.dot_general(sk_ref[0, 1], q[:, half:], nt, preferred_element_type=F32)
    n_iota = lax.broadcasted_iota(jnp.int32, s1.shape, 0).astype(F32)
    a, r1 = _top_rows(s1, n_iota, k)
    b, r2 = _top_rows(s2, n_iota, k)

    cand_scr[...] = jnp.full(cand_scr.shape, NEG_INF, F32)
    for p, (i, j) in enumerate(pairs):
        cand_scr[p:p + 1, :] = a[i] + b[j]
    c0 = cand_scr[...]
    p_iota = lax.broadcasted_iota(jnp.int32, c0.shape, 0).astype(F32)
    c = c0
    taken = jnp.zeros(c0.shape, F32)
    big = float(c0.shape[0])
    for _ in range(k):
        m = jnp.max(c, axis=0, keepdims=True)
        idx = jnp.min(jnp.where(c == m, p_iota, big), axis=0, keepdims=True)
        sel = p_iota == idx
        taken = jnp.where(sel, 1.0, taken)
        c = jnp.where(sel, NEG_INF, c)
    w = jnp.where(taken > 0.0, jnp.exp(c0 - (a[0] + b[0])), 0.0)
    z = jnp.sum(w, axis=0, keepdims=True)
    jcnt = jnp.dot(jsum_ref[...], taken.astype(BF16), preferred_element_type=F32)

    inv_z = 1.0 / z
    eaz = jnp.zeros(s1.shape, F32)
    jd = jnp.zeros(s1.shape, F32)
    eb = jnp.zeros(s1.shape, F32)
    for i in range(k):
        hit1 = r1 == float(i)
        eaz = jnp.where(hit1, jnp.exp(a[i] - a[0]) * inv_z, eaz)
        jd = jnp.where(hit1, jcnt[i:i + 1, :], jd)
        eb = jnp.where(r2 == float(i), jnp.exp(b[i] - b[0]), eb)
    eaz_ref[0] = eaz
    jd_ref[0] = jd
    eb_ref[0] = eb
    r2_ref[0] = r2


def _peer_topk(qp, sub_keys_b, n_prompt):
    n = qp.shape[0]
    heads, _, n_keys, half = sub_keys_b.shape
    tt = _pow2_tile(256, n_prompt, n - n_prompt)
    pairs = _cand_pairs(PEER_TOPK)
    n_cand = -(-len(pairs) // 64) * 64
    jsum = np.zeros((PEER_TOPK, n_cand), np.float32)
    for p, (i, _) in enumerate(pairs):
        jsum[i, p] = 1.0
    kern = functools.partial(_peer_topk_kernel, half=half, pairs=pairs, k=PEER_TOPK)
    maps = jax.ShapeDtypeStruct((heads, n_keys, n), F32)
    mspec = pl.BlockSpec((1, n_keys, tt), lambda i, h: (h, 0, i))
    return pl.pallas_call(
        kern,
        grid=(n // tt, heads),
        in_specs=[pl.BlockSpec((tt, 2 * half), lambda i, h: (i, h)),
                  pl.BlockSpec((1, 2, n_keys, half), lambda i, h: (h, 0, 0, 0)),
                  pl.BlockSpec(jsum.shape, lambda i, h: (0, 0))],
        out_specs=[mspec, mspec, mspec, mspec],
        out_shape=[maps, maps, maps, maps],
        scratch_shapes=[pltpu.VMEM((n_cand, tt), F32)],
        compiler_params=_params("arbitrary", "arbitrary"),
        name="peer_topk",
    )(qp, sub_keys_b, jnp.asarray(jsum, BF16))


def _peer_main_kernel(h_ref, u_ref, v_ref, eaz_ref, jd_ref, eb_ref, r2_ref, o_ref, ga_scr, *, heads, n_keys, rows_per_step):
    e = pl.program_id(1)

    @pl.when(e == 0)
    def _():
        o_ref[...] = jnp.zeros_like(o_ref)

    act_t = lax.dot_general(u_ref[...], h_ref[...], (((1,), (1,)), ((), ())),
                            preferred_element_type=F32)
    for r in range(rows_per_step):
        n1 = e * rows_per_step + r
        g = None
        for h in range(heads):
            ea = eaz_ref[h, pl.ds(n1, 1), :]
            jd = jd_ref[h, pl.ds(n1, 1), :]
            t = ea * jnp.where(r2_ref[h] < jd, eb_ref[h], 0.0)
            g = t if g is None else g + t
        sl = slice(r * n_keys, (r + 1) * n_keys)
        ga_scr[sl, :] = (g * jax.nn.gelu(act_t[sl, :])).astype(BF16)
    o_ref[...] += lax.dot_general(ga_scr[...], v_ref[...], (((0,), (0,)), ((), ())),
                                  preferred_element_type=F32)


def _peer_main(h2, u_b, v_b, maps, n_prompt):
    n, d = h2.shape
    n_exp = u_b.shape[0]
    heads, n_keys, _ = maps[0].shape
    tt = _pow2_tile(512, n_prompt, n - n_prompt)
    ec = _pow2_tile(256, n_exp)
    rps = ec // n_keys
    kern = functools.partial(_peer_main_kernel, heads=heads, n_keys=n_keys, rows_per_step=rps)
    mspec = pl.BlockSpec((heads, n_keys, tt), lambda i, e: (0, 0, i))
    return pl.pallas_call(
        kern,
        grid=(n // tt, n_exp // ec),
        in_specs=[pl.BlockSpec((tt, d), lambda i, e: (i, 0)),
                  pl.BlockSpec((ec, d), lambda i, e: (e, 0)),
                  pl.BlockSpec((ec, d), lambda i, e: (e, 0)),
                  mspec, mspec, mspec, mspec],
        out_specs=pl.BlockSpec((tt, d), lambda i, e: (i, 0)),
        out_shape=jax.ShapeDtypeStruct((n, d), F32),
        scratch_shapes=[pltpu.VMEM((ec, tt), BF16)],
        compiler_params=_params("arbitrary", "arbitrary"),
        name="peer_main",
    )(h2, u_b, v_b, *maps)


def _final_kernel(x_ref, p_ref, gp_ref, gs_ref, g_ref, yp_ref, ys_ref, *, n_prompt_tiles):
    i = pl.program_id(0)

    def norm(x):
        return x * lax.rsqrt(jnp.mean(x * x, axis=-1, keepdims=True) + EPS) * g_ref[...]

    @pl.when(i < n_prompt_tiles)
    def _():
        yp_ref[...] = norm(x_ref[...] + gp_ref[0] * p_ref[...])

    @pl.when(i >= n_prompt_tiles)
    def _():
        ys_ref[...] = norm(x_ref[...] + gs_ref[...] * p_ref[...])


def _final(x1, peer_out, mod_p, mod_s, k_gate, g_final, n_prompt, seq):
    n, d = x1.shape
    tm = _pow2_tile(256, seq, n - n_prompt)
    npt = n_prompt // tm
    per_b = seq // tm
    return pl.pallas_call(
        functools.partial(_final_kernel, n_prompt_tiles=npt),
        grid=(n // tm,),
        in_specs=[pl.BlockSpec((tm, d), lambda i: (i, 0)),
                  pl.BlockSpec((tm, d), lambda i: (i, 0)),
                  pl.BlockSpec((1, 1, d), lambda i: (jnp.minimum(i, npt - 1) // per_b, 0, k_gate)),
                  pl.BlockSpec((tm, d), lambda i: (jnp.maximum(i - npt, 0), k_gate)),
                  pl.BlockSpec((1, d), lambda i: (0, 0))],
        out_specs=[pl.BlockSpec((tm, d), lambda i: (jnp.minimum(i, npt - 1), 0)),
                   pl.BlockSpec((tm, d), lambda i: (jnp.maximum(i - npt, 0), 0))],
        out_shape=[jax.ShapeDtypeStruct((n_prompt, d), F32),
                   jax.ShapeDtypeStruct((n - n_prompt, d), F32)],
        compiler_params=_params("arbitrary"),
        name="final_norm",
    )(x1, peer_out, mod_p, mod_s, g_final.reshape(1, d))


def kernel(x_prompt, x_sample, c_prompt, c_sample, state_hgrn, w_ada, b_ada, g_norm1, w_in, hgrn_lb_logits,
           g_hgrn_out, g_gmlp_v, w_spatial, b_spatial, w_branch_a, w_branch_b, w_out, g_norm2, w_peer_q,
           peer_sub_keys, peer_u, peer_v, g_final):
    batch, seq, d = x_prompt.shape
    db, dt, _ = x_sample.shape
    depth, _, heads, dk, dv = state_hgrn.shape
    assert depth == 1 and dk == dv == LANE_V7X
    hw = heads * dk
    groups, gc = w_spatial.shape[1], w_spatial.shape[2]
    gw = w_branch_b.shape[1]
    assert gc == LANE_V7X and gw // groups == LANE_V7X and dt <= gc and gc % dt == 0
    n_prompt, n_sample = batch * seq, db * dt
    n = n_prompt + n_sample

    c_all = jnp.concatenate([c_prompt, c_sample], axis=0)
    pad = (-c_all.shape[0]) % 8
    c_all = jnp.pad(c_all, ((0, pad), (0, 0)))
    mod = _ada(c_all, w_ada[0], b_ada[0])
    mod_p = mod[:batch].reshape(batch, 1, N_MOD * d)
    mod_s = jnp.repeat(mod[batch:batch + db], dt, axis=0)

    x = jnp.concatenate([x_prompt.reshape(n_prompt, d), x_sample.reshape(n_sample, d)], axis=0)

    h1 = _norm_mod(x, g_norm1[0], mod_p, mod_s, 1, 0, n_prompt, seq)
    proj = _matmul(h1, w_in[0].astype(BF16), F32, n_prompt, "in_proj")

    lb = jnp.cumsum(jax.nn.softmax(hgrn_lb_logits.astype(F32), axis=0), axis=0)[0].reshape(1, hw)
    g_ho = g_hgrn_out[0].reshape(1, hw)
    a_p, st_p = _hgrn_prompt(proj, lb, g_ho, batch, seq, heads, dk, n_prompt)
    proj_s = proj[n_prompt:].reshape(db, dt, proj.shape[1])
    a_s, st_s = _hgrn_sample(proj_s, lb, g_ho, state_hgrn, heads, dk)
    a_all = jnp.concatenate([a_p, a_s.reshape(n_sample, hw).astype(BF16)], axis=0)

    tril = jnp.tril(jnp.ones((gc, gc), F32))
    w_sp = w_spatial[0]
    blk = jnp.arange(gc) // dt
    w_samp = jnp.tile(w_sp[:, :dt, :dt], (1, gc // dt, gc // dt)) * (blk[:, None] == blk[None, :])
    w_st = jnp.stack([w_sp * tril, w_samp * tril]).astype(BF16)
    bias_full = jnp.repeat(b_spatial[0].T, gw // groups, axis=1)
    bias_st = jnp.stack([bias_full, jnp.tile(bias_full[:dt], (gc // dt, 1))])
    gu_blk = 4 * hw // gw
    assert gu_blk * gw == 4 * hw
    bm, v_s = _gmlp(proj, w_st, bias_st, g_gmlp_v[0], n_prompt, gu_blk, gu_blk + 1)

    ga_off = 4 * hw + 2 * gw
    mix = _branches(a_all, bm, w_branch_a[0].astype(BF16), w_branch_b[0].astype(BF16), proj,
                    ga_off, ga_off + d, n_prompt)
    x1 = _out_proj(mix, w_out[0].astype(BF16), x, mod_p, mod_s, 2, n_prompt, seq)

    h2 = _norm_mod(x1, g_norm2[0], mod_p, mod_s, 4, 3, n_prompt, seq)
    qp = _matmul(h2, w_peer_q[0].astype(BF16), F32, n_prompt, "peer_query")
    maps = _peer_topk(qp, peer_sub_keys[0].astype(BF16), n_prompt)
    peer_out = _peer_main(h2, peer_u[0].astype(BF16), peer_v[0].astype(BF16), maps, n_prompt)
    y_p, y_s = _final(x1, peer_out, mod_p, mod_s, 5, g_final, n_prompt, seq)

    state_p = jnp.swapaxes(st_p, -1, -2)
    return (y_p.reshape(batch, seq, d), y_s.reshape(db, dt, d), state_p, st_s,
            v_s.reshape(1, db, dt, gw))
```

```python
import functools
import math

import numpy as np
import jax
import jax.numpy as jnp
from jax import lax
from jax.experimental import pallas as pl
from jax.experimental.pallas import tpu as pltpu

EPS = 1e-6
N_MOD = 6
PEER_TOPK = 16
LANE_V7X = 128
VMEM_LIMIT_V7X = 56 * 1024 * 1024
F32 = jnp.float32
BF16 = jnp.bfloat16
NEG_INF = float("-inf")


def _params(*sem):
    return pltpu.CompilerParams(dimension_semantics=sem, vmem_limit_bytes=VMEM_LIMIT_V7X)


def _pow2_tile(target, *sizes):
    g = 0
    for s in sizes:
        g = math.gcd(g, s)
    t = 1
    while t * 2 <= target and g % (t * 2) == 0:
        t *= 2
    return t


def _silu(x):
    return x * jax.nn.sigmoid(x)


def _ada_kernel(c_ref, w_ref, b_ref, o_ref):
    s = _silu(c_ref[...]).astype(BF16)
    o_ref[...] = jnp.dot(s, w_ref[...].astype(BF16), preferred_element_type=F32) + b_ref[...]


def _ada(c_all, w_ada, b_ada):
    m, d = c_all.shape
    n = w_ada.shape[1]
    tn = _pow2_tile(512, n)
    return pl.pallas_call(
        _ada_kernel,
        grid=(n // tn,),
        in_specs=[pl.BlockSpec((m, d), lambda j: (0, 0)),
                  pl.BlockSpec((d, tn), lambda j: (0, j)),
                  pl.BlockSpec((1, tn), lambda j: (0, j))],
        out_specs=pl.BlockSpec((m, tn), lambda j: (0, j)),
        out_shape=jax.ShapeDtypeStruct((m, n), F32),
        compiler_params=_params("arbitrary"),
        name="ada_mod",
    )(c_all, w_ada, b_ada.reshape(1, n))


def _norm_mod_kernel(x_ref, g_ref, scp_ref, shp_ref, scs_ref, shs_ref, o_ref, *, n_prompt_tiles):
    i = pl.program_id(0)
    x = x_ref[...]
    r = lax.rsqrt(jnp.mean(x * x, axis=-1, keepdims=True) + EPS)
    xn = x * r * g_ref[...]

    @pl.when(i < n_prompt_tiles)
    def _():
        o_ref[...] = (xn * (1.0 + scp_ref[0]) + shp_ref[0]).astype(o_ref.dtype)

    @pl.when(i >= n_prompt_tiles)
    def _():
        o_ref[...] = (xn * (1.0 + scs_ref[...]) + shs_ref[...]).astype(o_ref.dtype)


def _norm_mod(x, g, mod_p, mod_s, k_scale, k_shift, n_prompt, seq):
    n, d = x.shape
    tm = _pow2_tile(256, seq, n - n_prompt)
    npt = n_prompt // tm
    per_b = seq // tm

    def pidx(k):
        return lambda i: (jnp.minimum(i, npt - 1) // per_b, 0, k)

    def sidx(k):
        return lambda i: (jnp.maximum(i - npt, 0), k)

    return pl.pallas_call(
        functools.partial(_norm_mod_kernel, n_prompt_tiles=npt),
        grid=(n // tm,),
        in_specs=[pl.BlockSpec((tm, d), lambda i: (i, 0)),
                  pl.BlockSpec((1, d), lambda i: (0, 0)),
                  pl.BlockSpec((1, 1, d), pidx(k_scale)),
                  pl.BlockSpec((1, 1, d), pidx(k_shift)),
                  pl.BlockSpec((tm, d), sidx(k_scale)),
                  pl.BlockSpec((tm, d), sidx(k_shift))],
        out_specs=pl.BlockSpec((tm, d), lambda i: (i, 0)),
        out_shape=jax.ShapeDtypeStruct((n, d), BF16),
        compiler_params=_params("arbitrary"),
        name="norm_mod",
    )(x, g.reshape(1, d), mod_p, mod_p, mod_s, mod_s)


def _mm_kernel(a_ref, w_ref, o_ref):
    o_ref[...] = jnp.dot(a_ref[...], w_ref[...], preferred_element_type=F32).astype(o_ref.dtype)


def _matmul(a, w, out_dtype, n_prompt, name):
    n, k = a.shape
    nc = w.shape[1]
    tm = _pow2_tile(512, n_prompt, n - n_prompt)
    tn = _pow2_tile(1024, nc)
    return pl.pallas_call(
        _mm_kernel,
        grid=(nc // tn, n // tm),
        in_specs=[pl.BlockSpec((tm, k), lambda j, i: (i, 0)),
                  pl.BlockSpec((k, tn), lambda j, i: (0, j))],
        out_specs=pl.BlockSpec((tm, tn), lambda j, i: (i, j)),
        out_shape=jax.ShapeDtypeStruct((n, nc), out_dtype),
        compiler_params=_params("arbitrary", "arbitrary"),
        name=name,
    )(a, w)


def _hgrn_tables(c):
    nl = int(math.log2(c))
    r = np.arange(c)[:, None]
    j = np.arange(c)[None, :]
    mats = []
    lvl = np.full((c, c), -1, np.int32)
    for l in range(nl):
        m = c >> (l + 1)
        mid = (r // (2 * m)) * 2 * m + m
        upper = r >= mid
        t = np.where(upper, (j >= mid) & (j <= r), (j > r) & (j <= mid - 1))
        mats.append(t)
        same = (r // (2 * m)) == (j // (2 * m))
        lvl = np.where(same & upper & (j < mid), l, lvl)
    lvl = np.where(r == j, nl, lvl)
    mats.append(j <= r)
    mats.append(j > r)
    t_all = np.concatenate(mats, axis=0).astype(np.float32)
    return jnp.asarray(t_all, BF16), jnp.asarray(lvl), nl


def _hgrn_prompt_kernel(q_ref, f_ref, v_ref, og_ref, lb_ref, g_ref, tall_ref, lvl_ref,
                        a_ref, st_ref, s_scr, *, chunk, n_levels, n_chunks, heads_per_step, dk):
    c = chunk
    s_scr[...] = jnp.zeros_like(s_scr)
    lb = lb_ref[...]
    lvl = lvl_ref[...]
    rows = lax.broadcasted_iota(jnp.int32, (c, q_ref.shape[1]), 0)
    nt = (((1,), (1,)), ((), ()))

    def body(ci, carry):
        sl = pl.ds(pl.multiple_of(ci * c, c), c)
        qs = _silu(q_ref[sl, :])
        f = lb + (1.0 - lb) * jax.nn.sigmoid(f_ref[sl, :])
        k = 1.0 - f
        lf = jnp.log(f)
        v = v_ref[sl, :].astype(BF16)
        lf_hi = lf.astype(BF16)
        lf_lo = (lf - lf_hi.astype(F32)).astype(BF16)
        ex = (jnp.dot(tall_ref[...], lf_hi, preferred_element_type=F32)
              + jnp.dot(tall_ref[...], lf_lo, preferred_element_type=F32))
        qs_b = qs.astype(BF16)
        k_b = k.astype(BF16)
        ys = []
        for l in range(n_levels):
            m = c >> (l + 1)
            e_l = jnp.exp(ex[l * c:(l + 1) * c, :])
            ys.append((e_l * jnp.where((rows & m) != 0, qs, k)).astype(BF16))
        bcum = ex[n_levels * c:(n_levels + 1) * c, :]
        brev = ex[(n_levels + 1) * c:(n_levels + 2) * c, :]
        qh = (qs * jnp.exp(bcum)).astype(BF16)
        kh = (k * jnp.exp(brev)).astype(BF16)
        decay = jnp.exp(bcum[c - 1:c, :])
        outs = []
        for hh in range(heads_per_step):
            hs = slice(hh * dk, (hh + 1) * dk)
            scores = jnp.where(lvl == n_levels,
                               lax.dot_general(qs_b[:, hs], k_b[:, hs], nt, preferred_element_type=F32), 0.0)
            for l in range(n_levels):
                y = ys[l][:, hs]
                p = lax.dot_general(y, y, nt, preferred_element_type=F32)
                scores = scores + jnp.where(lvl == l, p, 0.0)
            st = s_scr[hh]
            o = jnp.dot(scores.astype(BF16), v[:, hs], preferred_element_type=F32)
            o = o + lax.dot_general(qh[:, hs], st.astype(BF16), nt, preferred_element_type=F32)
            kv_t = lax.dot_general(v[:, hs], kh[:, hs], (((0,), (0,)), ((), ())), preferred_element_type=F32)
            s_scr[hh] = st * decay[:, hs] + kv_t
            outs.append(o * lax.rsqrt(jnp.mean(o * o, axis=-1, keepdims=True) + EPS))
        on = jnp.concatenate(outs, axis=1) if heads_per_step > 1 else outs[0]
        a_ref[sl, :] = (on * g_ref[...] * _silu(og_ref[sl, :])).astype(a_ref.dtype)
        return carry

    lax.fori_loop(0, n_chunks, body, 0)
    st_ref[0, 0] = s_scr[...]


def _hgrn_prompt(proj, lb, g_out, batch, seq, heads, dk, n_rows):
    c = _pow2_tile(128, seq)
    hp = _pow2_tile(4, heads)
    t_all, lvl, nl = _hgrn_tables(c)
    w = hp * dk
    nh = heads // hp
    col = lambda off: (lambda b, h: (b, off + h))
    kern = functools.partial(_hgrn_prompt_kernel, chunk=c, n_levels=nl, n_chunks=seq // c,
                             heads_per_step=hp, dk=dk)
    return pl.pallas_call(
        kern,
        grid=(batch, nh),
        in_specs=[pl.BlockSpec((seq, w), col(0)),
                  pl.BlockSpec((seq, w), col(nh)),
                  pl.BlockSpec((seq, w), col(2 * nh)),
                  pl.BlockSpec((seq, w), col(3 * nh)),
                  pl.BlockSpec((1, w), lambda b, h: (0, h)),
                  pl.BlockSpec((1, w), lambda b, h: (0, h)),
                  pl.BlockSpec(t_all.shape, lambda b, h: (0, 0)),
                  pl.BlockSpec(lvl.shape, lambda b, h: (0, 0))],
        out_specs=[pl.BlockSpec((seq, w), lambda b, h: (b, h)),
                   pl.BlockSpec((1, 1, hp, dk, dk), lambda b, h: (0, b, h, 0, 0))],
        out_shape=[jax.ShapeDtypeStruct((n_rows, heads * dk), BF16),
                   jax.ShapeDtypeStruct((1, batch, heads, dk, dk), F32)],
        scratch_shapes=[pltpu.VMEM((hp, dk, dk), F32)],
        compiler_params=_params("arbitrary", "arbitrary"),
        name="hgrn_prompt",
    )(proj, proj, proj, proj, lb, g_out, t_all, lvl)


def _hgrn_sample_kernel(q_ref, f_ref, v_ref, og_ref, lb_ref, g_ref, s0_ref, a_ref, s_ref, *, bt, steps):
    lb = lb_ref[...]
    g = g_ref[...]
    rows = lax.broadcasted_iota(jnp.int32, (steps, q_ref.shape[2]), 0)
    nt = (((1,), (1,)), ((), ()))

    def body(b, carry):
        qs = _silu(q_ref[b])
        f = lb + (1.0 - lb) * jax.nn.sigmoid(f_ref[b])
        k = 1.0 - f
        lf = jnp.log(f)
        v = v_ref[b]
        cum = []
        run = None
        for t in range(steps):
            run = lf[t:t + 1, :] if run is None else run + lf[t:t + 1, :]
            cum.append(run)
        bmat = jnp.broadcast_to(cum[0], lf.shape)
        for t in range(1, steps):
            bmat = jnp.where(rows == t, cum[t], bmat)
        o = jnp.zeros_like(qs)
        for s in range(steps):
            x = jnp.where(rows >= s, qs * k[s:s + 1, :] * jnp.exp(jnp.minimum(bmat - cum[s], 0.0)), 0.0)
            o = o + jnp.sum(x, axis=-1, keepdims=True) * v[s:s + 1, :]
        st = s0_ref[0, b, 0].T
        o = o + lax.dot_general((qs * jnp.exp(bmat)).astype(BF16), st.astype(BF16), nt,
                                preferred_element_type=F32)
        kh = (k * jnp.exp(cum[steps - 1] - bmat)).astype(BF16)
        kv_t = lax.dot_general(v.astype(BF16), kh, (((0,), (0,)), ((), ())), preferred_element_type=F32)
        s_ref[0, b, 0] = (st * jnp.exp(cum[steps - 1]) + kv_t).T
        on = o * lax.rsqrt(jnp.mean(o * o, axis=-1, keepdims=True) + EPS)
        a_ref[b] = on * g * _silu(og_ref[b])
        return carry

    lax.fori_loop(0, bt, body, 0, unroll=8)


def _hgrn_sample(proj_s, lb, g_out, s0, heads, dk):
    db, steps, _ = proj_s.shape
    bt = _pow2_tile(32, db)
    col = lambda off: (lambda j, h: (j, 0, off + h))
    kern = functools.partial(_hgrn_sample_kernel, bt=bt, steps=steps)
    return pl.pallas_call(
        kern,
        grid=(db // bt, heads),
        in_specs=[pl.BlockSpec((bt, steps, dk), col(0)),
                  pl.BlockSpec((bt, steps, dk), col(heads)),
                  pl.BlockSpec((bt, steps, dk), col(2 * heads)),
                  pl.BlockSpec((bt, steps, dk), col(3 * heads)),
                  pl.BlockSpec((1, dk), lambda j, h: (0, h)),
                  pl.BlockSpec((1, dk), lambda j, h: (0, h)),
                  pl.BlockSpec((1, bt, 1, dk, dk), lambda j, h: (0, j, h, 0, 0))],
        out_specs=[pl.BlockSpec((bt, steps, dk), lambda j, h: (j, 0, h)),
                   pl.BlockSpec((1, bt, 1, dk, dk), lambda j, h: (0, j, h, 0, 0))],
        out_shape=[jax.ShapeDtypeStruct((db, steps, heads * dk), F32),
                   jax.ShapeDtypeStruct(s0.shape, F32)],
        compiler_params=_params("arbitrary", "arbitrary"),
        name="hgrn_sample",
    )(proj_s, proj_s, proj_s, proj_s, lb, g_out, s0)


def _gmlp_kernel(gu_ref, gv_ref, w_ref, bias_ref, g_ref, o_ref, v_ref, *, groups, gd):
    u = jax.nn.gelu(gu_ref[...])
    vv = jax.nn.gelu(gv_ref[...])
    r = lax.rsqrt(jnp.mean(vv * vv, axis=-1, keepdims=True) + EPS)
    v = vv * r * g_ref[...]
    v_ref[...] = v
    for gi in range(groups):
        sl = slice(gi * gd, (gi + 1) * gd)
        z = jnp.dot(w_ref[0, gi], v[:, sl].astype(BF16), preferred_element_type=F32) + bias_ref[0, :, sl]
        o_ref[:, sl] = (u[:, sl] * z).astype(o_ref.dtype)


def _gmlp(proj, w_st, bias_st, g_v, n_prompt, gu_blk, gv_blk):
    n = proj.shape[0]
    _, groups, c, _ = w_st.shape
    gw = bias_st.shape[2]
    npc = n_prompt // c
    sel = lambda i: jnp.where(i >= npc, 1, 0)
    kern = functools.partial(_gmlp_kernel, groups=groups, gd=gw // groups)
    return pl.pallas_call(
        kern,
        grid=(n // c,),
        in_specs=[pl.BlockSpec((c, gw), lambda i: (i, gu_blk)),
                  pl.BlockSpec((c, gw), lambda i: (i, gv_blk)),
                  pl.BlockSpec((1, groups, c, c), lambda i: (sel(i), 0, 0, 0)),
                  pl.BlockSpec((1, c, gw), lambda i: (sel(i), 0, 0)),
                  pl.BlockSpec((1, gw), lambda i: (0, 0))],
        out_specs=[pl.BlockSpec((c, gw), lambda i: (i, 0)),
                   pl.BlockSpec((c, gw), lambda i: (jnp.maximum(i - npc, 0), 0))],
        out_shape=[jax.ShapeDtypeStruct((n, gw), BF16),
                   jax.ShapeDtypeStruct((n - n_prompt, gw), F32)],
        compiler_params=_params("arbitrary"),
        name="chunk_mlp",
    )(proj, proj, w_st, bias_st, g_v.reshape(1, gw))


def _branch_kernel(a_ref, b_ref, wa_ref, wb_ref, ga_ref, gb_ref, o_ref):
    ya = jnp.dot(a_ref[...], wa_ref[...], preferred_element_type=F32)
    yb = jnp.dot(b_ref[...], wb_ref[...], preferred_element_type=F32)
    o_ref[...] = (jax.nn.sigmoid(ga_ref[...]) * ya + jax.nn.sigmoid(gb_ref[...]) * yb).astype(o_ref.dtype)


def _branches(a, bm, wa, wb, proj, ga_off, gb_off, n_prompt):
    n, ka = a.shape
    kb = bm.shape[1]
    d = wa.shape[1]
    tm = _pow2_tile(512, n_prompt, n - n_prompt)
    tn = _pow2_tile(1024, d, ga_off, gb_off)
    return pl.pallas_call(
        _branch_kernel,
        grid=(d // tn, n // tm),
        in_specs=[pl.BlockSpec((tm, ka), lambda j, i: (i, 0)),
                  pl.BlockSpec((tm, kb), lambda j, i: (i, 0)),
                  pl.BlockSpec((ka, tn), lambda j, i: (0, j)),
                  pl.BlockSpec((kb, tn), lambda j, i: (0, j)),
                  pl.BlockSpec((tm, tn), lambda j, i: (i, ga_off // tn + j)),
                  pl.BlockSpec((tm, tn), lambda j, i: (i, gb_off // tn + j))],
        out_specs=pl.BlockSpec((tm, tn), lambda j, i: (i, j)),
        out_shape=jax.ShapeDtypeStruct((n, d), BF16),
        compiler_params=_params("arbitrary", "arbitrary"),
        name="branches",
    )(a, bm, wa, wb, proj, proj)


def _out_kernel(m_ref, w_ref, x_ref, gp_ref, gs_ref, o_ref, *, n_prompt_tiles):
    i = pl.program_id(1)
    y = jnp.dot(m_ref[...], w_ref[...], preferred_element_type=F32)

    @pl.when(i < n_prompt_tiles)
    def _():
        o_ref[...] = x_ref[...] + gp_ref[0] * y

    @pl.when(i >= n_prompt_tiles)
    def _():
        o_ref[...] = x_ref[...] + gs_ref[...] * y


def _out_proj(mix, w, x, mod_p, mod_s, k_gate, n_prompt, seq):
    n, d = x.shape
    tm = _pow2_tile(512, seq, n - n_prompt)
    tn = _pow2_tile(1024, d)
    npt = n_prompt // tm
    per_b = seq // tm
    nj = d // tn
    return pl.pallas_call(
        functools.partial(_out_kernel, n_prompt_tiles=npt),
        grid=(nj, n // tm),
        in_specs=[pl.BlockSpec((tm, d), lambda j, i: (i, 0)),
                  pl.BlockSpec((d, tn), lambda j, i: (0, j)),
                  pl.BlockSpec((tm, tn), lambda j, i: (i, j)),
                  pl.BlockSpec((1, 1, tn), lambda j, i: (jnp.minimum(i, npt - 1) // per_b, 0, k_gate * nj + j)),
                  pl.BlockSpec((tm, tn), lambda j, i: (jnp.maximum(i - npt, 0), k_gate * nj + j))],
        out_specs=pl.BlockSpec((tm, tn), lambda j, i: (i, j)),
        out_shape=jax.ShapeDtypeStruct((n, d), F32),
        compiler_params=_params("arbitrary", "arbitrary"),
        name="out_proj",
    )(mix, w, x, mod_p, mod_s)


def _cand_pairs(k):
    return [(i, j) for i in range(k) for j in range(k) if (i + 1) * (j + 1) <= k]


def _top_rows(s, n_iota, k):
    rank = jnp.full(s.shape, float(k), F32)
    vals = []
    big = float(s.shape[0])
    for i in range(k):
        m = jnp.max(s, axis=0, keepdims=True)
        idx = jnp.min(jnp.where(s == m, n_iota, big), axis=0, keepdims=True)
        sel = n_iota == idx
        rank = jnp.where(sel, float(i), rank)
        s = jnp.where(sel, NEG_INF, s)
        vals.append(m)
    return vals, rank


def _peer_topk_kernel(q_ref, sk_ref, jsum_ref, eaz_ref, jd_ref, eb_ref, r2_ref, cand_scr, *, half, pairs, k):
    nt = (((1,), (1,)), ((), ()))
    q = q_ref[...].astype(BF16)
    s1 = lax.dot_general(sk_ref[0, 0], q[:, :half], nt, preferred_element_type=F32)
    s2 = lax.dot_general(sk_ref[0, 1], q[:, half:], nt, preferred_element_type=F32)
    n_iota = lax.broadcasted_iota(jnp.int32, s1.shape, 0).astype(F32)
    a, r1 = _top_rows(s1, n_iota, k)
    b, r2 = _top_rows(s2, n_iota, k)

    cand_scr[...] = jnp.full(cand_scr.shape, NEG_INF, F32)
    for p, (i, j) in enumerate(pairs):
        cand_scr[p:p + 1, :] = a[i] + b[j]
    c0 = cand_scr[...]
    p_iota = lax.broadcasted_iota(jnp.int32, c0.shape, 0).astype(F32)
    c = c0
    taken = jnp.zeros(c0.shape, F32)
    big = float(c0.shape[0])
    for _ in range(k):
        m = jnp.max(c, axis=0, keepdims=True)
        idx = jnp.min(jnp.where(c == m, p_iota, big), axis=0, keepdims=True)
        sel = p_iota == idx
        taken = jnp.where(sel, 1.0, taken)
        c = jnp.where(sel, NEG_INF, c)
    w = jnp.where(taken > 0.0, jnp.exp(c0 - (a[0] + b[0])), 0.0)
    z = jnp.sum(w, axis=0, keepdims=True)
    jcnt = jnp.dot(jsum_ref[...], taken.astype(BF16), preferred_element_type=F32)

    inv_z = 1.0 / z
    eaz = jnp.zeros(s1.shape, F32)
    jd = jnp.zeros(s1.shape, F32)
    eb = jnp.zeros(s1.shape, F32)
    for i in range(k):
        hit1 = r1 == float(i)
        eaz = jnp.where(hit1, jnp.exp(a[i] - a[0]) * inv_z, eaz)
        jd = jnp.where(hit1, jcnt[i:i + 1, :], jd)
        eb = jnp.where(r2 == float(i), jnp.exp(b[i] - b[0]), eb)
    eaz_ref[0] = eaz
    jd_ref[0] = jd
    eb_ref[0] = eb
    r2_ref[0] = r2


def _peer_topk(qp, sub_keys_b, n_prompt):
    n = qp.shape[0]
    heads, _, n_keys, half = sub_keys_b.shape
    tt = _pow2_tile(256, n_prompt, n - n_prompt)
    pairs = _cand_pairs(PEER_TOPK)
    n_cand = -(-len(pairs) // 64) * 64
    jsum = np.zeros((PEER_TOPK, n_cand), np.float32)
    for p, (i, _) in enumerate(pairs):
        jsum[i, p] = 1.0
    kern = functools.partial(_peer_topk_kernel, half=half, pairs=pairs, k=PEER_TOPK)
    maps = jax.ShapeDtypeStruct((heads, n_keys, n), F32)
    mspec = pl.BlockSpec((1, n_keys, tt), lambda i, h: (h, 0, i))
    return pl.pallas_call(
        kern,
        grid=(n // tt, heads),
        in_specs=[pl.BlockSpec((tt, 2 * half), lambda i, h: (i, h)),
                  pl.BlockSpec((1, 2, n_keys, half), lambda i, h: (h, 0, 0, 0)),
                  pl.BlockSpec(jsum.shape, lambda i, h: (0, 0))],
        out_specs=[mspec, mspec, mspec, mspec],
        out_shape=[maps, maps, maps, maps],
        scratch_shapes=[pltpu.VMEM((n_cand, tt), F32)],
        compiler_params=_params("arbitrary", "arbitrary"),
        name="peer_topk",
    )(qp, sub_keys_b, jnp.asarray(jsum, BF16))


def _peer_main_kernel(h_ref, u_ref, v_ref, eaz_ref, jd_ref, eb_ref, r2_ref, o_ref, ga0_scr, ga1_scr, *,
                      heads, n_keys, rows_per_step, n_steps):
    e = pl.program_id(1)

    @pl.when(e == 0)
    def _():
        o_ref[...] = jnp.zeros_like(o_ref)
        ga1_scr[...] = jnp.zeros_like(ga1_scr)

    def stage(cur_scr, prev_scr):
        o_ref[...] += lax.dot_general(prev_scr[...], v_ref[...], (((0,), (0,)), ((), ())),
                                      preferred_element_type=F32)
        act_t = lax.dot_general(u_ref[...], h_ref[...], (((1,), (1,)), ((), ())),
                                preferred_element_type=F32)
        chunk = jnp.minimum(e, n_steps - 1)
        for r in range(rows_per_step):
            n1 = chunk * rows_per_step + r
            g = None
            for h in range(heads):
                ea = eaz_ref[h, pl.ds(n1, 1), :]
                jd = jd_ref[h, pl.ds(n1, 1), :]
                t = ea * jnp.where(r2_ref[h] < jd, eb_ref[h], 0.0)
                g = t if g is None else g + t
            sl = slice(r * n_keys, (r + 1) * n_keys)
            cur_scr[sl, :] = (g * jax.nn.gelu(act_t[sl, :])).astype(BF16)

    @pl.when(e % 2 == 0)
    def _():
        stage(ga0_scr, ga1_scr)

    @pl.when(e % 2 == 1)
    def _():
        stage(ga1_scr, ga0_scr)


def _peer_main(h2, u_b, v_b, maps, n_prompt):
    n, d = h2.shape
    n_exp = u_b.shape[0]
    heads, n_keys, _ = maps[0].shape
    tt = _pow2_tile(512, n_prompt, n - n_prompt)
    ec = _pow2_tile(512, n_exp)
    rps = ec // n_keys
    n_steps = n_exp // ec
    kern = functools.partial(_peer_main_kernel, heads=heads, n_keys=n_keys, rows_per_step=rps, n_steps=n_steps)
    once = pl.Buffered(1)
    mspec = pl.BlockSpec((heads, n_keys, tt), lambda i, e: (0, 0, i), pipeline_mode=once)
    return pl.pallas_call(
        kern,
        grid=(n // tt, n_steps + 1),
        in_specs=[pl.BlockSpec((tt, d), lambda i, e: (i, 0), pipeline_mode=once),
                  pl.BlockSpec((ec, d), lambda i, e: (jnp.minimum(e, n_steps - 1), 0)),
                  pl.BlockSpec((ec, d), lambda i, e: (jnp.maximum(e - 1, 0), 0)),
                  mspec, mspec, mspec, mspec],
        out_specs=pl.BlockSpec((tt, d), lambda i, e: (i, 0), pipeline_mode=once),
        out_shape=jax.ShapeDtypeStruct((n, d), F32),
        scratch_shapes=[pltpu.VMEM((ec, tt), BF16), pltpu.VMEM((ec, tt), BF16)],
        compiler_params=_params("arbitrary", "arbitrary"),
        name="peer_main",
    )(h2, u_b, v_b, *maps)


def _final_kernel(x_ref, p_ref, gp_ref, gs_ref, g_ref, yp_ref, ys_ref, *, n_prompt_tiles):
    i = pl.program_id(0)

    def norm(x):
        return x * lax.rsqrt(jnp.mean(x * x, axis=-1, keepdims=True) + EPS) * g_ref[...]

    @pl.when(i < n_prompt_tiles)
    def _():
        yp_ref[...] = norm(x_ref[...] + gp_ref[0] * p_ref[...])

    @pl.when(i >= n_prompt_tiles)
    def _():
        ys_ref[...] = norm(x_ref[...] + gs_ref[...] * p_ref[...])


def _final(x1, peer_out, mod_p, mod_s, k_gate, g_final, n_prompt, seq):
    n, d = x1.shape
    tm = _pow2_tile(256, seq, n - n_prompt)
    npt = n_prompt // tm
    per_b = seq // tm
    return pl.pallas_call(
        functools.partial(_final_kernel, n_prompt_tiles=npt),
        grid=(n // tm,),
        in_specs=[pl.BlockSpec((tm, d), lambda i: (i, 0)),
                  pl.BlockSpec((tm, d), lambda i: (i, 0)),
                  pl.BlockSpec((1, 1, d), lambda i: (jnp.minimum(i, npt - 1) // per_b, 0, k_gate)),
                  pl.BlockSpec((tm, d), lambda i: (jnp.maximum(i - npt, 0), k_gate)),
                  pl.BlockSpec((1, d), lambda i: (0, 0))],
        out_specs=[pl.BlockSpec((tm, d), lambda i: (jnp.minimum(i, npt - 1), 0)),
                   pl.BlockSpec((tm, d), lambda i: (jnp.maximum(i - npt, 0), 0))],
        out_shape=[jax.ShapeDtypeStruct((n_prompt, d), F32),
                   jax.ShapeDtypeStruct((n - n_prompt, d), F32)],
        compiler_params=_params("arbitrary"),
        name="final_norm",
    )(x1, peer_out, mod_p, mod_s, g_final.reshape(1, d))


def kernel(x_prompt, x_sample, c_prompt, c_sample, state_hgrn, w_ada, b_ada, g_norm1, w_in, hgrn_lb_logits,
           g_hgrn_out, g_gmlp_v, w_spatial, b_spatial, w_branch_a, w_branch_b, w_out, g_norm2, w_peer_q,
           peer_sub_keys, peer_u, peer_v, g_final):
    batch, seq, d = x_prompt.shape
    db, dt, _ = x_sample.shape
    depth, _, heads, dk, dv = state_hgrn.shape
    assert depth == 1 and dk == dv == LANE_V7X
    hw = heads * dk
    groups, gc = w_spatial.shape[1], w_spatial.shape[2]
    gw = w_branch_b.shape[1]
    assert gc == LANE_V7X and gw // groups == LANE_V7X and dt <= gc and gc % dt == 0
    n_prompt, n_sample = batch * seq, db * dt
    n = n_prompt + n_sample

    c_all = jnp.concatenate([c_prompt, c_sample], axis=0)
    pad = (-c_all.shape[0]) % 8
    c_all = jnp.pad(c_all, ((0, pad), (0, 0)))
    mod = _ada(c_all, w_ada[0], b_ada[0])
    mod_p = mod[:batch].reshape(batch, 1, N_MOD * d)
    mod_s = jnp.repeat(mod[batch:batch + db], dt, axis=0)

    x = jnp.concatenate([x_prompt.reshape(n_prompt, d), x_sample.reshape(n_sample, d)], axis=0)

    h1 = _norm_mod(x, g_norm1[0], mod_p, mod_s, 1, 0, n_prompt, seq)
    proj = _matmul(h1, w_in[0].astype(BF16), F32, n_prompt, "in_proj")

    lb = jnp.cumsum(jax.nn.softmax(hgrn_lb_logits.astype(F32), axis=0), axis=0)[0].reshape(1, hw)
    g_ho = g_hgrn_out[0].reshape(1, hw)
    a_p, st_p = _hgrn_prompt(proj, lb, g_ho, batch, seq, heads, dk, n_prompt)
    proj_s = proj[n_prompt:].reshape(db, dt, proj.shape[1])
    a_s, st_s = _hgrn_sample(proj_s, lb, g_ho, state_hgrn, heads, dk)
    a_all = jnp.concatenate([a_p, a_s.reshape(n_sample, hw).astype(BF16)], axis=0)

    tril = jnp.tril(jnp.ones((gc, gc), F32))
    w_sp = w_spatial[0]
    blk = jnp.arange(gc) // dt
    w_samp = jnp.tile(w_sp[:, :dt, :dt], (1, gc // dt, gc // dt)) * (blk[:, None] == blk[None, :])
    w_st = jnp.stack([w_sp * tril, w_samp * tril]).astype(BF16)
    bias_full = jnp.repeat(b_spatial[0].T, gw // groups, axis=1)
    bias_st = jnp.stack([bias_full, jnp.tile(bias_full[:dt], (gc // dt, 1))])
    gu_blk = 4 * hw // gw
    assert gu_blk * gw == 4 * hw
    bm, v_s = _gmlp(proj, w_st, bias_st, g_gmlp_v[0], n_prompt, gu_blk, gu_blk + 1)

    ga_off = 4 * hw + 2 * gw
    mix = _branches(a_all, bm, w_branch_a[0].astype(BF16), w_branch_b[0].astype(BF16), proj,
                    ga_off, ga_off + d, n_prompt)
    x1 = _out_proj(mix, w_out[0].astype(BF16), x, mod_p, mod_s, 2, n_prompt, seq)

    h2 = _norm_mod(x1, g_norm2[0], mod_p, mod_s, 4, 3, n_prompt, seq)
    qp = _matmul(h2, w_peer_q[0].astype(BF16), F32, n_prompt, "peer_query")
    maps = _peer_topk(qp, peer_sub_keys[0].astype(BF16), n_prompt)
    peer_out = _peer_main(h2, peer_u[0].astype(BF16), peer_v[0].astype(BF16), maps, n_prompt)
    y_p, y_s = _final(x1, peer_out, mod_p, mod_s, 5, g_final, n_prompt, seq)

    state_p = jnp.swapaxes(st_p, -1, -2)
    return (y_p.reshape(batch, seq, d), y_s.reshape(db, dt, d), state_p, st_s,
            v_s.reshape(1, db, dt, gw))
```

```python
import functools
import math

import numpy as np
import jax
import jax.numpy as jnp
from jax import lax
from jax.experimental import pallas as pl
from jax.experimental.pallas import tpu as pltpu

EPS = 1e-6
N_MOD = 6
PEER_TOPK = 16
LANE_V7X = 128
VMEM_LIMIT_V7X = 56 * 1024 * 1024
F32 = jnp.float32
BF16 = jnp.bfloat16
NEG_INF = float("-inf")


def _params(*sem):
    return pltpu.CompilerParams(dimension_semantics=sem, vmem_limit_bytes=VMEM_LIMIT_V7X)


def _pow2_tile(target, *sizes):
    g = 0
    for s in sizes:
        g = math.gcd(g, s)
    t = 1
    while t * 2 <= target and g % (t * 2) == 0:
        t *= 2
    return t


def _silu(x):
    return x * jax.nn.sigmoid(x)


def _ada_kernel(c_ref, w_ref, b_ref, o_ref):
    s = _silu(c_ref[...]).astype(BF16)
    o_ref[...] = jnp.dot(s, w_ref[...].astype(BF16), preferred_element_type=F32) + b_ref[...]


def _ada(c_all, w_ada, b_ada):
    m, d = c_all.shape
    n = w_ada.shape[1]
    tn = _pow2_tile(512, n)
    return pl.pallas_call(
        _ada_kernel,
        grid=(n // tn,),
        in_specs=[pl.BlockSpec((m, d), lambda j: (0, 0)),
                  pl.BlockSpec((d, tn), lambda j: (0, j)),
                  pl.BlockSpec((1, tn), lambda j: (0, j))],
        out_specs=pl.BlockSpec((m, tn), lambda j: (0, j)),
        out_shape=jax.ShapeDtypeStruct((m, n), F32),
        compiler_params=_params("arbitrary"),
        name="ada_mod",
    )(c_all, w_ada, b_ada.reshape(1, n))


def _norm_mod_kernel(x_ref, g_ref, scp_ref, shp_ref, scs_ref, shs_ref, o_ref, *, n_prompt_tiles):
    i = pl.program_id(0)
    x = x_ref[...]
    r = lax.rsqrt(jnp.mean(x * x, axis=-1, keepdims=True) + EPS)
    xn = x * r * g_ref[...]

    @pl.when(i < n_prompt_tiles)
    def _():
        o_ref[...] = (xn * (1.0 + scp_ref[0]) + shp_ref[0]).astype(o_ref.dtype)

    @pl.when(i >= n_prompt_tiles)
    def _():
        o_ref[...] = (xn * (1.0 + scs_ref[...]) + shs_ref[...]).astype(o_ref.dtype)


def _norm_mod(x, g, mod_p, mod_s, k_scale, k_shift, n_prompt, seq):
    n, d = x.shape
    tm = _pow2_tile(256, seq, n - n_prompt)
    npt = n_prompt // tm
    per_b = seq // tm

    def pidx(k):
        return lambda i: (jnp.minimum(i, npt - 1) // per_b, 0, k)

    def sidx(k):
        return lambda i: (jnp.maximum(i - npt, 0), k)

    return pl.pallas_call(
        functools.partial(_norm_mod_kernel, n_prompt_tiles=npt),
        grid=(n // tm,),
        in_specs=[pl.BlockSpec((tm, d), lambda i: (i, 0)),
                  pl.BlockSpec((1, d), lambda i: (0, 0)),
                  pl.BlockSpec((1, 1, d), pidx(k_scale)),
                  pl.BlockSpec((1, 1, d), pidx(k_shift)),
                  pl.BlockSpec((tm, d), sidx(k_scale)),
                  pl.BlockSpec((tm, d), sidx(k_shift))],
        out_specs=pl.BlockSpec((tm, d), lambda i: (i, 0)),
        out_shape=jax.ShapeDtypeStruct((n, d), BF16),
        compiler_params=_params("arbitrary"),
        name="norm_mod",
    )(x, g.reshape(1, d), mod_p, mod_p, mod_s, mod_s)


def _mm_kernel(a_ref, w_ref, o_ref, wb_scr):
    @pl.when(pl.program_id(1) == 0)
    def _():
        wb_scr[...] = w_ref[...].astype(BF16)

    o_ref[...] = jnp.dot(a_ref[...], wb_scr[...], preferred_element_type=F32).astype(o_ref.dtype)


def _matmul(a, w, out_dtype, n_prompt, name):
    n, k = a.shape
    nc = w.shape[1]
    tm = _pow2_tile(512, n_prompt, n - n_prompt)
    tn = _pow2_tile(1024, nc)
    return pl.pallas_call(
        _mm_kernel,
        grid=(nc // tn, n // tm),
        in_specs=[pl.BlockSpec((tm, k), lambda j, i: (i, 0)),
                  pl.BlockSpec((k, tn), lambda j, i: (0, j))],
        out_specs=pl.BlockSpec((tm, tn), lambda j, i: (i, j)),
        out_shape=jax.ShapeDtypeStruct((n, nc), out_dtype),
        scratch_shapes=[pltpu.VMEM((k, tn), BF16)],
        compiler_params=_params("arbitrary", "arbitrary"),
        name=name,
    )(a, w)


def _hgrn_tables(c):
    nl = int(math.log2(c))
    r = np.arange(c)[:, None]
    j = np.arange(c)[None, :]
    mats = []
    lvl = np.full((c, c), -1, np.int32)
    for l in range(nl):
        m = c >> (l + 1)
        mid = (r // (2 * m)) * 2 * m + m
        upper = r >= mid
        t = np.where(upper, (j >= mid) & (j <= r), (j > r) & (j <= mid - 1))
        mats.append(t)
        same = (r // (2 * m)) == (j // (2 * m))
        lvl = np.where(same & upper & (j < mid), l, lvl)
    lvl = np.where(r == j, nl, lvl)
    mats.append(j <= r)
    mats.append(j > r)
    t_all = np.concatenate(mats, axis=0).astype(np.float32)
    return jnp.asarray(t_all, BF16), jnp.asarray(lvl), nl


def _hgrn_prompt_kernel(q_ref, f_ref, v_ref, og_ref, lb_ref, g_ref, tall_ref, lvl_ref,
                        a_ref, st_ref, s_scr, *, chunk, n_levels, n_chunks, heads_per_step, dk):
    c = chunk
    s_scr[...] = jnp.zeros_like(s_scr)
    lb = lb_ref[...]
    lvl = lvl_ref[...]
    rows = lax.broadcasted_iota(jnp.int32, (c, q_ref.shape[1]), 0)
    nt = (((1,), (1,)), ((), ()))

    def body(ci, carry):
        sl = pl.ds(pl.multiple_of(ci * c, c), c)
        qs = _silu(q_ref[sl, :])
        f = lb + (1.0 - lb) * jax.nn.sigmoid(f_ref[sl, :])
        k = 1.0 - f
        lf = jnp.log(f)
        v = v_ref[sl, :].astype(BF16)
        lf_hi = lf.astype(BF16)
        lf_lo = (lf - lf_hi.astype(F32)).astype(BF16)
        ex = (jnp.dot(tall_ref[...], lf_hi, preferred_element_type=F32)
              + jnp.dot(tall_ref[...], lf_lo, preferred_element_type=F32))
        qs_b = qs.astype(BF16)
        k_b = k.astype(BF16)
        ys = []
        for l in range(n_levels):
            m = c >> (l + 1)
            e_l = jnp.exp(ex[l * c:(l + 1) * c, :])
            ys.append((e_l * jnp.where((rows & m) != 0, qs, k)).astype(BF16))
        bcum = ex[n_levels * c:(n_levels + 1) * c, :]
        brev = ex[(n_levels + 1) * c:(n_levels + 2) * c, :]
        qh = (qs * jnp.exp(bcum)).astype(BF16)
        kh = (k * jnp.exp(brev)).astype(BF16)
        decay = jnp.exp(bcum[c - 1:c, :])
        outs = []
        for hh in range(heads_per_step):
            hs = slice(hh * dk, (hh + 1) * dk)
            scores = jnp.where(lvl == n_levels,
                               lax.dot_general(qs_b[:, hs], k_b[:, hs], nt, preferred_element_type=F32), 0.0)
            for l in range(n_levels):
                y = ys[l][:, hs]
                p = lax.dot_general(y, y, nt, preferred_element_type=F32)
                scores = scores + jnp.where(lvl == l, p, 0.0)
            st = s_scr[hh]
            o = jnp.dot(scores.astype(BF16), v[:, hs], preferred_element_type=F32)
            o = o + lax.dot_general(qh[:, hs], st.astype(BF16), nt, preferred_element_type=F32)
            kv_t = lax.dot_general(v[:, hs], kh[:, hs], (((0,), (0,)), ((), ())), preferred_element_type=F32)
            s_scr[hh] = st * decay[:, hs] + kv_t
            outs.append(o * lax.rsqrt(jnp.mean(o * o, axis=-1, keepdims=True) + EPS))
        on = jnp.concatenate(outs, axis=1) if heads_per_step > 1 else outs[0]
        a_ref[sl, :] = (on * g_ref[...] * _silu(og_ref[sl, :])).astype(a_ref.dtype)
        return carry

    lax.fori_loop(0, n_chunks, body, 0)
    st_ref[0, 0] = s_scr[...]


def _hgrn_prompt(proj, lb, g_out, batch, seq, heads, dk, n_rows):
    c = _pow2_tile(128, seq)
    hp = _pow2_tile(4, heads)
    t_all, lvl, nl = _hgrn_tables(c)
    w = hp * dk
    nh = heads // hp
    col = lambda off: (lambda b, h: (b, off + h))
    kern = functools.partial(_hgrn_prompt_kernel, chunk=c, n_levels=nl, n_chunks=seq // c,
                             heads_per_step=hp, dk=dk)
    return pl.pallas_call(
        kern,
        grid=(batch, nh),
        in_specs=[pl.BlockSpec((seq, w), col(0)),
                  pl.BlockSpec((seq, w), col(nh)),
                  pl.BlockSpec((seq, w), col(2 * nh)),
                  pl.BlockSpec((seq, w), col(3 * nh)),
                  pl.BlockSpec((1, w), lambda b, h: (0, h)),
                  pl.BlockSpec((1, w), lambda b, h: (0, h)),
                  pl.BlockSpec(t_all.shape, lambda b, h: (0, 0)),
                  pl.BlockSpec(lvl.shape, lambda b, h: (0, 0))],
        out_specs=[pl.BlockSpec((seq, w), lambda b, h: (b, h)),
                   pl.BlockSpec((1, 1, hp, dk, dk), lambda b, h: (0, b, h, 0, 0))],
        out_shape=[jax.ShapeDtypeStruct((n_rows, heads * dk), BF16),
                   jax.ShapeDtypeStruct((1, batch, heads, dk, dk), F32)],
        scratch_shapes=[pltpu.VMEM((hp, dk, dk), F32)],
        compiler_params=_params("arbitrary", "arbitrary"),
        name="hgrn_prompt",
    )(proj, proj, proj, proj, lb, g_out, t_all, lvl)


def _hgrn_sample_kernel(q_ref, f_ref, v_ref, og_ref, lb_ref, g_ref, s0_ref, a_ref, s_ref, *, bt, steps):
    lb = lb_ref[...]
    g = g_ref[...]
    rows = lax.broadcasted_iota(jnp.int32, (steps, q_ref.shape[2]), 0)

    def body(b, carry):
        qs = _silu(q_ref[b])
        f = lb + (1.0 - lb) * jax.nn.sigmoid(f_ref[b])
        k = 1.0 - f
        lf = jnp.log(f)
        v = v_ref[b]
        cum = []
        run = None
        for t in range(steps):
            run = lf[t:t + 1, :] if run is None else run + lf[t:t + 1, :]
            cum.append(run)
        bmat = jnp.broadcast_to(cum[0], lf.shape)
        for t in range(1, steps):
            bmat = jnp.where(rows == t, cum[t], bmat)
        o = jnp.zeros_like(qs)
        for s in range(steps):
            x = jnp.where(rows >= s, qs * k[s:s + 1, :] * jnp.exp(jnp.minimum(bmat - cum[s], 0.0)), 0.0)
            o = o + jnp.sum(x, axis=-1, keepdims=True) * v[s:s + 1, :]
        s0 = s0_ref[0, b, 0]
        o = o + jnp.dot((qs * jnp.exp(bmat)).astype(BF16), s0.astype(BF16), preferred_element_type=F32)
        kh = (k * jnp.exp(cum[steps - 1] - bmat)).astype(BF16)
        kv = lax.dot_general(kh, v.astype(BF16), (((0,), (0,)), ((), ())), preferred_element_type=F32)
        decay_col = jnp.broadcast_to(jnp.exp(cum[steps - 1]), s0.shape).T
        s_ref[0, b, 0] = s0 * decay_col + kv
        on = o * lax.rsqrt(jnp.mean(o * o, axis=-1, keepdims=True) + EPS)
        a_ref[b] = on * g * _silu(og_ref[b])
        return carry

    lax.fori_loop(0, bt, body, 0, unroll=8)


def _hgrn_sample(proj_s, lb, g_out, s0, heads, dk):
    db, steps, _ = proj_s.shape
    bt = _pow2_tile(32, db)
    col = lambda off: (lambda j, h: (j, 0, off + h))
    kern = functools.partial(_hgrn_sample_kernel, bt=bt, steps=steps)
    return pl.pallas_call(
        kern,
        grid=(db // bt, heads),
        in_specs=[pl.BlockSpec((bt, steps, dk), col(0)),
                  pl.BlockSpec((bt, steps, dk), col(heads)),
                  pl.BlockSpec((bt, steps, dk), col(2 * heads)),
                  pl.BlockSpec((bt, steps, dk), col(3 * heads)),
                  pl.BlockSpec((1, dk), lambda j, h: (0, h)),
                  pl.BlockSpec((1, dk), lambda j, h: (0, h)),
                  pl.BlockSpec((1, bt, 1, dk, dk), lambda j, h: (0, j, h, 0, 0))],
        out_specs=[pl.BlockSpec((bt, steps, dk), lambda j, h: (j, 0, h)),
                   pl.BlockSpec((1, bt, 1, dk, dk), lambda j, h: (0, j, h, 0, 0))],
        out_shape=[jax.ShapeDtypeStruct((db, steps, heads * dk), F32),
                   jax.ShapeDtypeStruct(s0.shape, F32)],
        compiler_params=_params("arbitrary", "arbitrary"),
        name="hgrn_sample",
    )(proj_s, proj_s, proj_s, proj_s, lb, g_out, s0)


def _gmlp_kernel(gu_ref, gv_ref, w_ref, bias_ref, g_ref, o_ref, v_ref, *, groups, gd):
    u = jax.nn.gelu(gu_ref[...])
    vv = jax.nn.gelu(gv_ref[...])
    r = lax.rsqrt(jnp.mean(vv * vv, axis=-1, keepdims=True) + EPS)
    v = vv * r * g_ref[...]
    v_ref[...] = v
    for gi in range(groups):
        sl = slice(gi * gd, (gi + 1) * gd)
        z = jnp.dot(w_ref[0, gi], v[:, sl].astype(BF16), preferred_element_type=F32) + bias_ref[0, :, sl]
        o_ref[:, sl] = (u[:, sl] * z).astype(o_ref.dtype)


def _gmlp(proj, w_st, bias_st, g_v, n_prompt, gu_blk, gv_blk):
    n = proj.shape[0]
    _, groups, c, _ = w_st.shape
    gw = bias_st.shape[2]
    npc = n_prompt // c
    sel = lambda i: jnp.where(i >= npc, 1, 0)
    kern = functools.partial(_gmlp_kernel, groups=groups, gd=gw // groups)
    return pl.pallas_call(
        kern,
        grid=(n // c,),
        in_specs=[pl.BlockSpec((c, gw), lambda i: (i, gu_blk)),
                  pl.BlockSpec((c, gw), lambda i: (i, gv_blk)),
                  pl.BlockSpec((1, groups, c, c), lambda i: (sel(i), 0, 0, 0)),
                  pl.BlockSpec((1, c, gw), lambda i: (sel(i), 0, 0)),
                  pl.BlockSpec((1, gw), lambda i: (0, 0))],
        out_specs=[pl.BlockSpec((c, gw), lambda i: (i, 0)),
                   pl.BlockSpec((c, gw), lambda i: (jnp.maximum(i - npc, 0), 0))],
        out_shape=[jax.ShapeDtypeStruct((n, gw), BF16),
                   jax.ShapeDtypeStruct((n - n_prompt, gw), F32)],
        compiler_params=_params("arbitrary"),
        name="chunk_mlp",
    )(proj, proj, w_st, bias_st, g_v.reshape(1, gw))


def _branch_kernel(a_ref, b_ref, wa_ref, wb_ref, ga_ref, gb_ref, o_ref):
    ya = jnp.dot(a_ref[...], wa_ref[...], preferred_element_type=F32)
    yb = jnp.dot(b_ref[...], wb_ref[...], preferred_element_type=F32)
    o_ref[...] = (jax.nn.sigmoid(ga_ref[...]) * ya + jax.nn.sigmoid(gb_ref[...]) * yb).astype(o_ref.dtype)


def _branches(a, bm, wa, wb, proj, ga_off, gb_off, n_prompt):
    n, ka = a.shape
    kb = bm.shape[1]
    d = wa.shape[1]
    tm = _pow2_tile(512, n_prompt, n - n_prompt)
    tn = _pow2_tile(1024, d, ga_off, gb_off)
    return pl.pallas_call(
        _branch_kernel,
        grid=(d // tn, n // tm),
        in_specs=[pl.BlockSpec((tm, ka), lambda j, i: (i, 0)),
                  pl.BlockSpec((tm, kb), lambda j, i: (i, 0)),
                  pl.BlockSpec((ka, tn), lambda j, i: (0, j)),
                  pl.BlockSpec((kb, tn), lambda j, i: (0, j)),
                  pl.BlockSpec((tm, tn), lambda j, i: (i, ga_off // tn + j)),
                  pl.BlockSpec((tm, tn), lambda j, i: (i, gb_off // tn + j))],
        out_specs=pl.BlockSpec((tm, tn), lambda j, i: (i, j)),
        out_shape=jax.ShapeDtypeStruct((n, d), BF16),
        compiler_params=_params("arbitrary", "arbitrary"),
        name="branches",
    )(a, bm, wa, wb, proj, proj)


def _out_kernel(m_ref, w_ref, x_ref, gp_ref, gs_ref, o_ref, *, n_prompt_tiles):
    i = pl.program_id(1)
    y = jnp.dot(m_ref[...], w_ref[...], preferred_element_type=F32)

    @pl.when(i < n_prompt_tiles)
    def _():
        o_ref[...] = x_ref[...] + gp_ref[0] * y

    @pl.when(i >= n_prompt_tiles)
    def _():
        o_ref[...] = x_ref[...] + gs_ref[...] * y


def _out_proj(mix, w, x, mod_p, mod_s, k_gate, n_prompt, seq):
    n, d = x.shape
    tm = _pow2_tile(512, seq, n - n_prompt)
    tn = _pow2_tile(1024, d)
    npt = n_prompt // tm
    per_b = seq // tm
    nj = d // tn
    return pl.pallas_call(
        functools.partial(_out_kernel, n_prompt_tiles=npt),
        grid=(nj, n // tm),
        in_specs=[pl.BlockSpec((tm, d), lambda j, i: (i, 0)),
                  pl.BlockSpec((d, tn), lambda j, i: (0, j)),
                  pl.BlockSpec((tm, tn), lambda j, i: (i, j)),
                  pl.BlockSpec((1, 1, tn), lambda j, i: (jnp.minimum(i, npt - 1) // per_b, 0, k_gate * nj + j)),
                  pl.BlockSpec((tm, tn), lambda j, i: (jnp.maximum(i - npt, 0), k_gate * nj + j))],
        out_specs=pl.BlockSpec((tm, tn), lambda j, i: (i, j)),
        out_shape=jax.ShapeDtypeStruct((n, d), F32),
        compiler_params=_params("arbitrary", "arbitrary"),
        name="out_proj",
    )(mix, w, x, mod_p, mod_s)


def _cand_pairs(k):
    return [(i, j) for i in range(k) for j in range(k) if (i + 1) * (j + 1) <= k]


def _arg_rounds(s, iota, k, exact):
    rank = jnp.full(s.shape, float(k), F32)
    vals = []
    big = float(s.shape[0])
    for i in range(k):
        m = jnp.max(s, axis=0, keepdims=True)
        if exact:
            idx = jnp.min(jnp.where(s == m, iota, big), axis=0, keepdims=True)
            sel = iota == idx
        else:
            sel = s == m
        rank = jnp.where(sel, float(i), rank)
        s = jnp.where(sel, NEG_INF, s)
        vals.append(m)
    return vals, rank


def _peer_gates(s1, s2, jsum, cand_scr, cols, pairs, k, exact):
    n_iota = lax.broadcasted_iota(jnp.int32, s1.shape, 0).astype(F32)
    a, r1 = _arg_rounds(s1, n_iota, k, exact)
    b, r2 = _arg_rounds(s2, n_iota, k, exact)

    n_cand = cand_scr.shape[0]
    cand_scr[len(pairs):, cols] = jnp.full((n_cand - len(pairs), s1.shape[1]), NEG_INF, F32)
    for p, (i, j) in enumerate(pairs):
        cand_scr[p:p + 1, cols] = a[i] + b[j]
    c0 = cand_scr[:, cols]
    p_iota = lax.broadcasted_iota(jnp.int32, c0.shape, 0).astype(F32)
    _, prank = _arg_rounds(c0, p_iota, k, exact)
    taken = prank < float(k)
    w = jnp.where(taken, jnp.exp(c0 - (a[0] + b[0])), 0.0)
    inv_z = 1.0 / jnp.sum(w, axis=0, keepdims=True)
    ones = jnp.where(taken, 1.0, 0.0)
    jcnt = jnp.dot(jsum, ones.astype(BF16), preferred_element_type=F32)

    eaz = jnp.zeros(s1.shape, F32)
    jd = jnp.zeros(s1.shape, F32)
    eb = jnp.zeros(s1.shape, F32)
    for i in range(k):
        hit1 = r1 == float(i)
        eaz = jnp.where(hit1, jnp.exp(a[i] - a[0]) * inv_z, eaz)
        jd = jnp.where(hit1, jcnt[i:i + 1, :], jd)
        eb = jnp.where(r2 == float(i), jnp.exp(b[i] - b[0]), eb)

    def excess(rank):
        return jnp.abs(jnp.sum(jnp.where(rank < float(k), 1.0, 0.0), axis=0, keepdims=True) - float(k))

    return eaz, jd, eb, r2, excess(r1) + excess(r2) + excess(prank)


def _peer_topk_kernel(q_ref, sk_ref, jsum_ref, eaz_ref, jd_ref, eb_ref, r2_ref, cand_scr, *, half, pairs, k):
    nt = (((1,), (1,)), ((), ()))
    q = q_ref[...].astype(BF16)
    s1 = lax.dot_general(sk_ref[0, 0], q[:, :half], nt, preferred_element_type=F32)
    s2 = lax.dot_general(sk_ref[0, 1], q[:, half:], nt, preferred_element_type=F32)
    jsum = jsum_ref[...]
    n_blocks = s1.shape[1] // LANE_V7X

    def run(exact):
        bad = None
        for cb in range(n_blocks):
            cols = slice(cb * LANE_V7X, (cb + 1) * LANE_V7X)
            eaz, jd, eb, r2, dev = _peer_gates(s1[:, cols], s2[:, cols], jsum, cand_scr, cols, pairs, k, exact)
            eaz_ref[0, :, cols] = eaz
            jd_ref[0, :, cols] = jd
            eb_ref[0, :, cols] = eb.astype(eb_ref.dtype)
            r2_ref[0, :, cols] = r2.astype(r2_ref.dtype)
            bad = dev if bad is None else bad + dev
        return jnp.max(bad)

    miscount = run(exact=False)

    @pl.when(miscount > 0.5)
    def _():
        run(exact=True)


def _peer_topk(qp, sub_keys_b, n_prompt):
    n = qp.shape[0]
    heads, _, n_keys, half = sub_keys_b.shape
    tt = _pow2_tile(256, n_prompt, n - n_prompt)
    pairs = _cand_pairs(PEER_TOPK)
    n_cand = -(-len(pairs) // 64) * 64
    jsum = np.zeros((PEER_TOPK, n_cand), np.float32)
    for p, (i, _) in enumerate(pairs):
        jsum[i, p] = 1.0
    kern = functools.partial(_peer_topk_kernel, half=half, pairs=pairs, k=PEER_TOPK)
    maps = jax.ShapeDtypeStruct((heads, n_keys, n), F32)
    maps_b = jax.ShapeDtypeStruct((heads, n_keys, n), BF16)
    mspec = pl.BlockSpec((1, n_keys, tt), lambda i, h: (h, 0, i))
    return pl.pallas_call(
        kern,
        grid=(n // tt, heads),
        in_specs=[pl.BlockSpec((tt, 2 * half), lambda i, h: (i, h)),
                  pl.BlockSpec((1, 2, n_keys, half), lambda i, h: (h, 0, 0, 0)),
                  pl.BlockSpec(jsum.shape, lambda i, h: (0, 0))],
        out_specs=[mspec, mspec, mspec, mspec],
        out_shape=[maps, maps, maps_b, maps_b],
        scratch_shapes=[pltpu.VMEM((n_cand, tt), F32)],
        compiler_params=_params("arbitrary", "arbitrary"),
        name="peer_topk",
    )(qp, sub_keys_b, jnp.asarray(jsum, BF16))


def _peer_main_kernel(h_ref, u_ref, v_ref, eaz_ref, jd_ref, eb_ref, r2_ref, o_ref, ga0_scr, ga1_scr, g_scr, *,
                      heads, n_keys, rows_per_step, n_steps):
    e = pl.program_id(1)

    @pl.when(e == 0)
    def _():
        o_ref[...] = jnp.zeros_like(o_ref)
        ga1_scr[...] = jnp.zeros_like(ga1_scr)

    def stage(cur_scr, prev_scr):
        chunk = jnp.minimum(e, n_steps - 1)
        blocks = [(slice(r * n_keys, (r + 1) * n_keys), slice(tb * LANE_V7X, (tb + 1) * LANE_V7X))
                  for r in range(rows_per_step) for tb in range(h_ref.shape[0] // LANE_V7X)]
        for sl, ts in blocks:
            n1 = chunk * rows_per_step + sl.start // n_keys
            g = None
            for h in range(heads):
                ea = eaz_ref[h, pl.ds(n1, 1), :][:, ts].astype(BF16)
                jd = jd_ref[h, pl.ds(n1, 1), :][:, ts].astype(BF16)
                t = ea * jnp.where(r2_ref[h, :, ts] < jd, eb_ref[h, :, ts], jnp.zeros((), BF16))
                g = t if g is None else g + t
            g_scr[sl, ts] = g
        act_t = lax.dot_general(u_ref[...], h_ref[...], (((1,), (1,)), ((), ())),
                                preferred_element_type=F32)
        for sl, ts in blocks:
            cur_scr[ts, sl] = (g_scr[sl, ts] * jax.nn.gelu(act_t[sl, ts].astype(BF16))).T
        o_ref[...] += jnp.dot(prev_scr[...], v_ref[...], preferred_element_type=F32)

    @pl.when(e % 2 == 0)
    def _():
        stage(ga0_scr, ga1_scr)

    @pl.when(e % 2 == 1)
    def _():
        stage(ga1_scr, ga0_scr)


def _peer_main(h2, u_b, v_b, maps, n_prompt):
    n, d = h2.shape
    n_exp = u_b.shape[0]
    heads, n_keys, _ = maps[0].shape
    tt = _pow2_tile(512, n_prompt, n - n_prompt)
    ec = _pow2_tile(512, n_exp)
    rps = ec // n_keys
    n_steps = n_exp // ec
    kern = functools.partial(_peer_main_kernel, heads=heads, n_keys=n_keys, rows_per_step=rps, n_steps=n_steps)
    once = pl.Buffered(1)
    mspec = pl.BlockSpec((heads, n_keys, tt), lambda i, e: (0, 0, i), pipeline_mode=once)
    return pl.pallas_call(
        kern,
        grid=(n // tt, n_steps + 1),
        in_specs=[pl.BlockSpec((tt, d), lambda i, e: (i, 0), pipeline_mode=once),
                  pl.BlockSpec((ec, d), lambda i, e: (jnp.minimum(e, n_steps - 1), 0)),
                  pl.BlockSpec((ec, d), lambda i, e: (jnp.maximum(e - 1, 0), 0)),
                  mspec, mspec, mspec, mspec],
        out_specs=pl.BlockSpec((tt, d), lambda i, e: (i, 0), pipeline_mode=once),
        out_shape=jax.ShapeDtypeStruct((n, d), F32),
        scratch_shapes=[pltpu.VMEM((tt, ec), BF16), pltpu.VMEM((tt, ec), BF16), pltpu.VMEM((ec, tt), BF16)],
        compiler_params=_params("arbitrary", "arbitrary"),
        name="peer_main",
    )(h2, u_b, v_b, *maps)


def _final_kernel(x_ref, p_ref, gp_ref, gs_ref, g_ref, yp_ref, ys_ref, *, n_prompt_tiles):
    i = pl.program_id(0)

    def norm(x):
        return x * lax.rsqrt(jnp.mean(x * x, axis=-1, keepdims=True) + EPS) * g_ref[...]

    @pl.when(i < n_prompt_tiles)
    def _():
        yp_ref[...] = norm(x_ref[...] + gp_ref[0] * p_ref[...])

    @pl.when(i >= n_prompt_tiles)
    def _():
        ys_ref[...] = norm(x_ref[...] + gs_ref[...] * p_ref[...])


def _final(x1, peer_out, mod_p, mod_s, k_gate, g_final, n_prompt, seq):
    n, d = x1.shape
    tm = _pow2_tile(256, seq, n - n_prompt)
    npt = n_prompt // tm
    per_b = seq // tm
    return pl.pallas_call(
        functools.partial(_final_kernel, n_prompt_tiles=npt),
        grid=(n // tm,),
        in_specs=[pl.BlockSpec((tm, d), lambda i: (i, 0)),
                  pl.BlockSpec((tm, d), lambda i: (i, 0)),
                  pl.BlockSpec((1, 1, d), lambda i: (jnp.minimum(i, npt - 1) // per_b, 0, k_gate)),
                  pl.BlockSpec((tm, d), lambda i: (jnp.maximum(i - npt, 0), k_gate)),
                  pl.BlockSpec((1, d), lambda i: (0, 0))],
        out_specs=[pl.BlockSpec((tm, d), lambda i: (jnp.minimum(i, npt - 1), 0)),
                   pl.BlockSpec((tm, d), lambda i: (jnp.maximum(i - npt, 0), 0))],
        out_shape=[jax.ShapeDtypeStruct((n_prompt, d), F32),
                   jax.ShapeDtypeStruct((n - n_prompt, d), F32)],
        compiler_params=_params("arbitrary"),
        name="final_norm",
    )(x1, peer_out, mod_p, mod_s, g_final.reshape(1, d))


def kernel(x_prompt, x_sample, c_prompt, c_sample, state_hgrn, w_ada, b_ada, g_norm1, w_in, hgrn_lb_logits,
           g_hgrn_out, g_gmlp_v, w_spatial, b_spatial, w_branch_a, w_branch_b, w_out, g_norm2, w_peer_q,
           peer_sub_keys, peer_u, peer_v, g_final):
    batch, seq, d = x_prompt.shape
    db, dt, _ = x_sample.shape
    depth, _, heads, dk, dv = state_hgrn.shape
    assert depth == 1 and dk == dv == LANE_V7X
    hw = heads * dk
    groups, gc = w_spatial.shape[1], w_spatial.shape[2]
    gw = w_branch_b.shape[1]
    assert gc == LANE_V7X and gw // groups == LANE_V7X and dt <= gc and gc % dt == 0
    n_prompt, n_sample = batch * seq, db * dt
    n = n_prompt + n_sample

    c_all = jnp.concatenate([c_prompt, c_sample], axis=0)
    pad = (-c_all.shape[0]) % 8
    c_all = jnp.pad(c_all, ((0, pad), (0, 0)))
    mod = _ada(c_all, w_ada[0], b_ada[0])
    mod_p = mod[:batch].reshape(batch, 1, N_MOD * d)
    mod_s = jnp.repeat(mod[batch:batch + db], dt, axis=0)

    x = jnp.concatenate([x_prompt.reshape(n_prompt, d), x_sample.reshape(n_sample, d)], axis=0)

    h1 = _norm_mod(x, g_norm1[0], mod_p, mod_s, 1, 0, n_prompt, seq)
    proj = _matmul(h1, w_in[0], F32, n_prompt, "in_proj")

    lb = jnp.cumsum(jax.nn.softmax(hgrn_lb_logits.astype(F32), axis=0), axis=0)[0].reshape(1, hw)
    g_ho = g_hgrn_out[0].reshape(1, hw)
    a_p, st_p = _hgrn_prompt(proj, lb, g_ho, batch, seq, heads, dk, n_prompt)
    proj_s = proj[n_prompt:].reshape(db, dt, proj.shape[1])
    a_s, st_s = _hgrn_sample(proj_s, lb, g_ho, state_hgrn, heads, dk)
    a_all = jnp.concatenate([a_p, a_s.reshape(n_sample, hw).astype(BF16)], axis=0)

    tril = jnp.tril(jnp.ones((gc, gc), F32))
    w_sp = w_spatial[0]
    blk = jnp.arange(gc) // dt
    w_samp = jnp.tile(w_sp[:, :dt, :dt], (1, gc // dt, gc // dt)) * (blk[:, None] == blk[None, :])
    w_st = jnp.stack([w_sp * tril, w_samp * tril]).astype(BF16)
    bias_full = jnp.repeat(b_spatial[0].T, gw // groups, axis=1)
    bias_st = jnp.stack([bias_full, jnp.tile(bias_full[:dt], (gc // dt, 1))])
    gu_blk = 4 * hw // gw
    assert gu_blk * gw == 4 * hw
    bm, v_s = _gmlp(proj, w_st, bias_st, g_gmlp_v[0], n_prompt, gu_blk, gu_blk + 1)

    ga_off = 4 * hw + 2 * gw
    mix = _branches(a_all, bm, w_branch_a[0].astype(BF16), w_branch_b[0].astype(BF16), proj,
                    ga_off, ga_off + d, n_prompt)
    x1 = _out_proj(mix, w_out[0].astype(BF16), x, mod_p, mod_s, 2, n_prompt, seq)

    h2 = _norm_mod(x1, g_norm2[0], mod_p, mod_s, 4, 3, n_prompt, seq)
    qp = _matmul(h2, w_peer_q[0], F32, n_prompt, "peer_query")
    maps = _peer_topk(qp, peer_sub_keys[0].astype(BF16), n_prompt)
    peer_out = _peer_main(h2, peer_u[0].astype(BF16), peer_v[0].astype(BF16), maps, n_prompt)
    y_p, y_s = _final(x1, peer_out, mod_p, mod_s, 5, g_final, n_prompt, seq)

    state_p = jnp.swapaxes(st_p, -1, -2)
    return (y_p.reshape(batch, seq, d), y_s.reshape(db, dt, d), state_p, st_s,
            v_s.reshape(1, db, dt, gw))
```

```python
import functools
import math

import numpy as np
import jax
import jax.numpy as jnp
from jax import lax
from jax.experimental import pallas as pl
from jax.experimental.pallas import tpu as pltpu

EPS = 1e-6
N_MOD = 6
PEER_TOPK = 16
LANE_V7X = 128
VMEM_LIMIT_V7X = 56 * 1024 * 1024
F32 = jnp.float32
BF16 = jnp.bfloat16
NEG_INF = float("-inf")


def _params(*sem):
    return pltpu.CompilerParams(dimension_semantics=sem, vmem_limit_bytes=VMEM_LIMIT_V7X)


def _pow2_tile(target, *sizes):
    g = 0
    for s in sizes:
        g = math.gcd(g, s)
    t = 1
    while t * 2 <= target and g % (t * 2) == 0:
        t *= 2
    return t


def _silu(x):
    return x * jax.nn.sigmoid(x)


def _ada_kernel(c_ref, w_ref, b_ref, o_ref):
    s = _silu(c_ref[...]).astype(BF16)
    o_ref[...] = jnp.dot(s, w_ref[...].astype(BF16), preferred_element_type=F32) + b_ref[...]


def _ada(c_all, w_ada, b_ada):
    m, d = c_all.shape
    n = w_ada.shape[1]
    tn = _pow2_tile(512, n)
    return pl.pallas_call(
        _ada_kernel,
        grid=(n // tn,),
        in_specs=[pl.BlockSpec((m, d), lambda j: (0, 0)),
                  pl.BlockSpec((d, tn), lambda j: (0, j)),
                  pl.BlockSpec((1, tn), lambda j: (0, j))],
        out_specs=pl.BlockSpec((m, tn), lambda j: (0, j)),
        out_shape=jax.ShapeDtypeStruct((m, n), F32),
        compiler_params=_params("arbitrary"),
        name="ada_mod",
    )(c_all, w_ada, b_ada.reshape(1, n))


def _norm_mod_kernel(x_ref, g_ref, scp_ref, shp_ref, scs_ref, shs_ref, o_ref, *, n_prompt_tiles):
    i = pl.program_id(0)
    x = x_ref[...]
    r = lax.rsqrt(jnp.mean(x * x, axis=-1, keepdims=True) + EPS)
    xn = x * r * g_ref[...]

    @pl.when(i < n_prompt_tiles)
    def _():
        o_ref[...] = (xn * (1.0 + scp_ref[0]) + shp_ref[0]).astype(o_ref.dtype)

    @pl.when(i >= n_prompt_tiles)
    def _():
        o_ref[...] = (xn * (1.0 + scs_ref[...]) + shs_ref[...]).astype(o_ref.dtype)


def _norm_mod(x, g, mod_p, mod_s, k_scale, k_shift, n_prompt, seq):
    n, d = x.shape
    tm = _pow2_tile(256, seq, n - n_prompt)
    npt = n_prompt // tm
    per_b = seq // tm

    def pidx(k):
        return lambda i: (jnp.minimum(i, npt - 1) // per_b, 0, k)

    def sidx(k):
        return lambda i: (jnp.maximum(i - npt, 0), k)

    return pl.pallas_call(
        functools.partial(_norm_mod_kernel, n_prompt_tiles=npt),
        grid=(n // tm,),
        in_specs=[pl.BlockSpec((tm, d), lambda i: (i, 0)),
                  pl.BlockSpec((1, d), lambda i: (0, 0)),
                  pl.BlockSpec((1, 1, d), pidx(k_scale)),
                  pl.BlockSpec((1, 1, d), pidx(k_shift)),
                  pl.BlockSpec((tm, d), sidx(k_scale)),
                  pl.BlockSpec((tm, d), sidx(k_shift))],
        out_specs=pl.BlockSpec((tm, d), lambda i: (i, 0)),
        out_shape=jax.ShapeDtypeStruct((n, d), BF16),
        compiler_params=_params("arbitrary"),
        name="norm_mod",
    )(x, g.reshape(1, d), mod_p, mod_p, mod_s, mod_s)


def _mm_kernel(a_ref, w_ref, o_ref, wb_scr):
    @pl.when(pl.program_id(1) == 0)
    def _():
        wb_scr[...] = w_ref[...].astype(BF16)

    o_ref[...] = jnp.dot(a_ref[...], wb_scr[...], preferred_element_type=F32).astype(o_ref.dtype)


def _matmul(a, w, out_dtype, n_prompt, name):
    n, k = a.shape
    nc = w.shape[1]
    tm = _pow2_tile(512, n_prompt, n - n_prompt)
    tn = _pow2_tile(1024, nc)
    return pl.pallas_call(
        _mm_kernel,
        grid=(nc // tn, n // tm),
        in_specs=[pl.BlockSpec((tm, k), lambda j, i: (i, 0)),
                  pl.BlockSpec((k, tn), lambda j, i: (0, j))],
        out_specs=pl.BlockSpec((tm, tn), lambda j, i: (i, j)),
        out_shape=jax.ShapeDtypeStruct((n, nc), out_dtype),
        scratch_shapes=[pltpu.VMEM((k, tn), BF16)],
        compiler_params=_params("arbitrary", "arbitrary"),
        name=name,
    )(a, w)


def _hgrn_tables(c):
    nl = int(math.log2(c))
    r = np.arange(c)[:, None]
    j = np.arange(c)[None, :]
    mats = []
    lvl = np.full((c, c), -1, np.int32)
    for l in range(nl):
        m = c >> (l + 1)
        mid = (r // (2 * m)) * 2 * m + m
        upper = r >= mid
        t = np.where(upper, (j >= mid) & (j <= r), (j > r) & (j <= mid - 1))
        mats.append(t)
        same = (r // (2 * m)) == (j // (2 * m))
        lvl = np.where(same & upper & (j < mid), l, lvl)
    lvl = np.where(r == j, nl, lvl)
    mats.append(j <= r)
    mats.append(j > r)
    t_all = np.concatenate(mats, axis=0).astype(np.float32)
    return jnp.asarray(t_all, BF16), jnp.asarray(lvl), nl


def _hgrn_prompt_kernel(q_ref, f_ref, v_ref, og_ref, lb_ref, g_ref, tall_ref, lvl_ref,
                        a_ref, st_ref, s_scr, *, chunk, n_levels, n_chunks, heads_per_step, dk):
    c = chunk
    s_scr[...] = jnp.zeros_like(s_scr)
    lb = lb_ref[...]
    lvl = lvl_ref[...]
    rows = lax.broadcasted_iota(jnp.int32, (c, q_ref.shape[1]), 0)
    nt = (((1,), (1,)), ((), ()))

    def body(ci, carry):
        sl = pl.ds(pl.multiple_of(ci * c, c), c)
        qs = _silu(q_ref[sl, :])
        f = lb + (1.0 - lb) * jax.nn.sigmoid(f_ref[sl, :])
        k = 1.0 - f
        lf = jnp.log(f)
        v = v_ref[sl, :].astype(BF16)
        lf_hi = lf.astype(BF16)
        lf_lo = (lf - lf_hi.astype(F32)).astype(BF16)
        ex = (jnp.dot(tall_ref[...], lf_hi, preferred_element_type=F32)
              + jnp.dot(tall_ref[...], lf_lo, preferred_element_type=F32))
        qs_b = qs.astype(BF16)
        k_b = k.astype(BF16)
        ys = []
        for l in range(n_levels):
            m = c >> (l + 1)
            e_l = jnp.exp(ex[l * c:(l + 1) * c, :])
            ys.append((e_l * jnp.where((rows & m) != 0, qs, k)).astype(BF16))
        bcum = ex[n_levels * c:(n_levels + 1) * c, :]
        brev = ex[(n_levels + 1) * c:(n_levels + 2) * c, :]
        qh = (qs * jnp.exp(bcum)).astype(BF16)
        kh = (k * jnp.exp(brev)).astype(BF16)
        decay = jnp.exp(bcum[c - 1:c, :])
        outs = []
        for hh in range(heads_per_step):
            hs = slice(hh * dk, (hh + 1) * dk)
            scores = jnp.where(lvl == n_levels,
                               lax.dot_general(qs_b[:, hs], k_b[:, hs], nt, preferred_element_type=F32), 0.0)
            for l in range(n_levels):
                y = ys[l][:, hs]
                p = lax.dot_general(y, y, nt, preferred_element_type=F32)
                scores = scores + jnp.where(lvl == l, p, 0.0)
            st = s_scr[hh]
            o = jnp.dot(scores.astype(BF16), v[:, hs], preferred_element_type=F32)
            o = o + lax.dot_general(qh[:, hs], st.astype(BF16), nt, preferred_element_type=F32)
            kv_t = lax.dot_general(v[:, hs], kh[:, hs], (((0,), (0,)), ((), ())), preferred_element_type=F32)
            s_scr[hh] = st * decay[:, hs] + kv_t
            outs.append(o * lax.rsqrt(jnp.mean(o * o, axis=-1, keepdims=True) + EPS))
        on = jnp.concatenate(outs, axis=1) if heads_per_step > 1 else outs[0]
        a_ref[sl, :] = (on * g_ref[...] * _silu(og_ref[sl, :])).astype(a_ref.dtype)
        return carry

    lax.fori_loop(0, n_chunks, body, 0)
    st_ref[0, 0] = s_scr[...]


def _hgrn_prompt(proj, lb, g_out, batch, seq, heads, dk, n_rows):
    c = _pow2_tile(128, seq)
    hp = _pow2_tile(4, heads)
    t_all, lvl, nl = _hgrn_tables(c)
    w = hp * dk
    nh = heads // hp
    col = lambda off: (lambda b, h: (b, off + h))
    kern = functools.partial(_hgrn_prompt_kernel, chunk=c, n_levels=nl, n_chunks=seq // c,
                             heads_per_step=hp, dk=dk)
    return pl.pallas_call(
        kern,
        grid=(batch, nh),
        in_specs=[pl.BlockSpec((seq, w), col(0)),
                  pl.BlockSpec((seq, w), col(nh)),
                  pl.BlockSpec((seq, w), col(2 * nh)),
                  pl.BlockSpec((seq, w), col(3 * nh)),
                  pl.BlockSpec((1, w), lambda b, h: (0, h)),
                  pl.BlockSpec((1, w), lambda b, h: (0, h)),
                  pl.BlockSpec(t_all.shape, lambda b, h: (0, 0)),
                  pl.BlockSpec(lvl.shape, lambda b, h: (0, 0))],
        out_specs=[pl.BlockSpec((seq, w), lambda b, h: (b, h)),
                   pl.BlockSpec((1, 1, hp, dk, dk), lambda b, h: (0, b, h, 0, 0))],
        out_shape=[jax.ShapeDtypeStruct((n_rows, heads * dk), BF16),
                   jax.ShapeDtypeStruct((1, batch, heads, dk, dk), F32)],
        scratch_shapes=[pltpu.VMEM((hp, dk, dk), F32)],
        compiler_params=_params("arbitrary", "arbitrary"),
        name="hgrn_prompt",
    )(proj, proj, proj, proj, lb, g_out, t_all, lvl)


def _hgrn_sample_kernel(q_ref, f_ref, v_ref, og_ref, lb_ref, g_ref, s0_ref, a_ref, s_ref, *, bt, steps):
    lb = lb_ref[...]
    g = g_ref[...]
    rows = lax.broadcasted_iota(jnp.int32, (steps, q_ref.shape[2]), 0)

    def body(b, carry):
        qs = _silu(q_ref[b])
        f = lb + (1.0 - lb) * jax.nn.sigmoid(f_ref[b])
        k = 1.0 - f
        lf = jnp.log(f)
        v = v_ref[b]
        cum = []
        run = None
        for t in range(steps):
            run = lf[t:t + 1, :] if run is None else run + lf[t:t + 1, :]
            cum.append(run)
        bmat = jnp.broadcast_to(cum[0], lf.shape)
        for t in range(1, steps):
            bmat = jnp.where(rows == t, cum[t], bmat)
        o = jnp.zeros_like(qs)
        for s in range(steps):
            x = jnp.where(rows >= s, qs * k[s:s + 1, :] * jnp.exp(jnp.minimum(bmat - cum[s], 0.0)), 0.0)
            o = o + jnp.sum(x, axis=-1, keepdims=True) * v[s:s + 1, :]
        s0 = s0_ref[0, b, 0]
        o = o + jnp.dot((qs * jnp.exp(bmat)).astype(BF16), s0.astype(BF16), preferred_element_type=F32)
        kh = (k * jnp.exp(cum[steps - 1] - bmat)).astype(BF16)
        kv = lax.dot_general(kh, v.astype(BF16), (((0,), (0,)), ((), ())), preferred_element_type=F32)
        decay_col = jnp.broadcast_to(jnp.exp(cum[steps - 1]), s0.shape).T
        s_ref[0, b, 0] = s0 * decay_col + kv
        on = o * lax.rsqrt(jnp.mean(o * o, axis=-1, keepdims=True) + EPS)
        a_ref[b] = on * g * _silu(og_ref[b])
        return carry

    lax.fori_loop(0, bt, body, 0, unroll=8)


def _hgrn_sample(proj_s, lb, g_out, s0, heads, dk):
    db, steps, _ = proj_s.shape
    bt = _pow2_tile(32, db)
    col = lambda off: (lambda j, h: (j, 0, off + h))
    kern = functools.partial(_hgrn_sample_kernel, bt=bt, steps=steps)
    return pl.pallas_call(
        kern,
        grid=(db // bt, heads),
        in_specs=[pl.BlockSpec((bt, steps, dk), col(0)),
                  pl.BlockSpec((bt, steps, dk), col(heads)),
                  pl.BlockSpec((bt, steps, dk), col(2 * heads)),
                  pl.BlockSpec((bt, steps, dk), col(3 * heads)),
                  pl.BlockSpec((1, dk), lambda j, h: (0, h)),
                  pl.BlockSpec((1, dk), lambda j, h: (0, h)),
                  pl.BlockSpec((1, bt, 1, dk, dk), lambda j, h: (0, j, h, 0, 0))],
        out_specs=[pl.BlockSpec((bt, steps, dk), lambda j, h: (j, 0, h)),
                   pl.BlockSpec((1, bt, 1, dk, dk), lambda j, h: (0, j, h, 0, 0))],
        out_shape=[jax.ShapeDtypeStruct((db, steps, heads * dk), F32),
                   jax.ShapeDtypeStruct(s0.shape, F32)],
        compiler_params=_params("arbitrary", "arbitrary"),
        name="hgrn_sample",
    )(proj_s, proj_s, proj_s, proj_s, lb, g_out, s0)


def _gmlp_kernel(gu_ref, gv_ref, w_ref, bias_ref, g_ref, o_ref, v_ref, *, groups, gd):
    u = jax.nn.gelu(gu_ref[...])
    vv = jax.nn.gelu(gv_ref[...])
    r = lax.rsqrt(jnp.mean(vv * vv, axis=-1, keepdims=True) + EPS)
    v = vv * r * g_ref[...]
    v_ref[...] = v
    for gi in range(groups):
        sl = slice(gi * gd, (gi + 1) * gd)
        z = jnp.dot(w_ref[0, gi], v[:, sl].astype(BF16), preferred_element_type=F32) + bias_ref[0, :, sl]
        o_ref[:, sl] = (u[:, sl] * z).astype(o_ref.dtype)


def _gmlp(proj, w_st, bias_st, g_v, n_prompt, gu_blk, gv_blk):
    n = proj.shape[0]
    _, groups, c, _ = w_st.shape
    gw = bias_st.shape[2]
    npc = n_prompt // c
    sel = lambda i: jnp.where(i >= npc, 1, 0)
    kern = functools.partial(_gmlp_kernel, groups=groups, gd=gw // groups)
    return pl.pallas_call(
        kern,
        grid=(n // c,),
        in_specs=[pl.BlockSpec((c, gw), lambda i: (i, gu_blk)),
                  pl.BlockSpec((c, gw), lambda i: (i, gv_blk)),
                  pl.BlockSpec((1, groups, c, c), lambda i: (sel(i), 0, 0, 0)),
                  pl.BlockSpec((1, c, gw), lambda i: (sel(i), 0, 0)),
                  pl.BlockSpec((1, gw), lambda i: (0, 0))],
        out_specs=[pl.BlockSpec((c, gw), lambda i: (i, 0)),
                   pl.BlockSpec((c, gw), lambda i: (jnp.maximum(i - npc, 0), 0))],
        out_shape=[jax.ShapeDtypeStruct((n, gw), BF16),
                   jax.ShapeDtypeStruct((n - n_prompt, gw), F32)],
        compiler_params=_params("arbitrary"),
        name="chunk_mlp",
    )(proj, proj, w_st, bias_st, g_v.reshape(1, gw))


def _branch_kernel(a_ref, b_ref, wa_ref, wb_ref, ga_ref, gb_ref, o_ref):
    ya = jnp.dot(a_ref[...], wa_ref[...], preferred_element_type=F32)
    yb = jnp.dot(b_ref[...], wb_ref[...], preferred_element_type=F32)
    o_ref[...] = (jax.nn.sigmoid(ga_ref[...]) * ya + jax.nn.sigmoid(gb_ref[...]) * yb).astype(o_ref.dtype)


def _branches(a, bm, wa, wb, proj, ga_off, gb_off, n_prompt):
    n, ka = a.shape
    kb = bm.shape[1]
    d = wa.shape[1]
    tm = _pow2_tile(512, n_prompt, n - n_prompt)
    tn = _pow2_tile(1024, d, ga_off, gb_off)
    return pl.pallas_call(
        _branch_kernel,
        grid=(d // tn, n // tm),
        in_specs=[pl.BlockSpec((tm, ka), lambda j, i: (i, 0)),
                  pl.BlockSpec((tm, kb), lambda j, i: (i, 0)),
                  pl.BlockSpec((ka, tn), lambda j, i: (0, j)),
                  pl.BlockSpec((kb, tn), lambda j, i: (0, j)),
                  pl.BlockSpec((tm, tn), lambda j, i: (i, ga_off // tn + j)),
                  pl.BlockSpec((tm, tn), lambda j, i: (i, gb_off // tn + j))],
        out_specs=pl.BlockSpec((tm, tn), lambda j, i: (i, j)),
        out_shape=jax.ShapeDtypeStruct((n, d), BF16),
        compiler_params=_params("arbitrary", "arbitrary"),
        name="branches",
    )(a, bm, wa, wb, proj, proj)


def _out_kernel(m_ref, w_ref, x_ref, gp_ref, gs_ref, o_ref, *, n_prompt_tiles):
    i = pl.program_id(1)
    y = jnp.dot(m_ref[...], w_ref[...], preferred_element_type=F32)

    @pl.when(i < n_prompt_tiles)
    def _():
        o_ref[...] = x_ref[...] + gp_ref[0] * y

    @pl.when(i >= n_prompt_tiles)
    def _():
        o_ref[...] = x_ref[...] + gs_ref[...] * y


def _out_proj(mix, w, x, mod_p, mod_s, k_gate, n_prompt, seq):
    n, d = x.shape
    tm = _pow2_tile(512, seq, n - n_prompt)
    tn = _pow2_tile(1024, d)
    npt = n_prompt // tm
    per_b = seq // tm
    nj = d // tn
    return pl.pallas_call(
        functools.partial(_out_kernel, n_prompt_tiles=npt),
        grid=(nj, n // tm),
        in_specs=[pl.BlockSpec((tm, d), lambda j, i: (i, 0)),
                  pl.BlockSpec((d, tn), lambda j, i: (0, j)),
                  pl.BlockSpec((tm, tn), lambda j, i: (i, j)),
                  pl.BlockSpec((1, 1, tn), lambda j, i: (jnp.minimum(i, npt - 1) // per_b, 0, k_gate * nj + j)),
                  pl.BlockSpec((tm, tn), lambda j, i: (jnp.maximum(i - npt, 0), k_gate * nj + j))],
        out_specs=pl.BlockSpec((tm, tn), lambda j, i: (i, j)),
        out_shape=jax.ShapeDtypeStruct((n, d), F32),
        compiler_params=_params("arbitrary", "arbitrary"),
        name="out_proj",
    )(mix, w, x, mod_p, mod_s)


def _cand_pairs(k):
    return [(i, j) for i in range(k) for j in range(k) if (i + 1) * (j + 1) <= k]


def _arg_rounds(s, iota, k, exact):
    rank = jnp.full(s.shape, float(k), F32)
    vals = []
    big = float(s.shape[0])
    for i in range(k):
        m = jnp.max(s, axis=0, keepdims=True)
        if exact:
            idx = jnp.min(jnp.where(s == m, iota, big), axis=0, keepdims=True)
            sel = iota == idx
        else:
            sel = s == m
        rank = jnp.where(sel, float(i), rank)
        s = jnp.where(sel, NEG_INF, s)
        vals.append(m)
    return vals, rank


def _peer_gates(s1, s2, jsum, cand_scr, cols, pairs, k, exact):
    n_iota = lax.broadcasted_iota(jnp.int32, s1.shape, 0).astype(F32)
    a, r1 = _arg_rounds(s1, n_iota, k, exact)
    b, r2 = _arg_rounds(s2, n_iota, k, exact)

    n_cand = cand_scr.shape[0]
    cand_scr[len(pairs):, cols] = jnp.full((n_cand - len(pairs), s1.shape[1]), NEG_INF, F32)
    for p, (i, j) in enumerate(pairs):
        cand_scr[p:p + 1, cols] = a[i] + b[j]
    c0 = cand_scr[:, cols]
    p_iota = lax.broadcasted_iota(jnp.int32, c0.shape, 0).astype(F32)
    _, prank = _arg_rounds(c0, p_iota, k, exact)
    taken = prank < float(k)
    w = jnp.where(taken, jnp.exp(c0 - (a[0] + b[0])), 0.0)
    inv_z = 1.0 / jnp.sum(w, axis=0, keepdims=True)
    ones = jnp.where(taken, 1.0, 0.0)
    jcnt = jnp.dot(jsum, ones.astype(BF16), preferred_element_type=F32)

    eaz = jnp.zeros(s1.shape, F32)
    jd = jnp.zeros(s1.shape, F32)
    eb = jnp.zeros(s1.shape, F32)
    for i in range(k):
        hit1 = r1 == float(i)
        eaz = jnp.where(hit1, jnp.exp(a[i] - a[0]) * inv_z, eaz)
        jd = jnp.where(hit1, jcnt[i:i + 1, :], jd)
        eb = jnp.where(r2 == float(i), jnp.exp(b[i] - b[0]), eb)

    def excess(rank):
        return jnp.abs(jnp.sum(jnp.where(rank < float(k), 1.0, 0.0), axis=0, keepdims=True) - float(k))

    return eaz, jd, eb, r2, excess(r1) + excess(r2) + excess(prank)


def _peer_topk_kernel(q_ref, sk_ref, jsum_ref, eaz_ref, jd_ref, eb_ref, r2_ref, cand_scr, *, half, pairs, k):
    nt = (((1,), (1,)), ((), ()))
    q = q_ref[...].astype(BF16)
    s1 = lax.dot_general(sk_ref[0, 0], q[:, :half], nt, preferred_element_type=F32)
    s2 = lax.dot_general(sk_ref[0, 1], q[:, half:], nt, preferred_element_type=F32)
    jsum = jsum_ref[...]
    n_blocks = s1.shape[1] // LANE_V7X

    def run(exact):
        bad = None
        for cb in range(n_blocks):
            cols = slice(cb * LANE_V7X, (cb + 1) * LANE_V7X)
            eaz, jd, eb, r2, dev = _peer_gates(s1[:, cols], s2[:, cols], jsum, cand_scr, cols, pairs, k, exact)
            eaz_ref[0, :, cols] = eaz
            jd_ref[0, :, cols] = jd
            eb_ref[0, :, cols] = eb.astype(eb_ref.dtype)
            r2_ref[0, :, cols] = r2.astype(r2_ref.dtype)
            bad = dev if bad is None else bad + dev
        return jnp.max(bad)

    miscount = run(exact=False)

    @pl.when(miscount > 0.5)
    def _():
        run(exact=True)


def _peer_topk(qp, sub_keys_b, n_prompt):
    n = qp.shape[0]
    heads, _, n_keys, half = sub_keys_b.shape
    tt = _pow2_tile(256, n_prompt, n - n_prompt)
    pairs = _cand_pairs(PEER_TOPK)
    n_cand = -(-len(pairs) // 64) * 64
    jsum = np.zeros((PEER_TOPK, n_cand), np.float32)
    for p, (i, _) in enumerate(pairs):
        jsum[i, p] = 1.0
    kern = functools.partial(_peer_topk_kernel, half=half, pairs=pairs, k=PEER_TOPK)
    maps = jax.ShapeDtypeStruct((heads, n_keys, n), F32)
    maps_b = jax.ShapeDtypeStruct((heads, n_keys, n), BF16)
    mspec = pl.BlockSpec((1, n_keys, tt), lambda i, h: (h, 0, i))
    return pl.pallas_call(
        kern,
        grid=(n // tt, heads),
        in_specs=[pl.BlockSpec((tt, 2 * half), lambda i, h: (i, h)),
                  pl.BlockSpec((1, 2, n_keys, half), lambda i, h: (h, 0, 0, 0)),
                  pl.BlockSpec(jsum.shape, lambda i, h: (0, 0))],
        out_specs=[mspec, mspec, mspec, mspec],
        out_shape=[maps, maps, maps_b, maps_b],
        scratch_shapes=[pltpu.VMEM((n_cand, tt), F32)],
        compiler_params=_params("arbitrary", "arbitrary"),
        name="peer_topk",
    )(qp, sub_keys_b, jnp.asarray(jsum, BF16))


def _peer_main_kernel(h_ref, u_ref, v_ref, eaz_ref, jd_ref, eb_ref, r2_ref, o_ref,
                      ga0_scr, ga1_scr, g0_scr, g1_scr, *, heads, n_keys, rows_per_step, n_steps):
    e = pl.program_id(1)
    blocks = [(slice(r * n_keys, (r + 1) * n_keys), slice(tb * LANE_V7X, (tb + 1) * LANE_V7X))
              for r in range(rows_per_step) for tb in range(h_ref.shape[0] // LANE_V7X)]
    n_slices = math.gcd(len(blocks), 16)

    def gates(chunk, g_scr, part=None):
        for sl, ts in (blocks if part is None else blocks[part::n_slices]):
            n1 = chunk * rows_per_step + sl.start // n_keys
            g = None
            for h in range(heads):
                ea = eaz_ref[h, pl.ds(n1, 1), :][:, ts].astype(BF16)
                jd = jd_ref[h, pl.ds(n1, 1), :][:, ts].astype(BF16)
                t = ea * jnp.where(r2_ref[h, :, ts] < jd, eb_ref[h, :, ts], jnp.zeros((), BF16))
                g = t if g is None else g + t
            g_scr[sl, ts] = g

    @pl.when(e == 0)
    def _():
        o_ref[...] = jnp.zeros_like(o_ref)
        ga1_scr[...] = jnp.zeros_like(ga1_scr)
        gates(0, g0_scr)

    def stage(ga_cur, ga_prev, g_cur, g_next):
        act_t = lax.dot_general(u_ref[...], h_ref[...], (((1,), (1,)), ((), ())),
                                preferred_element_type=F32)
        dq = o_ref.shape[1] // n_slices
        for q in range(n_slices):
            qs = slice(q * dq, (q + 1) * dq)
            o_ref[:, qs] += jnp.dot(ga_prev[...], v_ref[:, qs], preferred_element_type=F32)
            for sl, ts in blocks[q::n_slices]:
                ga_cur[ts, sl] = (g_cur[sl, ts] * jax.nn.gelu(act_t[sl, ts].astype(BF16))).T
            gates(jnp.minimum(e + 1, n_steps - 1), g_next, part=q)

    @pl.when(e % 2 == 0)
    def _():
        stage(ga0_scr, ga1_scr, g0_scr, g1_scr)

    @pl.when(e % 2 == 1)
    def _():
        stage(ga1_scr, ga0_scr, g1_scr, g0_scr)


def _peer_main(h2, u_b, v_b, maps, n_prompt):
    n, d = h2.shape
    n_exp = u_b.shape[0]
    heads, n_keys, _ = maps[0].shape
    tt = _pow2_tile(512, n_prompt, n - n_prompt)
    ec = _pow2_tile(512, n_exp)
    rps = ec // n_keys
    n_steps = n_exp // ec
    kern = functools.partial(_peer_main_kernel, heads=heads, n_keys=n_keys, rows_per_step=rps, n_steps=n_steps)
    once = pl.Buffered(1)
    mspec = pl.BlockSpec((heads, n_keys, tt), lambda i, e: (0, 0, i), pipeline_mode=once)
    return pl.pallas_call(
        kern,
        grid=(n // tt, n_steps + 1),
        in_specs=[pl.BlockSpec((tt, d), lambda i, e: (i, 0), pipeline_mode=once),
                  pl.BlockSpec((ec, d), lambda i, e: (jnp.minimum(e, n_steps - 1), 0)),
                  pl.BlockSpec((ec, d), lambda i, e: (jnp.maximum(e - 1, 0), 0)),
                  mspec, mspec, mspec, mspec],
        out_specs=pl.BlockSpec((tt, d), lambda i, e: (i, 0), pipeline_mode=once),
        out_shape=jax.ShapeDtypeStruct((n, d), F32),
        scratch_shapes=[pltpu.VMEM((tt, ec), BF16), pltpu.VMEM((tt, ec), BF16),
                        pltpu.VMEM((ec, tt), BF16), pltpu.VMEM((ec, tt), BF16)],
        compiler_params=_params("arbitrary", "arbitrary"),
        name="peer_main",
    )(h2, u_b, v_b, *maps)


def _final_kernel(x_ref, p_ref, gp_ref, gs_ref, g_ref, yp_ref, ys_ref, *, n_prompt_tiles):
    i = pl.program_id(0)

    def norm(x):
        return x * lax.rsqrt(jnp.mean(x * x, axis=-1, keepdims=True) + EPS) * g_ref[...]

    @pl.when(i < n_prompt_tiles)
    def _():
        yp_ref[...] = norm(x_ref[...] + gp_ref[0] * p_ref[...])

    @pl.when(i >= n_prompt_tiles)
    def _():
        ys_ref[...] = norm(x_ref[...] + gs_ref[...] * p_ref[...])


def _final(x1, peer_out, mod_p, mod_s, k_gate, g_final, n_prompt, seq):
    n, d = x1.shape
    tm = _pow2_tile(256, seq, n - n_prompt)
    npt = n_prompt // tm
    per_b = seq // tm
    return pl.pallas_call(
        functools.partial(_final_kernel, n_prompt_tiles=npt),
        grid=(n // tm,),
        in_specs=[pl.BlockSpec((tm, d), lambda i: (i, 0)),
                  pl.BlockSpec((tm, d), lambda i: (i, 0)),
                  pl.BlockSpec((1, 1, d), lambda i: (jnp.minimum(i, npt - 1) // per_b, 0, k_gate)),
                  pl.BlockSpec((tm, d), lambda i: (jnp.maximum(i - npt, 0), k_gate)),
                  pl.BlockSpec((1, d), lambda i: (0, 0))],
        out_specs=[pl.BlockSpec((tm, d), lambda i: (jnp.minimum(i, npt - 1), 0)),
                   pl.BlockSpec((tm, d), lambda i: (jnp.maximum(i - npt, 0), 0))],
        out_shape=[jax.ShapeDtypeStruct((n_prompt, d), F32),
                   jax.ShapeDtypeStruct((n - n_prompt, d), F32)],
        compiler_params=_params("arbitrary"),
        name="final_norm",
    )(x1, peer_out, mod_p, mod_s, g_final.reshape(1, d))


def kernel(x_prompt, x_sample, c_prompt, c_sample, state_hgrn, w_ada, b_ada, g_norm1, w_in, hgrn_lb_logits,
           g_hgrn_out, g_gmlp_v, w_spatial, b_spatial, w_branch_a, w_branch_b, w_out, g_norm2, w_peer_q,
           peer_sub_keys, peer_u, peer_v, g_final):
    batch, seq, d = x_prompt.shape
    db, dt, _ = x_sample.shape
    depth, _, heads, dk, dv = state_hgrn.shape
    assert depth == 1 and dk == dv == LANE_V7X
    hw = heads * dk
    groups, gc = w_spatial.shape[1], w_spatial.shape[2]
    gw = w_branch_b.shape[1]
    assert gc == LANE_V7X and gw // groups == LANE_V7X and dt <= gc and gc % dt == 0
    n_prompt, n_sample = batch * seq, db * dt
    n = n_prompt + n_sample

    c_all = jnp.concatenate([c_prompt, c_sample], axis=0)
    pad = (-c_all.shape[0]) % 8
    c_all = jnp.pad(c_all, ((0, pad), (0, 0)))
    mod = _ada(c_all, w_ada[0], b_ada[0])
    mod_p = mod[:batch].reshape(batch, 1, N_MOD * d)
    mod_s = jnp.repeat(mod[batch:batch + db], dt, axis=0)

    x = jnp.concatenate([x_prompt.reshape(n_prompt, d), x_sample.reshape(n_sample, d)], axis=0)

    h1 = _norm_mod(x, g_norm1[0], mod_p, mod_s, 1, 0, n_prompt, seq)
    proj = _matmul(h1, w_in[0], F32, n_prompt, "in_proj")

    lb = jnp.cumsum(jax.nn.softmax(hgrn_lb_logits.astype(F32), axis=0), axis=0)[0].reshape(1, hw)
    g_ho = g_hgrn_out[0].reshape(1, hw)
    a_p, st_p = _hgrn_prompt(proj, lb, g_ho, batch, seq, heads, dk, n_prompt)
    proj_s = proj[n_prompt:].reshape(db, dt, proj.shape[1])
    a_s, st_s = _hgrn_sample(proj_s, lb, g_ho, state_hgrn, heads, dk)
    a_all = jnp.concatenate([a_p, a_s.reshape(n_sample, hw).astype(BF16)], axis=0)

    tril = jnp.tril(jnp.ones((gc, gc), F32))
    w_sp = w_spatial[0]
    blk = jnp.arange(gc) // dt
    w_samp = jnp.tile(w_sp[:, :dt, :dt], (1, gc // dt, gc // dt)) * (blk[:, None] == blk[None, :])
    w_st = jnp.stack([w_sp * tril, w_samp * tril]).astype(BF16)
    bias_full = jnp.repeat(b_spatial[0].T, gw // groups, axis=1)
    bias_st = jnp.stack([bias_full, jnp.tile(bias_full[:dt], (gc // dt, 1))])
    gu_blk = 4 * hw // gw
    assert gu_blk * gw == 4 * hw
    bm, v_s = _gmlp(proj, w_st, bias_st, g_gmlp_v[0], n_prompt, gu_blk, gu_blk + 1)

    ga_off = 4 * hw + 2 * gw
    mix = _branches(a_all, bm, w_branch_a[0].astype(BF16), w_branch_b[0].astype(BF16), proj,
                    ga_off, ga_off + d, n_prompt)
    x1 = _out_proj(mix, w_out[0].astype(BF16), x, mod_p, mod_s, 2, n_prompt, seq)

    h2 = _norm_mod(x1, g_norm2[0], mod_p, mod_s, 4, 3, n_prompt, seq)
    qp = _matmul(h2, w_peer_q[0], F32, n_prompt, "peer_query")
    maps = _peer_topk(qp, peer_sub_keys[0].astype(BF16), n_prompt)
    peer_out = _peer_main(h2, peer_u[0].astype(BF16), peer_v[0].astype(BF16), maps, n_prompt)
    y_p, y_s = _final(x1, peer_out, mod_p, mod_s, 5, g_final, n_prompt, seq)

    state_p = jnp.swapaxes(st_p, -1, -2)
    return (y_p.reshape(batch, seq, d), y_s.reshape(db, dt, d), state_p, st_s,
            v_s.reshape(1, db, dt, gw))
```

```python
import functools
import math

import numpy as np
import jax
import jax.numpy as jnp
from jax import lax
from jax.experimental import pallas as pl
from jax.experimental.pallas import tpu as pltpu

EPS = 1e-6
N_MOD = 6
PEER_TOPK = 16
LANE_V7X = 128
VMEM_LIMIT_V7X = 60 * 1024 * 1024
F32 = jnp.float32
BF16 = jnp.bfloat16
NEG_INF = float("-inf")


def _params(*sem):
    return pltpu.CompilerParams(dimension_semantics=sem, vmem_limit_bytes=VMEM_LIMIT_V7X)


def _pow2_tile(target, *sizes):
    g = 0
    for s in sizes:
        g = math.gcd(g, s)
    t = 1
    while t * 2 <= target and g % (t * 2) == 0:
        t *= 2
    return t


def _silu(x):
    return x * jax.nn.sigmoid(x)


def _ada_kernel(c_ref, w_ref, b_ref, o_ref):
    s = _silu(c_ref[...]).astype(BF16)
    o_ref[...] = jnp.dot(s, w_ref[...].astype(BF16), preferred_element_type=F32) + b_ref[...]


def _ada(c_all, w_ada, b_ada):
    m, d = c_all.shape
    n = w_ada.shape[1]
    tn = _pow2_tile(512, n)
    return pl.pallas_call(
        _ada_kernel,
        grid=(n // tn,),
        in_specs=[pl.BlockSpec((m, d), lambda j: (0, 0)),
                  pl.BlockSpec((d, tn), lambda j: (0, j)),
                  pl.BlockSpec((1, tn), lambda j: (0, j))],
        out_specs=pl.BlockSpec((m, tn), lambda j: (0, j)),
        out_shape=jax.ShapeDtypeStruct((m, n), F32),
        compiler_params=_params("arbitrary"),
        name="ada_mod",
    )(c_all, w_ada, b_ada.reshape(1, n))


def _norm_mod_kernel(xp_ref, xs_ref, g_ref, scp_ref, shp_ref, scs_ref, shs_ref, o_ref, *, n_prompt_tiles):
    i = pl.program_id(0)

    def normed(x):
        return x * lax.rsqrt(jnp.mean(x * x, axis=-1, keepdims=True) + EPS) * g_ref[...]

    @pl.when(i < n_prompt_tiles)
    def _():
        o_ref[...] = (normed(xp_ref[...]) * (1.0 + scp_ref[0]) + shp_ref[0]).astype(o_ref.dtype)

    @pl.when(i >= n_prompt_tiles)
    def _():
        o_ref[...] = (normed(xs_ref[...]) * (1.0 + scs_ref[...]) + shs_ref[...]).astype(o_ref.dtype)


def _row_sources(x_p, x_s, n_prompt, tm):
    npt = n_prompt // tm
    s_off = npt if x_s is x_p else 0
    return (lambda i: (jnp.minimum(i, npt - 1), 0)), (lambda i: (s_off + jnp.maximum(i - npt, 0), 0))


def _norm_mod(x_p, x_s, g, mod_p, mod_s, k_scale, k_shift, n_prompt, n_sample, seq):
    d = x_p.shape[1]
    n = n_prompt + n_sample
    tm = _pow2_tile(256, seq, n_sample)
    npt = n_prompt // tm
    per_b = seq // tm
    p_map, s_map = _row_sources(x_p, x_s, n_prompt, tm)

    def pidx(k):
        return lambda i: (jnp.minimum(i, npt - 1) // per_b, 0, k)

    def sidx(k):
        return lambda i: (jnp.maximum(i - npt, 0), k)

    return pl.pallas_call(
        functools.partial(_norm_mod_kernel, n_prompt_tiles=npt),
        grid=(n // tm,),
        in_specs=[pl.BlockSpec((tm, d), p_map),
                  pl.BlockSpec((tm, d), s_map),
                  pl.BlockSpec((1, d), lambda i: (0, 0)),
                  pl.BlockSpec((1, 1, d), pidx(k_scale)),
                  pl.BlockSpec((1, 1, d), pidx(k_shift)),
                  pl.BlockSpec((tm, d), sidx(k_scale)),
                  pl.BlockSpec((tm, d), sidx(k_shift))],
        out_specs=pl.BlockSpec((tm, d), lambda i: (i, 0)),
        out_shape=jax.ShapeDtypeStruct((n, d), BF16),
        compiler_params=_params("arbitrary"),
        name="norm_mod",
    )(x_p, x_s, g.reshape(1, d), mod_p, mod_p, mod_s, mod_s)


def _mm_kernel(a_ref, w_ref, *rest, n_side):
    side_in, o_ref, side_out, wb_scr = rest[:n_side], rest[n_side], rest[n_side + 1:2 * n_side + 1], rest[-1]

    @pl.when(pl.program_id(1) == 0)
    def _():
        wb_scr[...] = w_ref[...].astype(BF16)

    o_ref[...] = jnp.dot(a_ref[...], wb_scr[...], preferred_element_type=F32).astype(o_ref.dtype)
    for src, dst in zip(side_in, side_out):
        dst[...] = src[...].astype(dst.dtype)


def _side_cast_specs(side_tables, nj, ni):
    specs, shapes = [], []
    for t in side_tables:
        rows = t.shape[0]
        rb = rows // _pow2_tile(nj * ni, rows)
        last = rows // rb - 1
        specs.append(pl.BlockSpec((rb, t.shape[1]), lambda j, i, last=last: (jnp.minimum(j * ni + i, last), 0)))
        shapes.append(jax.ShapeDtypeStruct(t.shape, BF16))
    return specs, shapes


def _matmul(a, w, out_dtype, n_prompt, name, side_tables=()):
    n, k = a.shape
    nc = w.shape[1]
    tm = _pow2_tile(512, n_prompt, n - n_prompt)
    tn = _pow2_tile(1024, nc)
    ni = n // tm
    side_specs, side_shapes = _side_cast_specs(side_tables, nc // tn, ni)
    outs = pl.pallas_call(
        functools.partial(_mm_kernel, n_side=len(side_tables)),
        grid=(nc // tn, ni),
        in_specs=[pl.BlockSpec((tm, k), lambda j, i: (i, 0)),
                  pl.BlockSpec((k, tn), lambda j, i: (0, j))] + side_specs,
        out_specs=[pl.BlockSpec((tm, tn), lambda j, i: (i, j))] + side_specs,
        out_shape=[jax.ShapeDtypeStruct((n, nc), out_dtype)] + side_shapes,
        scratch_shapes=[pltpu.VMEM((k, tn), BF16)],
        compiler_params=_params("arbitrary", "arbitrary"),
        name=name,
    )(a, w, *side_tables)
    return outs[0] if not side_tables else outs


def _hgrn_tables(c):
    nl = int(math.log2(c))
    r = np.arange(c)[:, None]
    j = np.arange(c)[None, :]
    mats = []
    lvl = np.full((c, c), -1, np.int32)
    for l in range(nl):
        m = c >> (l + 1)
        mid = (r // (2 * m)) * 2 * m + m
        upper = r >= mid
        t = np.where(upper, (j >= mid) & (j <= r), (j > r) & (j <= mid - 1))
        mats.append(t)
        same = (r // (2 * m)) == (j // (2 * m))
        lvl = np.where(same & upper & (j < mid), l, lvl)
    lvl = np.where(r == j, nl, lvl)
    mats.append(j <= r)
    mats.append(j > r)
    t_all = np.concatenate(mats, axis=0).astype(np.float32)
    return jnp.asarray(t_all, BF16), jnp.asarray(lvl), nl


def _hgrn_prompt_kernel(q_ref, f_ref, v_ref, og_ref, lb_ref, g_ref, tall_ref, lvl_ref,
                        a_ref, st_ref, s_scr, *, chunk, n_levels, n_chunks, heads_per_step, dk):
    c = chunk
    s_scr[...] = jnp.zeros_like(s_scr)
    lb = lb_ref[...]
    lvl = lvl_ref[...]
    rows = lax.broadcasted_iota(jnp.int32, (c, q_ref.shape[1]), 0)
    nt = (((1,), (1,)), ((), ()))

    def body(ci, carry):
        sl = pl.ds(pl.multiple_of(ci * c, c), c)
        qs = _silu(q_ref[sl, :])
        f = lb + (1.0 - lb) * jax.nn.sigmoid(f_ref[sl, :])
        k = 1.0 - f
        lf = jnp.log(f)
        v = v_ref[sl, :].astype(BF16)
        lf_hi = lf.astype(BF16)
        lf_lo = (lf - lf_hi.astype(F32)).astype(BF16)
        ex = (jnp.dot(tall_ref[...], lf_hi, preferred_element_type=F32)
              + jnp.dot(tall_ref[...], lf_lo, preferred_element_type=F32))
        qs_b = qs.astype(BF16)
        k_b = k.astype(BF16)
        ys = []
        for l in range(n_levels):
            m = c >> (l + 1)
            e_l = jnp.exp(ex[l * c:(l + 1) * c, :])
            ys.append((e_l * jnp.where((rows & m) != 0, qs, k)).astype(BF16))
        bcum = ex[n_levels * c:(n_levels + 1) * c, :]
        brev = ex[(n_levels + 1) * c:(n_levels + 2) * c, :]
        qh = (qs * jnp.exp(bcum)).astype(BF16)
        kh = (k * jnp.exp(brev)).astype(BF16)
        decay = jnp.exp(bcum[c - 1:c, :])
        outs = []
        for hh in range(heads_per_step):
            hs = slice(hh * dk, (hh + 1) * dk)
            scores = jnp.where(lvl == n_levels,
                               lax.dot_general(qs_b[:, hs], k_b[:, hs], nt, preferred_element_type=F32), 0.0)
            for l in range(n_levels):
                y = ys[l][:, hs]
                p = lax.dot_general(y, y, nt, preferred_element_type=F32)
                scores = scores + jnp.where(lvl == l, p, 0.0)
            st = s_scr[hh]
            o = jnp.dot(scores.astype(BF16), v[:, hs], preferred_element_type=F32)
            o = o + lax.dot_general(qh[:, hs], st.astype(BF16), nt, preferred_element_type=F32)
            kv_t = lax.dot_general(v[:, hs], kh[:, hs], (((0,), (0,)), ((), ())), preferred_element_type=F32)
            s_scr[hh] = st * decay[:, hs] + kv_t
            outs.append(o * lax.rsqrt(jnp.mean(o * o, axis=-1, keepdims=True) + EPS))
        on = jnp.concatenate(outs, axis=1) if heads_per_step > 1 else outs[0]
        a_ref[sl, :] = (on * g_ref[...] * _silu(og_ref[sl, :])).astype(a_ref.dtype)
        return carry

    lax.fori_loop(0, n_chunks, body, 0)
    st_ref[0, 0] = s_scr[...]


def _hgrn_prompt(proj, lb, g_out, batch, seq, heads, dk, n_rows):
    c = _pow2_tile(128, seq)
    hp = _pow2_tile(4, heads)
    t_all, lvl, nl = _hgrn_tables(c)
    w = hp * dk
    nh = heads // hp
    col = lambda off: (lambda b, h: (b, off + h))
    kern = functools.partial(_hgrn_prompt_kernel, chunk=c, n_levels=nl, n_chunks=seq // c,
                             heads_per_step=hp, dk=dk)
    return pl.pallas_call(
        kern,
        grid=(batch, nh),
        in_specs=[pl.BlockSpec((seq, w), col(0)),
                  pl.BlockSpec((seq, w), col(nh)),
                  pl.BlockSpec((seq, w), col(2 * nh)),
                  pl.BlockSpec((seq, w), col(3 * nh)),
                  pl.BlockSpec((1, w), lambda b, h: (0, h)),
                  pl.BlockSpec((1, w), lambda b, h: (0, h)),
                  pl.BlockSpec(t_all.shape, lambda b, h: (0, 0)),
                  pl.BlockSpec(lvl.shape, lambda b, h: (0, 0))],
        out_specs=[pl.BlockSpec((seq, w), lambda b, h: (b, h)),
                   pl.BlockSpec((1, 1, hp, dk, dk), lambda b, h: (0, b, h, 0, 0))],
        out_shape=[jax.ShapeDtypeStruct((n_rows, heads * dk), BF16),
                   jax.ShapeDtypeStruct((1, batch, heads, dk, dk), F32)],
        scratch_shapes=[pltpu.VMEM((hp, dk, dk), F32)],
        compiler_params=_params("arbitrary", "arbitrary"),
        name="hgrn_prompt",
    )(proj, proj, proj, proj, lb, g_out, t_all, lvl)


def _hgrn_sample_kernel(q_ref, f_ref, v_ref, og_ref, lb_ref, g_ref, s0_ref, a_ref, s_ref, *, bt, steps):
    lb = lb_ref[...]
    g = g_ref[...]
    rows = lax.broadcasted_iota(jnp.int32, (steps, q_ref.shape[2]), 0)

    def body(b, carry):
        qs = _silu(q_ref[b])
        f = lb + (1.0 - lb) * jax.nn.sigmoid(f_ref[b])
        k = 1.0 - f
        lf = jnp.log(f)
        v = v_ref[b]
        cum = []
        run = None
        for t in range(steps):
            run = lf[t:t + 1, :] if run is None else run + lf[t:t + 1, :]
            cum.append(run)
        bmat = jnp.broadcast_to(cum[0], lf.shape)
        for t in range(1, steps):
            bmat = jnp.where(rows == t, cum[t], bmat)
        o = jnp.zeros_like(qs)
        for s in range(steps):
            x = jnp.where(rows >= s, qs * k[s:s + 1, :] * jnp.exp(jnp.minimum(bmat - cum[s], 0.0)), 0.0)
            o = o + jnp.sum(x, axis=-1, keepdims=True) * v[s:s + 1, :]
        s0 = s0_ref[0, b, 0]
        o = o + jnp.dot((qs * jnp.exp(bmat)).astype(BF16), s0.astype(BF16), preferred_element_type=F32)
        kh = (k * jnp.exp(cum[steps - 1] - bmat)).astype(BF16)
        kv = lax.dot_general(kh, v.astype(BF16), (((0,), (0,)), ((), ())), preferred_element_type=F32)
        decay_col = jnp.broadcast_to(jnp.exp(cum[steps - 1]), s0.shape).T
        s_ref[0, b, 0] = s0 * decay_col + kv
        on = o * lax.rsqrt(jnp.mean(o * o, axis=-1, keepdims=True) + EPS)
        a_ref[b] = on * g * _silu(og_ref[b])
        return carry

    lax.fori_loop(0, bt, body, 0, unroll=8)


def _hgrn_sample(proj_s, lb, g_out, s0, heads, dk):
    db, steps, _ = proj_s.shape
    bt = _pow2_tile(32, db)
    col = lambda off: (lambda j, h: (j, 0, off + h))
    kern = functools.partial(_hgrn_sample_kernel, bt=bt, steps=steps)
    return pl.pallas_call(
        kern,
        grid=(db // bt, heads),
        in_specs=[pl.BlockSpec((bt, steps, dk), col(0)),
                  pl.BlockSpec((bt, steps, dk), col(heads)),
                  pl.BlockSpec((bt, steps, dk), col(2 * heads)),
                  pl.BlockSpec((bt, steps, dk), col(3 * heads)),
                  pl.BlockSpec((1, dk), lambda j, h: (0, h)),
                  pl.BlockSpec((1, dk), lambda j, h: (0, h)),
                  pl.BlockSpec((1, bt, 1, dk, dk), lambda j, h: (0, j, h, 0, 0))],
        out_specs=[pl.BlockSpec((bt, steps, dk), lambda j, h: (j, 0, h)),
                   pl.BlockSpec((1, bt, 1, dk, dk), lambda j, h: (0, j, h, 0, 0))],
        out_shape=[jax.ShapeDtypeStruct((db, steps, heads * dk), F32),
                   jax.ShapeDtypeStruct(s0.shape, F32)],
        compiler_params=_params("arbitrary", "arbitrary"),
        name="hgrn_sample",
    )(proj_s, proj_s, proj_s, proj_s, lb, g_out, s0)


def _gmlp_kernel(gu_ref, gv_ref, w_ref, bias_ref, g_ref, o_ref, v_ref, *, groups, gd):
    u = jax.nn.gelu(gu_ref[...])
    vv = jax.nn.gelu(gv_ref[...])
    r = lax.rsqrt(jnp.mean(vv * vv, axis=-1, keepdims=True) + EPS)
    v = vv * r * g_ref[...]
    v_ref[...] = v
    for gi in range(groups):
        sl = slice(gi * gd, (gi + 1) * gd)
        z = jnp.dot(w_ref[0, gi], v[:, sl].astype(BF16), preferred_element_type=F32) + bias_ref[0, :, sl]
        o_ref[:, sl] = (u[:, sl] * z).astype(o_ref.dtype)


def _gmlp(proj, w_st, bias_st, g_v, n_prompt, gu_blk, gv_blk):
    n = proj.shape[0]
    _, groups, c, _ = w_st.shape
    gw = bias_st.shape[2]
    npc = n_prompt // c
    sel = lambda i: jnp.where(i >= npc, 1, 0)
    kern = functools.partial(_gmlp_kernel, groups=groups, gd=gw // groups)
    return pl.pallas_call(
        kern,
        grid=(n // c,),
        in_specs=[pl.BlockSpec((c, gw), lambda i: (i, gu_blk)),
                  pl.BlockSpec((c, gw), lambda i: (i, gv_blk)),
                  pl.BlockSpec((1, groups, c, c), lambda i: (sel(i), 0, 0, 0)),
                  pl.BlockSpec((1, c, gw), lambda i: (sel(i), 0, 0)),
                  pl.BlockSpec((1, gw), lambda i: (0, 0))],
        out_specs=[pl.BlockSpec((c, gw), lambda i: (i, 0)),
                   pl.BlockSpec((c, gw), lambda i: (jnp.maximum(i - npc, 0), 0))],
        out_shape=[jax.ShapeDtypeStruct((n, gw), BF16),
                   jax.ShapeDtypeStruct((n - n_prompt, gw), F32)],
        compiler_params=_params("arbitrary"),
        name="chunk_mlp",
    )(proj, proj, w_st, bias_st, g_v.reshape(1, gw))


def _branch_kernel(a_ref, b_ref, wa_ref, wb_ref, ga_ref, gb_ref, o_ref):
    ya = jnp.dot(a_ref[...], wa_ref[...], preferred_element_type=F32)
    yb = jnp.dot(b_ref[...], wb_ref[...], preferred_element_type=F32)
    o_ref[...] = (jax.nn.sigmoid(ga_ref[...]) * ya + jax.nn.sigmoid(gb_ref[...]) * yb).astype(o_ref.dtype)


def _branches(a, bm, wa, wb, proj, ga_off, gb_off, n_prompt):
    n, ka = a.shape
    kb = bm.shape[1]
    d = wa.shape[1]
    tm = _pow2_tile(512, n_prompt, n - n_prompt)
    tn = _pow2_tile(1024, d, ga_off, gb_off)
    return pl.pallas_call(
        _branch_kernel,
        grid=(d // tn, n // tm),
        in_specs=[pl.BlockSpec((tm, ka), lambda j, i: (i, 0)),
                  pl.BlockSpec((tm, kb), lambda j, i: (i, 0)),
                  pl.BlockSpec((ka, tn), lambda j, i: (0, j)),
                  pl.BlockSpec((kb, tn), lambda j, i: (0, j)),
                  pl.BlockSpec((tm, tn), lambda j, i: (i, ga_off // tn + j)),
                  pl.BlockSpec((tm, tn), lambda j, i: (i, gb_off // tn + j))],
        out_specs=pl.BlockSpec((tm, tn), lambda j, i: (i, j)),
        out_shape=jax.ShapeDtypeStruct((n, d), BF16),
        compiler_params=_params("arbitrary", "arbitrary"),
        name="branches",
    )(a, bm, wa, wb, proj, proj)


def _out_kernel(m_ref, w_ref, xp_ref, xs_ref, gp_ref, gs_ref, side_ref, o_ref, side_out_ref, *, n_prompt_tiles):
    i = pl.program_id(1)
    y = jnp.dot(m_ref[...], w_ref[...], preferred_element_type=F32)
    side_out_ref[...] = side_ref[...].astype(side_out_ref.dtype)

    @pl.when(i < n_prompt_tiles)
    def _():
        o_ref[...] = xp_ref[...] + gp_ref[0] * y

    @pl.when(i >= n_prompt_tiles)
    def _():
        o_ref[...] = xs_ref[...] + gs_ref[...] * y


def _out_proj(mix, w, x_p, x_s, mod_p, mod_s, k_gate, n_prompt, seq, side_table):
    n, d = mix.shape
    tm = _pow2_tile(512, seq, n - n_prompt)
    tn = _pow2_tile(1024, d)
    npt = n_prompt // tm
    per_b = seq // tm
    nj = d // tn
    side_specs, side_shapes = _side_cast_specs((side_table,), nj, n // tm)
    return pl.pallas_call(
        functools.partial(_out_kernel, n_prompt_tiles=npt),
        grid=(nj, n // tm),
        in_specs=[pl.BlockSpec((tm, d), lambda j, i: (i, 0)),
                  pl.BlockSpec((d, tn), lambda j, i: (0, j)),
                  pl.BlockSpec((tm, tn), lambda j, i: (jnp.minimum(i, npt - 1), j)),
                  pl.BlockSpec((tm, tn), lambda j, i: (jnp.maximum(i - npt, 0), j)),
                  pl.BlockSpec((1, 1, tn), lambda j, i: (jnp.minimum(i, npt - 1) // per_b, 0, k_gate * nj + j)),
                  pl.BlockSpec((tm, tn), lambda j, i: (jnp.maximum(i - npt, 0), k_gate * nj + j))] + side_specs,
        out_specs=[pl.BlockSpec((tm, tn), lambda j, i: (i, j))] + side_specs,
        out_shape=[jax.ShapeDtypeStruct((n, d), F32)] + side_shapes,
        compiler_params=_params("arbitrary", "arbitrary"),
        name="out_proj",
    )(mix, w, x_p, x_s, mod_p, mod_s, side_table)


def _cand_pairs(k):
    return [(i, j) for i in range(k) for j in range(k) if (i + 1) * (j + 1) <= k]


def _arg_rounds(s, iota, k, exact):
    rank = jnp.full(s.shape, float(k), F32)
    vals = []
    big = float(s.shape[0])
    for i in range(k):
        m = jnp.max(s, axis=0, keepdims=True)
        if exact:
            idx = jnp.min(jnp.where(s == m, iota, big), axis=0, keepdims=True)
            sel = iota == idx
        else:
            sel = s == m
        rank = jnp.where(sel, float(i), rank)
        s = jnp.where(sel, NEG_INF, s)
        vals.append(m)
    return vals, rank


def _peer_gates(s1, s2, jsum, cand_scr, cols, pairs, k, exact):
    n_iota = lax.broadcasted_iota(jnp.int32, s1.shape, 0).astype(F32)
    a, r1 = _arg_rounds(s1, n_iota, k, exact)
    b, r2 = _arg_rounds(s2, n_iota, k, exact)

    n_cand = cand_scr.shape[0]
    cand_scr[len(pairs):, cols] = jnp.full((n_cand - len(pairs), s1.shape[1]), NEG_INF, F32)
    for p, (i, j) in enumerate(pairs):
        cand_scr[p:p + 1, cols] = a[i] + b[j]
    c0 = cand_scr[:, cols]
    p_iota = lax.broadcasted_iota(jnp.int32, c0.shape, 0).astype(F32)
    _, prank = _arg_rounds(c0, p_iota, k, exact)
    taken = prank < float(k)
    w = jnp.where(taken, jnp.exp(c0 - (a[0] + b[0])), 0.0)
    inv_z = 1.0 / jnp.sum(w, axis=0, keepdims=True)
    ones = jnp.where(taken, 1.0, 0.0)
    jcnt = jnp.dot(jsum, ones.astype(BF16), preferred_element_type=F32)

    eaz = jnp.zeros(s1.shape, F32)
    jd = jnp.zeros(s1.shape, F32)
    eb = jnp.zeros(s1.shape, F32)
    for i in range(k):
        hit1 = r1 == float(i)
        eaz = jnp.where(hit1, jnp.exp(a[i] - a[0]) * inv_z, eaz)
        jd = jnp.where(hit1, jcnt[i:i + 1, :], jd)
        eb = jnp.where(r2 == float(i), jnp.exp(b[i] - b[0]), eb)

    def excess(rank):
        return jnp.abs(jnp.sum(jnp.where(rank < float(k), 1.0, 0.0), axis=0, keepdims=True) - float(k))

    return eaz, jd, eb, r2, excess(r1) + excess(r2) + excess(prank)


def _peer_topk_kernel(q_ref, sk_ref, jsum_ref, eaz_ref, jd_ref, eb_ref, r2_ref, cand_scr, *, half, pairs, k):
    nt = (((1,), (1,)), ((), ()))
    q = q_ref[...].astype(BF16)
    s1 = lax.dot_general(sk_ref[0, 0], q[:, :half], nt, preferred_element_type=F32)
    s2 = lax.dot_general(sk_ref[0, 1], q[:, half:], nt, preferred_element_type=F32)
    jsum = jsum_ref[...]
    n_blocks = s1.shape[1] // LANE_V7X

    def run(exact):
        bad = None
        for cb in range(n_blocks):
            cols = slice(cb * LANE_V7X, (cb + 1) * LANE_V7X)
            eaz, jd, eb, r2, dev = _peer_gates(s1[:, cols], s2[:, cols], jsum, cand_scr, cols, pairs, k, exact)
            eaz_ref[0, :, cols] = eaz
            jd_ref[0, :, cols] = jd
            eb_ref[0, :, cols] = eb.astype(eb_ref.dtype)
            r2_ref[0, :, cols] = r2.astype(r2_ref.dtype)
            bad = dev if bad is None else bad + dev
        return jnp.max(bad)

    miscount = run(exact=False)

    @pl.when(miscount > 0.5)
    def _():
        run(exact=True)


def _peer_topk(qp, sub_keys_b, n_prompt):
    n = qp.shape[0]
    heads, _, n_keys, half = sub_keys_b.shape
    tt = _pow2_tile(256, n_prompt, n - n_prompt)
    pairs = _cand_pairs(PEER_TOPK)
    n_cand = -(-len(pairs) // 64) * 64
    jsum = np.zeros((PEER_TOPK, n_cand), np.float32)
    for p, (i, _) in enumerate(pairs):
        jsum[i, p] = 1.0
    kern = functools.partial(_peer_topk_kernel, half=half, pairs=pairs, k=PEER_TOPK)
    maps = jax.ShapeDtypeStruct((heads, n_keys, n), F32)
    maps_b = jax.ShapeDtypeStruct((heads, n_keys, n), BF16)
    mspec = pl.BlockSpec((1, n_keys, tt), lambda i, h: (h, 0, i))
    return pl.pallas_call(
        kern,
        grid=(n // tt, heads),
        in_specs=[pl.BlockSpec((tt, 2 * half), lambda i, h: (i, h)),
                  pl.BlockSpec((1, 2, n_keys, half), lambda i, h: (h, 0, 0, 0)),
                  pl.BlockSpec(jsum.shape, lambda i, h: (0, 0))],
        out_specs=[mspec, mspec, mspec, mspec],
        out_shape=[maps, maps, maps_b, maps_b],
        scratch_shapes=[pltpu.VMEM((n_cand, tt), F32)],
        compiler_params=_params("arbitrary", "arbitrary"),
        name="peer_topk",
    )(qp, sub_keys_b, jnp.asarray(jsum, BF16))


def _peer_main_kernel(h_ref, u_ref, v_ref, eaz_ref, jd_ref, eb_ref, r2_ref, o_ref,
                      ga0_scr, ga1_scr, g0_scr, g1_scr, *, heads, n_keys, rows_per_step, n_steps):
    e = pl.program_id(1)
    blocks = [(slice(r * n_keys, (r + 1) * n_keys), slice(tb * LANE_V7X, (tb + 1) * LANE_V7X))
              for r in range(rows_per_step) for tb in range(h_ref.shape[0] // LANE_V7X)]
    n_slices = math.gcd(len(blocks), 16)

    def gates(chunk, g_scr, part=None):
        for sl, ts in (blocks if part is None else blocks[part::n_slices]):
            n1 = chunk * rows_per_step + sl.start // n_keys
            g = None
            for h in range(heads):
                ea = eaz_ref[h, pl.ds(n1, 1), :][:, ts].astype(BF16)
                jd = jd_ref[h, pl.ds(n1, 1), :][:, ts].astype(BF16)
                t = ea * jnp.where(r2_ref[h, :, ts] < jd, eb_ref[h, :, ts], jnp.zeros((), BF16))
                g = t if g is None else g + t
            g_scr[sl, ts] = g

    @pl.when(e == 0)
    def _():
        o_ref[...] = jnp.zeros_like(o_ref)
        ga1_scr[...] = jnp.zeros_like(ga1_scr)
        gates(0, g0_scr)

    def stage(ga_cur, ga_prev, g_cur, g_next):
        act_t = lax.dot_general(u_ref[...], h_ref[...], (((1,), (1,)), ((), ())),
                                preferred_element_type=F32)
        dq = o_ref.shape[1] // n_slices
        for q in range(n_slices):
            qs = slice(q * dq, (q + 1) * dq)
            o_ref[:, qs] += jnp.dot(ga_prev[...], v_ref[:, qs], preferred_element_type=F32)
            for sl, ts in blocks[q::n_slices]:
                ga_cur[ts, sl] = (g_cur[sl, ts] * jax.nn.gelu(act_t[sl, ts].astype(BF16))).T
            gates(jnp.minimum(e + 1, n_steps - 1), g_next, part=q)

    @pl.when(e % 2 == 0)
    def _():
        stage(ga0_scr, ga1_scr, g0_scr, g1_scr)

    @pl.when(e % 2 == 1)
    def _():
        stage(ga1_scr, ga0_scr, g1_scr, g0_scr)


def _peer_main(h2, u_b, v_b, maps, n_prompt):
    n, d = h2.shape
    n_exp = u_b.shape[0]
    heads, n_keys, _ = maps[0].shape
    tt = _pow2_tile(512, n_prompt, n - n_prompt)
    ec = _pow2_tile(512, n_exp)
    rps = ec // n_keys
    n_steps = n_exp // ec
    kern = functools.partial(_peer_main_kernel, heads=heads, n_keys=n_keys, rows_per_step=rps, n_steps=n_steps)
    once = pl.Buffered(1)
    mspec = pl.BlockSpec((heads, n_keys, tt), lambda i, e: (0, 0, i), pipeline_mode=once)
    return pl.pallas_call(
        kern,
        grid=(n // tt, n_steps + 1),
        in_specs=[pl.BlockSpec((tt, d), lambda i, e: (i, 0), pipeline_mode=once),
                  pl.BlockSpec((ec, d), lambda i, e: (jnp.minimum(e, n_steps - 1), 0)),
                  pl.BlockSpec((ec, d), lambda i, e: (jnp.maximum(e - 1, 0), 0)),
                  mspec, mspec, mspec, mspec],
        out_specs=pl.BlockSpec((tt, d), lambda i, e: (i, 0), pipeline_mode=once),
        out_shape=jax.ShapeDtypeStruct((n, d), F32),
        scratch_shapes=[pltpu.VMEM((tt, ec), BF16), pltpu.VMEM((tt, ec), BF16),
                        pltpu.VMEM((ec, tt), BF16), pltpu.VMEM((ec, tt), BF16)],
        compiler_params=_params("arbitrary", "arbitrary"),
        name="peer_main",
    )(h2, u_b, v_b, *maps)


def _final_kernel(x_ref, p_ref, gp_ref, gs_ref, g_ref, yp_ref, ys_ref, *, n_prompt_tiles):
    i = pl.program_id(0)

    def norm(x):
        return x * lax.rsqrt(jnp.mean(x * x, axis=-1, keepdims=True) + EPS) * g_ref[...]

    @pl.when(i < n_prompt_tiles)
    def _():
        yp_ref[...] = norm(x_ref[...] + gp_ref[0] * p_ref[...])

    @pl.when(i >= n_prompt_tiles)
    def _():
        ys_ref[...] = norm(x_ref[...] + gs_ref[...] * p_ref[...])


def _final(x1, peer_out, mod_p, mod_s, k_gate, g_final, n_prompt, seq):
    n, d = x1.shape
    tm = _pow2_tile(256, seq, n - n_prompt)
    npt = n_prompt // tm
    per_b = seq // tm
    return pl.pallas_call(
        functools.partial(_final_kernel, n_prompt_tiles=npt),
        grid=(n // tm,),
        in_specs=[pl.BlockSpec((tm, d), lambda i: (i, 0)),
                  pl.BlockSpec((tm, d), lambda i: (i, 0)),
                  pl.BlockSpec((1, 1, d), lambda i: (jnp.minimum(i, npt - 1) // per_b, 0, k_gate)),
                  pl.BlockSpec((tm, d), lambda i: (jnp.maximum(i - npt, 0), k_gate)),
                  pl.BlockSpec((1, d), lambda i: (0, 0))],
        out_specs=[pl.BlockSpec((tm, d), lambda i: (jnp.minimum(i, npt - 1), 0)),
                   pl.BlockSpec((tm, d), lambda i: (jnp.maximum(i - npt, 0), 0))],
        out_shape=[jax.ShapeDtypeStruct((n_prompt, d), F32),
                   jax.ShapeDtypeStruct((n - n_prompt, d), F32)],
        compiler_params=_params("arbitrary"),
        name="final_norm",
    )(x1, peer_out, mod_p, mod_s, g_final.reshape(1, d))


def kernel(x_prompt, x_sample, c_prompt, c_sample, state_hgrn, w_ada, b_ada, g_norm1, w_in, hgrn_lb_logits,
           g_hgrn_out, g_gmlp_v, w_spatial, b_spatial, w_branch_a, w_branch_b, w_out, g_norm2, w_peer_q,
           peer_sub_keys, peer_u, peer_v, g_final):
    batch, seq, d = x_prompt.shape
    db, dt, _ = x_sample.shape
    depth, _, heads, dk, dv = state_hgrn.shape
    assert depth == 1 and dk == dv == LANE_V7X
    hw = heads * dk
    groups, gc = w_spatial.shape[1], w_spatial.shape[2]
    gw = w_branch_b.shape[1]
    assert gc == LANE_V7X and gw // groups == LANE_V7X and dt <= gc and gc % dt == 0
    n_prompt, n_sample = batch * seq, db * dt
    n = n_prompt + n_sample

    c_all = jnp.concatenate([c_prompt, c_sample], axis=0)
    pad = (-c_all.shape[0]) % 8
    c_all = jnp.pad(c_all, ((0, pad), (0, 0)))
    mod = _ada(c_all, w_ada[0], b_ada[0])
    mod_p = mod[:batch].reshape(batch, 1, N_MOD * d)
    mod_s = jnp.repeat(mod[batch:batch + db], dt, axis=0)

    x_p = x_prompt.reshape(n_prompt, d)
    x_s = x_sample.reshape(n_sample, d)

    h1 = _norm_mod(x_p, x_s, g_norm1[0], mod_p, mod_s, 1, 0, n_prompt, n_sample, seq)
    proj, peer_u_b = _matmul(h1, w_in[0], F32, n_prompt, "in_proj", side_tables=(peer_u[0],))

    lb = jnp.cumsum(jax.nn.softmax(hgrn_lb_logits.astype(F32), axis=0), axis=0)[0].reshape(1, hw)
    g_ho = g_hgrn_out[0].reshape(1, hw)
    a_p, st_p = _hgrn_prompt(proj, lb, g_ho, batch, seq, heads, dk, n_prompt)
    proj_s = proj[n_prompt:].reshape(db, dt, proj.shape[1])
    a_s, st_s = _hgrn_sample(proj_s, lb, g_ho, state_hgrn, heads, dk)
    a_all = jnp.concatenate([a_p, a_s.reshape(n_sample, hw).astype(BF16)], axis=0)

    tril = jnp.tril(jnp.ones((gc, gc), F32))
    w_sp = w_spatial[0]
    blk = jnp.arange(gc) // dt
    w_samp = jnp.tile(w_sp[:, :dt, :dt], (1, gc // dt, gc // dt)) * (blk[:, None] == blk[None, :])
    w_st = jnp.stack([w_sp * tril, w_samp * tril]).astype(BF16)
    bias_full = jnp.repeat(b_spatial[0].T, gw // groups, axis=1)
    bias_st = jnp.stack([bias_full, jnp.tile(bias_full[:dt], (gc // dt, 1))])
    gu_blk = 4 * hw // gw
    assert gu_blk * gw == 4 * hw
    bm, v_s = _gmlp(proj, w_st, bias_st, g_gmlp_v[0], n_prompt, gu_blk, gu_blk + 1)

    ga_off = 4 * hw + 2 * gw
    mix = _branches(a_all, bm, w_branch_a[0].astype(BF16), w_branch_b[0].astype(BF16), proj,
                    ga_off, ga_off + d, n_prompt)
    x1, peer_v_b = _out_proj(mix, w_out[0].astype(BF16), x_p, x_s, mod_p, mod_s, 2, n_prompt, seq, peer_v[0])

    h2 = _norm_mod(x1, x1, g_norm2[0], mod_p, mod_s, 4, 3, n_prompt, n_sample, seq)
    qp = _matmul(h2, w_peer_q[0], F32, n_prompt, "peer_query")
    maps = _peer_topk(qp, peer_sub_keys[0].astype(BF16), n_prompt)
    peer_out = _peer_main(h2, peer_u_b, peer_v_b, maps, n_prompt)
    y_p, y_s = _final(x1, peer_out, mod_p, mod_s, 5, g_final, n_prompt, seq)

    state_p = jnp.swapaxes(st_p, -1, -2)
    return (y_p.reshape(batch, seq, d), y_s.reshape(db, dt, d), state_p, st_s,
            v_s.reshape(1, db, dt, gw))
```

```python
import functools
import math

import numpy as np
import jax
import jax.numpy as jnp
from jax import lax
from jax.experimental import pallas as pl
from jax.experimental.pallas import tpu as pltpu

EPS = 1e-6
N_MOD = 6
PEER_TOPK = 16
LANE_V7X = 128
VMEM_LIMIT_V7X = 60 * 1024 * 1024
F32 = jnp.float32
BF16 = jnp.bfloat16
NEG_INF = float("-inf")


def _params(*sem):
    return pltpu.CompilerParams(dimension_semantics=sem, vmem_limit_bytes=VMEM_LIMIT_V7X)


def _pow2_tile(target, *sizes):
    g = 0
    for s in sizes:
        g = math.gcd(g, s)
    t = 1
    while t * 2 <= target and g % (t * 2) == 0:
        t *= 2
    return t


def _silu(x):
    return x * jax.nn.sigmoid(x)


def _ada_kernel(c_ref, w_ref, b_ref, o_ref):
    s = _silu(c_ref[...]).astype(BF16)
    o_ref[...] = jnp.dot(s, w_ref[...].astype(BF16), preferred_element_type=F32) + b_ref[...]


def _ada(c_all, w_ada, b_ada):
    m, d = c_all.shape
    n = w_ada.shape[1]
    tn = _pow2_tile(512, n)
    return pl.pallas_call(
        _ada_kernel,
        grid=(n // tn,),
        in_specs=[pl.BlockSpec((m, d), lambda j: (0, 0)),
                  pl.BlockSpec((d, tn), lambda j: (0, j)),
                  pl.BlockSpec((1, tn), lambda j: (0, j))],
        out_specs=pl.BlockSpec((m, tn), lambda j: (0, j)),
        out_shape=jax.ShapeDtypeStruct((m, n), F32),
        compiler_params=_params("arbitrary"),
        name="ada_mod",
    )(c_all, w_ada, b_ada.reshape(1, n))


def _norm_mod_kernel(xp_ref, xs_ref, g_ref, scp_ref, shp_ref, scs_ref, shs_ref, o_ref, *, n_prompt_tiles):
    i = pl.program_id(0)

    def normed(x):
        return x * lax.rsqrt(jnp.mean(x * x, axis=-1, keepdims=True) + EPS) * g_ref[...]

    @pl.when(i < n_prompt_tiles)
    def _():
        o_ref[...] = (normed(xp_ref[...]) * (1.0 + scp_ref[0]) + shp_ref[0]).astype(o_ref.dtype)

    @pl.when(i >= n_prompt_tiles)
    def _():
        o_ref[...] = (normed(xs_ref[...]) * (1.0 + scs_ref[...]) + shs_ref[...]).astype(o_ref.dtype)


def _row_sources(x_p, x_s, n_prompt, tm):
    npt = n_prompt // tm
    s_off = npt if x_s is x_p else 0
    return (lambda i: (jnp.minimum(i, npt - 1), 0)), (lambda i: (s_off + jnp.maximum(i - npt, 0), 0))


def _norm_mod(x_p, x_s, g, mod_p, mod_s, k_scale, k_shift, n_prompt, n_sample, seq):
    d = x_p.shape[1]
    n = n_prompt + n_sample
    tm = _pow2_tile(256, seq, n_sample)
    npt = n_prompt // tm
    per_b = seq // tm
    p_map, s_map = _row_sources(x_p, x_s, n_prompt, tm)

    def pidx(k):
        return lambda i: (jnp.minimum(i, npt - 1) // per_b, 0, k)

    def sidx(k):
        return lambda i: (jnp.maximum(i - npt, 0), k)

    return pl.pallas_call(
        functools.partial(_norm_mod_kernel, n_prompt_tiles=npt),
        grid=(n // tm,),
        in_specs=[pl.BlockSpec((tm, d), p_map),
                  pl.BlockSpec((tm, d), s_map),
                  pl.BlockSpec((1, d), lambda i: (0, 0)),
                  pl.BlockSpec((1, 1, d), pidx(k_scale)),
                  pl.BlockSpec((1, 1, d), pidx(k_shift)),
                  pl.BlockSpec((tm, d), sidx(k_scale)),
                  pl.BlockSpec((tm, d), sidx(k_shift))],
        out_specs=pl.BlockSpec((tm, d), lambda i: (i, 0)),
        out_shape=jax.ShapeDtypeStruct((n, d), BF16),
        compiler_params=_params("arbitrary"),
        name="norm_mod",
    )(x_p, x_s, g.reshape(1, d), mod_p, mod_p, mod_s, mod_s)


def _mm_kernel(a_ref, w_ref, *rest, n_side):
    side_in, o_ref, side_out, wb_scr = rest[:n_side], rest[n_side], rest[n_side + 1:2 * n_side + 1], rest[-1]

    @pl.when(pl.program_id(1) == 0)
    def _():
        wb_scr[...] = w_ref[...].astype(BF16)

    o_ref[...] = jnp.dot(a_ref[...], wb_scr[...], preferred_element_type=F32).astype(o_ref.dtype)
    for src, dst in zip(side_in, side_out):
        dst[...] = src[...].astype(dst.dtype)


def _side_cast_specs(side_tables, nj, ni):
    specs, shapes = [], []
    for t in side_tables:
        rows = t.shape[0]
        rb = rows // _pow2_tile(nj * ni, rows)
        last = rows // rb - 1
        specs.append(pl.BlockSpec((rb, t.shape[1]), lambda j, i, last=last: (jnp.minimum(j * ni + i, last), 0)))
        shapes.append(jax.ShapeDtypeStruct(t.shape, BF16))
    return specs, shapes


def _matmul(a, w, out_dtype, n_prompt, name, side_tables=()):
    n, k = a.shape
    nc = w.shape[1]
    tm = _pow2_tile(512, n_prompt, n - n_prompt)
    tn = _pow2_tile(1024, nc)
    ni = n // tm
    side_specs, side_shapes = _side_cast_specs(side_tables, nc // tn, ni)
    outs = pl.pallas_call(
        functools.partial(_mm_kernel, n_side=len(side_tables)),
        grid=(nc // tn, ni),
        in_specs=[pl.BlockSpec((tm, k), lambda j, i: (i, 0)),
                  pl.BlockSpec((k, tn), lambda j, i: (0, j))] + side_specs,
        out_specs=[pl.BlockSpec((tm, tn), lambda j, i: (i, j))] + side_specs,
        out_shape=[jax.ShapeDtypeStruct((n, nc), out_dtype)] + side_shapes,
        scratch_shapes=[pltpu.VMEM((k, tn), BF16)],
        compiler_params=_params("arbitrary", "arbitrary"),
        name=name,
    )(a, w, *side_tables)
    return outs[0] if not side_tables else outs


def _hgrn_tables(c):
    nl = int(math.log2(c))
    r = np.arange(c)[:, None]
    j = np.arange(c)[None, :]
    mats = []
    lvl = np.full((c, c), -1, np.int32)
    for l in range(nl):
        m = c >> (l + 1)
        mid = (r // (2 * m)) * 2 * m + m
        upper = r >= mid
        t = np.where(upper, (j >= mid) & (j <= r), (j > r) & (j <= mid - 1))
        mats.append(t)
        same = (r // (2 * m)) == (j // (2 * m))
        lvl = np.where(same & upper & (j < mid), l, lvl)
    lvl = np.where(r == j, nl, lvl)
    mats.append(j <= r)
    mats.append(j > r)
    t_all = np.concatenate(mats, axis=0).astype(np.float32)
    t_all = np.concatenate([t_all, t_all], axis=1)
    return jnp.asarray(t_all, BF16), jnp.asarray(lvl), nl


def _hgrn_prompt_kernel(q_ref, f_ref, v_ref, og_ref, lb_ref, g_ref, tall_ref, lvl_ref,
                        a_ref, st_ref, s_scr, *, chunk, n_levels, n_chunks, heads_per_step, dk):
    c = chunk
    s_scr[...] = jnp.zeros_like(s_scr)
    lb = lb_ref[...]
    lvl = lvl_ref[...]
    rows = lax.broadcasted_iota(jnp.int32, (c, q_ref.shape[1]), 0)
    nt = (((1,), (1,)), ((), ()))

    def body(ci, carry):
        sl = pl.ds(pl.multiple_of(ci * c, c), c)
        qs = _silu(q_ref[sl, :])
        f = lb + (1.0 - lb) * jax.nn.sigmoid(f_ref[sl, :])
        k = 1.0 - f
        lf = jnp.log(f)
        v = v_ref[sl, :].astype(BF16)
        lf_hi = lf.astype(BF16)
        lf_lo = (lf - lf_hi.astype(F32)).astype(BF16)
        ex = jnp.dot(tall_ref[...], jnp.concatenate([lf_hi, lf_lo], axis=0), preferred_element_type=F32)
        qs_b = qs.astype(BF16)
        k_b = k.astype(BF16)
        ys = []
        for l in range(n_levels):
            m = c >> (l + 1)
            e_l = jnp.exp(ex[l * c:(l + 1) * c, :])
            ys.append((e_l * jnp.where((rows & m) != 0, qs, k)).astype(BF16))
        bcum = ex[n_levels * c:(n_levels + 1) * c, :]
        brev = ex[(n_levels + 1) * c:(n_levels + 2) * c, :]
        qh = (qs * jnp.exp(bcum)).astype(BF16)
        kh = (k * jnp.exp(brev)).astype(BF16)
        decay = jnp.exp(bcum[c - 1:c, :])
        outs = []
        for hh in range(heads_per_step):
            hs = slice(hh * dk, (hh + 1) * dk)
            scores = jnp.where(lvl == n_levels,
                               lax.dot_general(qs_b[:, hs], k_b[:, hs], nt, preferred_element_type=F32), 0.0)
            for l in range(n_levels):
                y = ys[l][:, hs]
                p = lax.dot_general(y, y, nt, preferred_element_type=F32)
                scores = scores + jnp.where(lvl == l, p, 0.0)
            st = s_scr[hh]
            o = jnp.dot(scores.astype(BF16), v[:, hs], preferred_element_type=F32)
            o = o + lax.dot_general(qh[:, hs], st.astype(BF16), nt, preferred_element_type=F32)
            kv_t = lax.dot_general(v[:, hs], kh[:, hs], (((0,), (0,)), ((), ())), preferred_element_type=F32)
            s_scr[hh] = st * decay[:, hs] + kv_t
            outs.append(o * lax.rsqrt(jnp.mean(o * o, axis=-1, keepdims=True) + EPS))
        on = jnp.concatenate(outs, axis=1) if heads_per_step > 1 else outs[0]
        a_ref[sl, :] = (on * g_ref[...] * _silu(og_ref[sl, :])).astype(a_ref.dtype)
        return carry

    lax.fori_loop(0, n_chunks, body, 0, unroll=2)
    st_ref[0, 0] = s_scr[...]


def _hgrn_prompt(proj, lb, g_out, batch, seq, heads, dk, n_rows):
    c = _pow2_tile(128, seq)
    hp = _pow2_tile(4, heads)
    t_all, lvl, nl = _hgrn_tables(c)
    w = hp * dk
    nh = heads // hp
    col = lambda off: (lambda b, h: (b, off + h))
    kern = functools.partial(_hgrn_prompt_kernel, chunk=c, n_levels=nl, n_chunks=seq // c,
                             heads_per_step=hp, dk=dk)
    return pl.pallas_call(
        kern,
        grid=(batch, nh),
        in_specs=[pl.BlockSpec((seq, w), col(0)),
                  pl.BlockSpec((seq, w), col(nh)),
                  pl.BlockSpec((seq, w), col(2 * nh)),
                  pl.BlockSpec((seq, w), col(3 * nh)),
                  pl.BlockSpec((1, w), lambda b, h: (0, h)),
                  pl.BlockSpec((1, w), lambda b, h: (0, h)),
                  pl.BlockSpec(t_all.shape, lambda b, h: (0, 0)),
                  pl.BlockSpec(lvl.shape, lambda b, h: (0, 0))],
        out_specs=[pl.BlockSpec((seq, w), lambda b, h: (b, h)),
                   pl.BlockSpec((1, 1, hp, dk, dk), lambda b, h: (0, b, h, 0, 0))],
        out_shape=[jax.ShapeDtypeStruct((n_rows, heads * dk), BF16),
                   jax.ShapeDtypeStruct((1, batch, heads, dk, dk), F32)],
        scratch_shapes=[pltpu.VMEM((hp, dk, dk), F32)],
        compiler_params=_params("arbitrary", "arbitrary"),
        name="hgrn_prompt",
    )(proj, proj, proj, proj, lb, g_out, t_all, lvl)


def _hgrn_sample_kernel(q_ref, f_ref, v_ref, og_ref, lb_ref, g_ref, s0_ref, a_ref, s_ref, *, bt, steps):
    lb = lb_ref[...]
    g = g_ref[...]
    rows = lax.broadcasted_iota(jnp.int32, (steps, q_ref.shape[2]), 0)

    def body(b, carry):
        qs = _silu(q_ref[b])
        f = lb + (1.0 - lb) * jax.nn.sigmoid(f_ref[b])
        k = 1.0 - f
        lf = jnp.log(f)
        v = v_ref[b]
        cum = []
        run = None
        for t in range(steps):
            run = lf[t:t + 1, :] if run is None else run + lf[t:t + 1, :]
            cum.append(run)
        bmat = jnp.broadcast_to(cum[0], lf.shape)
        for t in range(1, steps):
            bmat = jnp.where(rows == t, cum[t], bmat)
        o = jnp.zeros_like(qs)
        for s in range(steps):
            x = jnp.where(rows >= s, qs * k[s:s + 1, :] * jnp.exp(jnp.minimum(bmat - cum[s], 0.0)), 0.0)
            o = o + jnp.sum(x, axis=-1, keepdims=True) * v[s:s + 1, :]
        s0 = s0_ref[0, b, 0]
        o = o + jnp.dot((qs * jnp.exp(bmat)).astype(BF16), s0.astype(BF16), preferred_element_type=F32)
        kh = (k * jnp.exp(cum[steps - 1] - bmat)).astype(BF16)
        kv = lax.dot_general(kh, v.astype(BF16), (((0,), (0,)), ((), ())), preferred_element_type=F32)
        decay_col = jnp.broadcast_to(jnp.exp(cum[steps - 1]), s0.shape).T
        s_ref[0, b, 0] = s0 * decay_col + kv
        on = o * lax.rsqrt(jnp.mean(o * o, axis=-1, keepdims=True) + EPS)
        a_ref[b] = on * g * _silu(og_ref[b])
        return carry

    lax.fori_loop(0, bt, body, 0, unroll=8)


def _hgrn_sample(proj_s, lb, g_out, s0, heads, dk):
    db, steps, _ = proj_s.shape
    bt = _pow2_tile(32, db)
    col = lambda off: (lambda j, h: (j, 0, off + h))
    kern = functools.partial(_hgrn_sample_kernel, bt=bt, steps=steps)
    return pl.pallas_call(
        kern,
        grid=(db // bt, heads),
        in_specs=[pl.BlockSpec((bt, steps, dk), col(0)),
                  pl.BlockSpec((bt, steps, dk), col(heads)),
                  pl.BlockSpec((bt, steps, dk), col(2 * heads)),
                  pl.BlockSpec((bt, steps, dk), col(3 * heads)),
                  pl.BlockSpec((1, dk), lambda j, h: (0, h)),
                  pl.BlockSpec((1, dk), lambda j, h: (0, h)),
                  pl.BlockSpec((1, bt, 1, dk, dk), lambda j, h: (0, j, h, 0, 0))],
        out_specs=[pl.BlockSpec((bt, steps, dk), lambda j, h: (j, 0, h)),
                   pl.BlockSpec((1, bt, 1, dk, dk), lambda j, h: (0, j, h, 0, 0))],
        out_shape=[jax.ShapeDtypeStruct((db, steps, heads * dk), F32),
                   jax.ShapeDtypeStruct(s0.shape, F32)],
        compiler_params=_params("arbitrary", "arbitrary"),
        name="hgrn_sample",
    )(proj_s, proj_s, proj_s, proj_s, lb, g_out, s0)


def _gmlp_kernel(gu_ref, gv_ref, w_ref, bias_ref, g_ref, o_ref, v_ref, *, groups, gd):
    u = jax.nn.gelu(gu_ref[...])
    vv = jax.nn.gelu(gv_ref[...])
    r = lax.rsqrt(jnp.mean(vv * vv, axis=-1, keepdims=True) + EPS)
    v = vv * r * g_ref[...]
    v_ref[...] = v
    for gi in range(groups):
        sl = slice(gi * gd, (gi + 1) * gd)
        z = jnp.dot(w_ref[0, gi], v[:, sl].astype(BF16), preferred_element_type=F32) + bias_ref[0, :, sl]
        o_ref[:, sl] = (u[:, sl] * z).astype(o_ref.dtype)


def _gmlp(proj, w_st, bias_st, g_v, n_prompt, gu_blk, gv_blk):
    n = proj.shape[0]
    _, groups, c, _ = w_st.shape
    gw = bias_st.shape[2]
    npc = n_prompt // c
    sel = lambda i: jnp.where(i >= npc, 1, 0)
    kern = functools.partial(_gmlp_kernel, groups=groups, gd=gw // groups)
    return pl.pallas_call(
        kern,
        grid=(n // c,),
        in_specs=[pl.BlockSpec((c, gw), lambda i: (i, gu_blk)),
                  pl.BlockSpec((c, gw), lambda i: (i, gv_blk)),
                  pl.BlockSpec((1, groups, c, c), lambda i: (sel(i), 0, 0, 0)),
                  pl.BlockSpec((1, c, gw), lambda i: (sel(i), 0, 0)),
                  pl.BlockSpec((1, gw), lambda i: (0, 0))],
        out_specs=[pl.BlockSpec((c, gw), lambda i: (i, 0)),
                   pl.BlockSpec((c, gw), lambda i: (jnp.maximum(i - npc, 0), 0))],
        out_shape=[jax.ShapeDtypeStruct((n, gw), BF16),
                   jax.ShapeDtypeStruct((n - n_prompt, gw), F32)],
        compiler_params=_params("arbitrary"),
        name="chunk_mlp",
    )(proj, proj, w_st, bias_st, g_v.reshape(1, gw))


def _branch_kernel(ap_ref, as_ref, b_ref, wa_ref, wb_ref, ga_ref, gb_ref, o_ref, *, n_prompt_tiles):
    a = jnp.where(pl.program_id(1) < n_prompt_tiles, ap_ref[...], as_ref[...].astype(BF16))
    ya = jnp.dot(a, wa_ref[...], preferred_element_type=F32)
    yb = jnp.dot(b_ref[...], wb_ref[...], preferred_element_type=F32)
    o_ref[...] = (jax.nn.sigmoid(ga_ref[...]) * ya + jax.nn.sigmoid(gb_ref[...]) * yb).astype(o_ref.dtype)


def _branches(a_p, a_s, bm, wa, wb, proj, ga_off, gb_off, n_prompt):
    n, kb = bm.shape
    ka = a_p.shape[1]
    d = wa.shape[1]
    tm = _pow2_tile(512, n_prompt, n - n_prompt)
    tn = _pow2_tile(1024, d, ga_off, gb_off)
    npt = n_prompt // tm
    return pl.pallas_call(
        functools.partial(_branch_kernel, n_prompt_tiles=npt),
        grid=(d // tn, n // tm),
        in_specs=[pl.BlockSpec((tm, ka), lambda j, i: (jnp.minimum(i, npt - 1), 0)),
                  pl.BlockSpec((tm, ka), lambda j, i: (jnp.maximum(i - npt, 0), 0)),
                  pl.BlockSpec((tm, kb), lambda j, i: (i, 0)),
                  pl.BlockSpec((ka, tn), lambda j, i: (0, j)),
                  pl.BlockSpec((kb, tn), lambda j, i: (0, j)),
                  pl.BlockSpec((tm, tn), lambda j, i: (i, ga_off // tn + j)),
                  pl.BlockSpec((tm, tn), lambda j, i: (i, gb_off // tn + j))],
        out_specs=pl.BlockSpec((tm, tn), lambda j, i: (i, j)),
        out_shape=jax.ShapeDtypeStruct((n, d), BF16),
        compiler_params=_params("arbitrary", "arbitrary"),
        name="branches",
    )(a_p, a_s, bm, wa, wb, proj, proj)


def _out_kernel(m_ref, w_ref, xp_ref, xs_ref, gp_ref, gs_ref, side_ref, o_ref, side_out_ref, *, n_prompt_tiles):
    i = pl.program_id(1)
    y = jnp.dot(m_ref[...], w_ref[...], preferred_element_type=F32)
    side_out_ref[...] = side_ref[...].astype(side_out_ref.dtype)

    @pl.when(i < n_prompt_tiles)
    def _():
        o_ref[...] = xp_ref[...] + gp_ref[0] * y

    @pl.when(i >= n_prompt_tiles)
    def _():
        o_ref[...] = xs_ref[...] + gs_ref[...] * y


def _out_proj(mix, w, x_p, x_s, mod_p, mod_s, k_gate, n_prompt, seq, side_table):
    n, d = mix.shape
    tm = _pow2_tile(512, seq, n - n_prompt)
    tn = _pow2_tile(1024, d)
    npt = n_prompt // tm
    per_b = seq // tm
    nj = d // tn
    side_specs, side_shapes = _side_cast_specs((side_table,), nj, n // tm)
    return pl.pallas_call(
        functools.partial(_out_kernel, n_prompt_tiles=npt),
        grid=(nj, n // tm),
        in_specs=[pl.BlockSpec((tm, d), lambda j, i: (i, 0)),
                  pl.BlockSpec((d, tn), lambda j, i: (0, j)),
                  pl.BlockSpec((tm, tn), lambda j, i: (jnp.minimum(i, npt - 1), j)),
                  pl.BlockSpec((tm, tn), lambda j, i: (jnp.maximum(i - npt, 0), j)),
                  pl.BlockSpec((1, 1, tn), lambda j, i: (jnp.minimum(i, npt - 1) // per_b, 0, k_gate * nj + j)),
                  pl.BlockSpec((tm, tn), lambda j, i: (jnp.maximum(i - npt, 0), k_gate * nj + j))] + side_specs,
        out_specs=[pl.BlockSpec((tm, tn), lambda j, i: (i, j))] + side_specs,
        out_shape=[jax.ShapeDtypeStruct((n, d), F32)] + side_shapes,
        compiler_params=_params("arbitrary", "arbitrary"),
        name="out_proj",
    )(mix, w, x_p, x_s, mod_p, mod_s, side_table)


def _cand_pairs(k):
    return [(i, j) for i in range(k) for j in range(k) if (i + 1) * (j + 1) <= k]


def _arg_rounds(s, iota, k, exact):
    rank = jnp.full(s.shape, float(k), F32)
    vals = []
    big = float(s.shape[0])
    for i in range(k):
        m = jnp.max(s, axis=0, keepdims=True)
        if exact:
            idx = jnp.min(jnp.where(s == m, iota, big), axis=0, keepdims=True)
            sel = iota == idx
        else:
            sel = s == m
        rank = jnp.where(sel, float(i), rank)
        s = jnp.where(sel, NEG_INF, s)
        vals.append(m)
    return vals, rank


def _peer_gates(s1, s2, jsum, cand_scr, cols, pairs, k, exact):
    n_iota = lax.broadcasted_iota(jnp.int32, s1.shape, 0).astype(F32)
    a, r1 = _arg_rounds(s1, n_iota, k, exact)
    b, r2 = _arg_rounds(s2, n_iota, k, exact)

    n_cand = cand_scr.shape[0]
    cand_scr[len(pairs):, cols] = jnp.full((n_cand - len(pairs), s1.shape[1]), NEG_INF, F32)
    for p, (i, j) in enumerate(pairs):
        cand_scr[p:p + 1, cols] = a[i] + b[j]
    c0 = cand_scr[:, cols]
    p_iota = lax.broadcasted_iota(jnp.int32, c0.shape, 0).astype(F32)
    _, prank = _arg_rounds(c0, p_iota, k, exact)
    taken = prank < float(k)
    w = jnp.where(taken, jnp.exp(c0 - (a[0] + b[0])), 0.0)
    inv_z = 1.0 / jnp.sum(w, axis=0, keepdims=True)
    ones = jnp.where(taken, 1.0, 0.0)
    jcnt = jnp.dot(jsum, ones.astype(BF16), preferred_element_type=F32)

    eaz = jnp.zeros(s1.shape, F32)
    jd = jnp.zeros(s1.shape, F32)
    eb = jnp.zeros(s1.shape, F32)
    for i in range(k):
        hit1 = r1 == float(i)
        eaz = jnp.where(hit1, jnp.exp(a[i] - a[0]) * inv_z, eaz)
        jd = jnp.where(hit1, jcnt[i:i + 1, :], jd)
        eb = jnp.where(r2 == float(i), jnp.exp(b[i] - b[0]), eb)

    def excess(rank):
        return jnp.abs(jnp.sum(jnp.where(rank < float(k), 1.0, 0.0), axis=0, keepdims=True) - float(k))

    return eaz, jd, eb, r2, excess(r1) + excess(r2) + excess(prank)


def _peer_topk_kernel(q_ref, sk_ref, jsum_ref, eaz_ref, jd_ref, eb_ref, r2_ref, cand_scr, *, half, pairs, k):
    nt = (((1,), (1,)), ((), ()))
    q = q_ref[...].astype(BF16)
    s1 = lax.dot_general(sk_ref[0, 0], q[:, :half], nt, preferred_element_type=F32)
    s2 = lax.dot_general(sk_ref[0, 1], q[:, half:], nt, preferred_element_type=F32)
    jsum = jsum_ref[...]
    n_blocks = s1.shape[1] // LANE_V7X

    def run(exact):
        bad = None
        for cb in range(n_blocks):
            cols = slice(cb * LANE_V7X, (cb + 1) * LANE_V7X)
            eaz, jd, eb, r2, dev = _peer_gates(s1[:, cols], s2[:, cols], jsum, cand_scr, cols, pairs, k, exact)
            eaz_ref[0, :, cols] = eaz
            jd_ref[0, :, cols] = jd
            eb_ref[0, :, cols] = eb.astype(eb_ref.dtype)
            r2_ref[0, :, cols] = r2.astype(r2_ref.dtype)
            bad = dev if bad is None else bad + dev
        return jnp.max(bad)

    miscount = run(exact=False)

    @pl.when(miscount > 0.5)
    def _():
        run(exact=True)


def _peer_topk(qp, sub_keys_b, n_prompt):
    n = qp.shape[0]
    heads, _, n_keys, half = sub_keys_b.shape
    tt = _pow2_tile(256, n_prompt, n - n_prompt)
    pairs = _cand_pairs(PEER_TOPK)
    n_cand = -(-len(pairs) // 64) * 64
    jsum = np.zeros((PEER_TOPK, n_cand), np.float32)
    for p, (i, _) in enumerate(pairs):
        jsum[i, p] = 1.0
    kern = functools.partial(_peer_topk_kernel, half=half, pairs=pairs, k=PEER_TOPK)
    maps = jax.ShapeDtypeStruct((heads, n_keys, n), F32)
    maps_b = jax.ShapeDtypeStruct((heads, n_keys, n), BF16)
    mspec = pl.BlockSpec((1, n_keys, tt), lambda i, h: (h, 0, i))
    return pl.pallas_call(
        kern,
        grid=(n // tt, heads),
        in_specs=[pl.BlockSpec((tt, 2 * half), lambda i, h: (i, h)),
                  pl.BlockSpec((1, 2, n_keys, half), lambda i, h: (h, 0, 0, 0)),
                  pl.BlockSpec(jsum.shape, lambda i, h: (0, 0))],
        out_specs=[mspec, mspec, mspec, mspec],
        out_shape=[maps, maps, maps_b, maps_b],
        scratch_shapes=[pltpu.VMEM((n_cand, tt), F32)],
        compiler_params=_params("arbitrary", "arbitrary"),
        name="peer_topk",
    )(qp, sub_keys_b, jnp.asarray(jsum, BF16))


def _peer_main_kernel(h_ref, u_ref, v_ref, eaz_ref, jd_ref, eb_ref, r2_ref, o_ref,
                      ga0_scr, ga1_scr, g0_scr, g1_scr, *, heads, n_keys, rows_per_step, n_steps):
    e = pl.program_id(1)
    blocks = [(slice(r * n_keys, (r + 1) * n_keys), slice(tb * LANE_V7X, (tb + 1) * LANE_V7X))
              for r in range(rows_per_step) for tb in range(h_ref.shape[0] // LANE_V7X)]
    n_slices = math.gcd(len(blocks), 16)

    def gates(chunk, g_scr, part=None):
        for sl, ts in (blocks if part is None else blocks[part::n_slices]):
            n1 = chunk * rows_per_step + sl.start // n_keys
            g = None
            for h in range(heads):
                ea = eaz_ref[h, pl.ds(n1, 1), :][:, ts].astype(BF16)
                jd = jd_ref[h, pl.ds(n1, 1), :][:, ts].astype(BF16)
                t = ea * jnp.where(r2_ref[h, :, ts] < jd, eb_ref[h, :, ts], jnp.zeros((), BF16))
                g = t if g is None else g + t
            g_scr[sl, ts] = g

    @pl.when(e == 0)
    def _():
        o_ref[...] = jnp.zeros_like(o_ref)
        ga1_scr[...] = jnp.zeros_like(ga1_scr)
        gates(0, g0_scr)

    def stage(ga_cur, ga_prev, g_cur, g_next):
        act_t = lax.dot_general(u_ref[...], h_ref[...], (((1,), (1,)), ((), ())),
                                preferred_element_type=F32)
        dq = o_ref.shape[1] // n_slices
        for q in range(n_slices):
            qs = slice(q * dq, (q + 1) * dq)
            o_ref[:, qs] += jnp.dot(ga_prev[...], v_ref[:, qs], preferred_element_type=F32)
            for sl, ts in blocks[q::n_slices]:
                ga_cur[ts, sl] = (g_cur[sl, ts] * jax.nn.gelu(act_t[sl, ts].astype(BF16))).T
            gates(jnp.minimum(e + 1, n_steps - 1), g_next, part=q)

    @pl.when(e % 2 == 0)
    def _():
        stage(ga0_scr, ga1_scr, g0_scr, g1_scr)

    @pl.when(e % 2 == 1)
    def _():
        stage(ga1_scr, ga0_scr, g1_scr, g0_scr)


def _peer_main(h2, u_b, v_b, maps, n_prompt):
    n, d = h2.shape
    n_exp = u_b.shape[0]
    heads, n_keys, _ = maps[0].shape
    tt = _pow2_tile(512, n_prompt, n - n_prompt)
    ec = _pow2_tile(1024, n_exp)
    rps = ec // n_keys
    n_steps = n_exp // ec
    kern = functools.partial(_peer_main_kernel, heads=heads, n_keys=n_keys, rows_per_step=rps, n_steps=n_steps)
    once = pl.Buffered(1)
    mspec = pl.BlockSpec((heads, n_keys, tt), lambda i, e: (0, 0, i), pipeline_mode=once)
    return pl.pallas_call(
        kern,
        grid=(n // tt, n_steps + 1),
        in_specs=[pl.BlockSpec((tt, d), lambda i, e: (i, 0), pipeline_mode=once),
                  pl.BlockSpec((ec, d), lambda i, e: (jnp.minimum(e, n_steps - 1), 0)),
                  pl.BlockSpec((ec, d), lambda i, e: (jnp.maximum(e - 1, 0), 0)),
                  mspec, mspec, mspec, mspec],
        out_specs=pl.BlockSpec((tt, d), lambda i, e: (i, 0), pipeline_mode=once),
        out_shape=jax.ShapeDtypeStruct((n, d), F32),
        scratch_shapes=[pltpu.VMEM((tt, ec), BF16), pltpu.VMEM((tt, ec), BF16),
                        pltpu.VMEM((ec, tt), BF16), pltpu.VMEM((ec, tt), BF16)],
        compiler_params=_params("arbitrary", "arbitrary"),
        name="peer_main",
    )(h2, u_b, v_b, *maps)


def _final_kernel(x_ref, p_ref, gp_ref, gs_ref, g_ref, yp_ref, ys_ref, *, n_prompt_tiles):
    i = pl.program_id(0)

    def norm(x):
        return x * lax.rsqrt(jnp.mean(x * x, axis=-1, keepdims=True) + EPS) * g_ref[...]

    @pl.when(i < n_prompt_tiles)
    def _():
        yp_ref[...] = norm(x_ref[...] + gp_ref[0] * p_ref[...])

    @pl.when(i >= n_prompt_tiles)
    def _():
        ys_ref[...] = norm(x_ref[...] + gs_ref[...] * p_ref[...])


def _final(x1, peer_out, mod_p, mod_s, k_gate, g_final, n_prompt, seq):
    n, d = x1.shape
    tm = _pow2_tile(256, seq, n - n_prompt)
    npt = n_prompt // tm
    per_b = seq // tm
    return pl.pallas_call(
        functools.partial(_final_kernel, n_prompt_tiles=npt),
        grid=(n // tm,),
        in_specs=[pl.BlockSpec((tm, d), lambda i: (i, 0)),
                  pl.BlockSpec((tm, d), lambda i: (i, 0)),
                  pl.BlockSpec((1, 1, d), lambda i: (jnp.minimum(i, npt - 1) // per_b, 0, k_gate)),
                  pl.BlockSpec((tm, d), lambda i: (jnp.maximum(i - npt, 0), k_gate)),
                  pl.BlockSpec((1, d), lambda i: (0, 0))],
        out_specs=[pl.BlockSpec((tm, d), lambda i: (jnp.minimum(i, npt - 1), 0)),
                   pl.BlockSpec((tm, d), lambda i: (jnp.maximum(i - npt, 0), 0))],
        out_shape=[jax.ShapeDtypeStruct((n_prompt, d), F32),
                   jax.ShapeDtypeStruct((n - n_prompt, d), F32)],
        compiler_params=_params("arbitrary"),
        name="final_norm",
    )(x1, peer_out, mod_p, mod_s, g_final.reshape(1, d))


def kernel(x_prompt, x_sample, c_prompt, c_sample, state_hgrn, w_ada, b_ada, g_norm1, w_in, hgrn_lb_logits,
           g_hgrn_out, g_gmlp_v, w_spatial, b_spatial, w_branch_a, w_branch_b, w_out, g_norm2, w_peer_q,
           peer_sub_keys, peer_u, peer_v, g_final):
    batch, seq, d = x_prompt.shape
    db, dt, _ = x_sample.shape
    depth, _, heads, dk, dv = state_hgrn.shape
    assert depth == 1 and dk == dv == LANE_V7X
    hw = heads * dk
    groups, gc = w_spatial.shape[1], w_spatial.shape[2]
    gw = w_branch_b.shape[1]
    assert gc == LANE_V7X and gw // groups == LANE_V7X and dt <= gc and gc % dt == 0
    n_prompt, n_sample = batch * seq, db * dt
    n = n_prompt + n_sample

    c_all = jnp.concatenate([c_prompt, c_sample], axis=0)
    pad = (-c_all.shape[0]) % 8
    c_all = jnp.pad(c_all, ((0, pad), (0, 0)))
    mod = _ada(c_all, w_ada[0], b_ada[0])
    mod_p = mod[:batch].reshape(batch, 1, N_MOD * d)
    mod_s = jnp.repeat(mod[batch:batch + db], dt, axis=0)

    x_p = x_prompt.reshape(n_prompt, d)
    x_s = x_sample.reshape(n_sample, d)

    h1 = _norm_mod(x_p, x_s, g_norm1[0], mod_p, mod_s, 1, 0, n_prompt, n_sample, seq)
    proj, peer_u_b = _matmul(h1, w_in[0], F32, n_prompt, "in_proj", side_tables=(peer_u[0],))

    lb = jnp.cumsum(jax.nn.softmax(hgrn_lb_logits.astype(F32), axis=0), axis=0)[0].reshape(1, hw)
    g_ho = g_hgrn_out[0].reshape(1, hw)
    a_p, st_p = _hgrn_prompt(proj, lb, g_ho, batch, seq, heads, dk, n_prompt)
    proj_s = proj[n_prompt:].reshape(db, dt, proj.shape[1])
    a_s, st_s = _hgrn_sample(proj_s, lb, g_ho, state_hgrn, heads, dk)

    tril = jnp.tril(jnp.ones((gc, gc), F32))
    w_sp = w_spatial[0]
    blk = jnp.arange(gc) // dt
    w_samp = jnp.tile(w_sp[:, :dt, :dt], (1, gc // dt, gc // dt)) * (blk[:, None] == blk[None, :])
    w_st = jnp.stack([w_sp * tril, w_samp * tril]).astype(BF16)
    bias_full = jnp.repeat(b_spatial[0].T, gw // groups, axis=1)
    bias_st = jnp.stack([bias_full, jnp.tile(bias_full[:dt], (gc // dt, 1))])
    gu_blk = 4 * hw // gw
    assert gu_blk * gw == 4 * hw
    bm, v_s = _gmlp(proj, w_st, bias_st, g_gmlp_v[0], n_prompt, gu_blk, gu_blk + 1)

    ga_off = 4 * hw + 2 * gw
    mix = _branches(a_p, a_s.reshape(n_sample, hw), bm, w_branch_a[0].astype(BF16), w_branch_b[0].astype(BF16), proj,
                    ga_off, ga_off + d, n_prompt)
    x1, peer_v_b = _out_proj(mix, w_out[0].astype(BF16), x_p, x_s, mod_p, mod_s, 2, n_prompt, seq, peer_v[0])

    h2 = _norm_mod(x1, x1, g_norm2[0], mod_p, mod_s, 4, 3, n_prompt, n_sample, seq)
    qp = _matmul(h2, w_peer_q[0], F32, n_prompt, "peer_query")
    maps = _peer_topk(qp, peer_sub_keys[0].astype(BF16), n_prompt)
    peer_out = _peer_main(h2, peer_u_b, peer_v_b, maps, n_prompt)
    y_p, y_s = _final(x1, peer_out, mod_p, mod_s, 5, g_final, n_prompt, seq)

    state_p = jnp.swapaxes(st_p, -1, -2)
    return (y_p.reshape(batch, seq, d), y_s.reshape(db, dt, d), state_p, st_s,
            v_s.reshape(1, db, dt, gw))
```

```python
import functools
import math

import numpy as np
import jax
import jax.numpy as jnp
from jax import lax
from jax.experimental import pallas as pl
from jax.experimental.pallas import tpu as pltpu

EPS = 1e-6
N_MOD = 6
PEER_TOPK = 16
LANE_V7X = 128
VMEM_LIMIT_V7X = 60 * 1024 * 1024
F32 = jnp.float32
BF16 = jnp.bfloat16
NEG_INF = float("-inf")


def _params(*sem):
    return pltpu.CompilerParams(dimension_semantics=sem, vmem_limit_bytes=VMEM_LIMIT_V7X)


def _pow2_tile(target, *sizes):
    g = 0
    for s in sizes:
        g = math.gcd(g, s)
    t = 1
    while t * 2 <= target and g % (t * 2) == 0:
        t *= 2
    return t


def _silu(x):
    return x * jax.nn.sigmoid(x)


def _ada_kernel(c_ref, w_ref, b_ref, o_ref):
    s = _silu(c_ref[...]).astype(BF16)
    o_ref[...] = jnp.dot(s, w_ref[...].astype(BF16), preferred_element_type=F32) + b_ref[...]


def _ada(c_all, w_ada, b_ada):
    m, d = c_all.shape
    n = w_ada.shape[1]
    tn = _pow2_tile(512, n)
    return pl.pallas_call(
        _ada_kernel,
        grid=(n // tn,),
        in_specs=[pl.BlockSpec((m, d), lambda j: (0, 0)),
                  pl.BlockSpec((d, tn), lambda j: (0, j)),
                  pl.BlockSpec((1, tn), lambda j: (0, j))],
        out_specs=pl.BlockSpec((m, tn), lambda j: (0, j)),
        out_shape=jax.ShapeDtypeStruct((m, n), F32),
        compiler_params=_params("arbitrary"),
        name="ada_mod",
    )(c_all, w_ada, b_ada.reshape(1, n))


def _norm_mod_kernel(xp_ref, xs_ref, g_ref, scp_ref, shp_ref, scs_ref, shs_ref, o_ref, *, n_prompt_tiles):
    i = pl.program_id(0)

    def normed(x):
        return x * lax.rsqrt(jnp.mean(x * x, axis=-1, keepdims=True) + EPS) * g_ref[...]

    @pl.when(i < n_prompt_tiles)
    def _():
        o_ref[...] = (normed(xp_ref[...]) * (1.0 + scp_ref[0]) + shp_ref[0]).astype(o_ref.dtype)

    @pl.when(i >= n_prompt_tiles)
    def _():
        o_ref[...] = (normed(xs_ref[...]) * (1.0 + scs_ref[...]) + shs_ref[...]).astype(o_ref.dtype)


def _row_sources(x_p, x_s, n_prompt, tm):
    npt = n_prompt // tm
    s_off = npt if x_s is x_p else 0
    return (lambda i: (jnp.minimum(i, npt - 1), 0)), (lambda i: (s_off + jnp.maximum(i - npt, 0), 0))


def _norm_mod(x_p, x_s, g, mod_p, mod_s, k_scale, k_shift, n_prompt, n_sample, seq):
    d = x_p.shape[1]
    n = n_prompt + n_sample
    tm = _pow2_tile(256, seq, n_sample)
    npt = n_prompt // tm
    per_b = seq // tm
    p_map, s_map = _row_sources(x_p, x_s, n_prompt, tm)

    def pidx(k):
        return lambda i: (jnp.minimum(i, npt - 1) // per_b, 0, k)

    def sidx(k):
        return lambda i: (jnp.maximum(i - npt, 0), k)

    return pl.pallas_call(
        functools.partial(_norm_mod_kernel, n_prompt_tiles=npt),
        grid=(n // tm,),
        in_specs=[pl.BlockSpec((tm, d), p_map),
                  pl.BlockSpec((tm, d), s_map),
                  pl.BlockSpec((1, d), lambda i: (0, 0)),
                  pl.BlockSpec((1, 1, d), pidx(k_scale)),
                  pl.BlockSpec((1, 1, d), pidx(k_shift)),
                  pl.BlockSpec((tm, d), sidx(k_scale)),
                  pl.BlockSpec((tm, d), sidx(k_shift))],
        out_specs=pl.BlockSpec((tm, d), lambda i: (i, 0)),
        out_shape=jax.ShapeDtypeStruct((n, d), BF16),
        compiler_params=_params("arbitrary"),
        name="norm_mod",
    )(x_p, x_s, g.reshape(1, d), mod_p, mod_p, mod_s, mod_s)


def _mm_kernel(a_ref, w_ref, *rest, n_side):
    side_in, o_ref, side_out, wb_scr = rest[:n_side], rest[n_side], rest[n_side + 1:2 * n_side + 1], rest[-1]

    @pl.when(pl.program_id(1) == 0)
    def _():
        wb_scr[...] = w_ref[...].astype(BF16)

    o_ref[...] = jnp.dot(a_ref[...], wb_scr[...], preferred_element_type=F32).astype(o_ref.dtype)
    for src, dst in zip(side_in, side_out):
        dst[...] = src[...].astype(dst.dtype)


def _side_cast_specs(side_tables, nj, ni):
    specs, shapes = [], []
    for t in side_tables:
        rows = t.shape[0]
        rb = rows // _pow2_tile(nj * ni, rows)
        last = rows // rb - 1
        specs.append(pl.BlockSpec((rb, t.shape[1]), lambda j, i, last=last: (jnp.minimum(j * ni + i, last), 0)))
        shapes.append(jax.ShapeDtypeStruct(t.shape, BF16))
    return specs, shapes


def _matmul(a, w, out_dtype, n_prompt, name, side_tables=()):
    n, k = a.shape
    nc = w.shape[1]
    tm = _pow2_tile(512, n_prompt, n - n_prompt)
    tn = _pow2_tile(1024, nc)
    ni = n // tm
    side_specs, side_shapes = _side_cast_specs(side_tables, nc // tn, ni)
    outs = pl.pallas_call(
        functools.partial(_mm_kernel, n_side=len(side_tables)),
        grid=(nc // tn, ni),
        in_specs=[pl.BlockSpec((tm, k), lambda j, i: (i, 0)),
                  pl.BlockSpec((k, tn), lambda j, i: (0, j))] + side_specs,
        out_specs=[pl.BlockSpec((tm, tn), lambda j, i: (i, j))] + side_specs,
        out_shape=[jax.ShapeDtypeStruct((n, nc), out_dtype)] + side_shapes,
        scratch_shapes=[pltpu.VMEM((k, tn), BF16)],
        compiler_params=_params("arbitrary", "arbitrary"),
        name=name,
    )(a, w, *side_tables)
    return outs[0] if not side_tables else outs


def _hgrn_tables(c):
    nl = int(math.log2(c))
    r = np.arange(c)[:, None]
    j = np.arange(c)[None, :]
    mats = []
    lvl = np.full((c, c), -1, np.int32)
    for l in range(nl):
        m = c >> (l + 1)
        mid = (r // (2 * m)) * 2 * m + m
        upper = r >= mid
        t = np.where(upper, (j >= mid) & (j <= r), (j > r) & (j <= mid - 1))
        mats.append(t)
        same = (r // (2 * m)) == (j // (2 * m))
        lvl = np.where(same & upper & (j < mid), l, lvl)
    lvl = np.where(r == j, nl, lvl)
    mats.append(j <= r)
    mats.append(j > r)
    t_all = np.concatenate(mats, axis=0).astype(np.float32)
    t_all = np.concatenate([t_all, t_all], axis=1)
    return jnp.asarray(t_all, BF16), jnp.asarray(lvl), nl


def _hgrn_prompt_kernel(q_ref, f_ref, v_ref, og_ref, lb_ref, g_ref, tall_ref, lvl_ref,
                        a_ref, st_ref, s_scr, *, chunk, n_levels, n_chunks, heads_per_step, dk):
    c = chunk
    s_scr[...] = jnp.zeros_like(s_scr)
    lb = lb_ref[...]
    lvl = lvl_ref[...]
    rows = lax.broadcasted_iota(jnp.int32, (c, q_ref.shape[1]), 0)
    nt = (((1,), (1,)), ((), ()))

    def body(ci, carry):
        sl = pl.ds(pl.multiple_of(ci * c, c), c)
        qs = _silu(q_ref[sl, :])
        f = lb + (1.0 - lb) * jax.nn.sigmoid(f_ref[sl, :])
        k = 1.0 - f
        lf = jnp.log(f)
        v = v_ref[sl, :].astype(BF16)
        lf_hi = lf.astype(BF16)
        lf_lo = (lf - lf_hi.astype(F32)).astype(BF16)
        ex = jnp.dot(tall_ref[...], jnp.concatenate([lf_hi, lf_lo], axis=0), preferred_element_type=F32)
        qs_b = qs.astype(BF16)
        k_b = k.astype(BF16)
        ys = []
        for l in range(n_levels):
            m = c >> (l + 1)
            e_l = jnp.exp(ex[l * c:(l + 1) * c, :])
            ys.append((e_l * jnp.where((rows & m) != 0, qs, k)).astype(BF16))
        bcum = ex[n_levels * c:(n_levels + 1) * c, :]
        brev = ex[(n_levels + 1) * c:(n_levels + 2) * c, :]
        qh = (qs * jnp.exp(bcum)).astype(BF16)
        kh = (k * jnp.exp(brev)).astype(BF16)
        decay = jnp.exp(bcum[c - 1:c, :])
        outs = []
        for hh in range(heads_per_step):
            hs = slice(hh * dk, (hh + 1) * dk)
            scores = jnp.where(lvl == n_levels,
                               lax.dot_general(qs_b[:, hs], k_b[:, hs], nt, preferred_element_type=F32), 0.0)
            for l in range(n_levels):
                y = ys[l][:, hs]
                p = lax.dot_general(y, y, nt, preferred_element_type=F32)
                scores = scores + jnp.where(lvl == l, p, 0.0)
            st = s_scr[hh]
            o = jnp.dot(scores.astype(BF16), v[:, hs], preferred_element_type=F32)
            o = o + lax.dot_general(qh[:, hs], st.astype(BF16), nt, preferred_element_type=F32)
            kv_t = lax.dot_general(v[:, hs], kh[:, hs], (((0,), (0,)), ((), ())), preferred_element_type=F32)
            s_scr[hh] = st * decay[:, hs] + kv_t
            outs.append(o * lax.rsqrt(jnp.mean(o * o, axis=-1, keepdims=True) + EPS))
        on = jnp.concatenate(outs, axis=1) if heads_per_step > 1 else outs[0]
        a_ref[sl, :] = (on * g_ref[...] * _silu(og_ref[sl, :])).astype(a_ref.dtype)
        return carry

    lax.fori_loop(0, n_chunks, body, 0, unroll=2)
    st_ref[0, 0] = s_scr[...]


def _hgrn_prompt(proj, lb, g_out, batch, seq, heads, dk, n_rows):
    c = _pow2_tile(128, seq)
    hp = _pow2_tile(4, heads)
    t_all, lvl, nl = _hgrn_tables(c)
    w = hp * dk
    nh = heads // hp
    col = lambda off: (lambda b, h: (b, off + h))
    kern = functools.partial(_hgrn_prompt_kernel, chunk=c, n_levels=nl, n_chunks=seq // c,
                             heads_per_step=hp, dk=dk)
    return pl.pallas_call(
        kern,
        grid=(batch, nh),
        in_specs=[pl.BlockSpec((seq, w), col(0)),
                  pl.BlockSpec((seq, w), col(nh)),
                  pl.BlockSpec((seq, w), col(2 * nh)),
                  pl.BlockSpec((seq, w), col(3 * nh)),
                  pl.BlockSpec((1, w), lambda b, h: (0, h)),
                  pl.BlockSpec((1, w), lambda b, h: (0, h)),
                  pl.BlockSpec(t_all.shape, lambda b, h: (0, 0)),
                  pl.BlockSpec(lvl.shape, lambda b, h: (0, 0))],
        out_specs=[pl.BlockSpec((seq, w), lambda b, h: (b, h)),
                   pl.BlockSpec((1, 1, hp, dk, dk), lambda b, h: (0, b, h, 0, 0))],
        out_shape=[jax.ShapeDtypeStruct((n_rows, heads * dk), BF16),
                   jax.ShapeDtypeStruct((1, batch, heads, dk, dk), F32)],
        scratch_shapes=[pltpu.VMEM((hp, dk, dk), F32)],
        compiler_params=_params("arbitrary", "arbitrary"),
        name="hgrn_prompt",
    )(proj, proj, proj, proj, lb, g_out, t_all, lvl)


def _hgrn_sample_kernel(q_ref, f_ref, v_ref, og_ref, lb_ref, g_ref, s0_ref, a_ref, s_ref, *, bt, steps):
    lb = lb_ref[...]
    g = g_ref[...]
    rows = lax.broadcasted_iota(jnp.int32, (steps, q_ref.shape[2]), 0)

    def body(b, carry):
        qs = _silu(q_ref[b])
        f = lb + (1.0 - lb) * jax.nn.sigmoid(f_ref[b])
        k = 1.0 - f
        lf = jnp.log(f)
        v = v_ref[b]
        cum = []
        run = None
        for t in range(steps):
            run = lf[t:t + 1, :] if run is None else run + lf[t:t + 1, :]
            cum.append(run)
        bmat = jnp.broadcast_to(cum[0], lf.shape)
        for t in range(1, steps):
            bmat = jnp.where(rows == t, cum[t], bmat)
        o = jnp.zeros_like(qs)
        for s in range(steps):
            x = jnp.where(rows >= s, qs * k[s:s + 1, :] * jnp.exp(jnp.minimum(bmat - cum[s], 0.0)), 0.0)
            o = o + jnp.sum(x, axis=-1, keepdims=True) * v[s:s + 1, :]
        s0 = s0_ref[0, b, 0]
        o = o + jnp.dot((qs * jnp.exp(bmat)).astype(BF16), s0.astype(BF16), preferred_element_type=F32)
        kh = (k * jnp.exp(cum[steps - 1] - bmat)).astype(BF16)
        kv = lax.dot_general(kh, v.astype(BF16), (((0,), (0,)), ((), ())), preferred_element_type=F32)
        decay_col = jnp.broadcast_to(jnp.exp(cum[steps - 1]), s0.shape).T
        s_ref[0, b, 0] = s0 * decay_col + kv
        on = o * lax.rsqrt(jnp.mean(o * o, axis=-1, keepdims=True) + EPS)
        a_ref[b] = on * g * _silu(og_ref[b])
        return carry

    lax.fori_loop(0, bt, body, 0, unroll=8)


def _hgrn_sample(proj_s, lb, g_out, s0, heads, dk):
    db, steps, _ = proj_s.shape
    bt = _pow2_tile(32, db)
    col = lambda off: (lambda j, h: (j, 0, off + h))
    kern = functools.partial(_hgrn_sample_kernel, bt=bt, steps=steps)
    return pl.pallas_call(
        kern,
        grid=(db // bt, heads),
        in_specs=[pl.BlockSpec((bt, steps, dk), col(0)),
                  pl.BlockSpec((bt, steps, dk), col(heads)),
                  pl.BlockSpec((bt, steps, dk), col(2 * heads)),
                  pl.BlockSpec((bt, steps, dk), col(3 * heads)),
                  pl.BlockSpec((1, dk), lambda j, h: (0, h)),
                  pl.BlockSpec((1, dk), lambda j, h: (0, h)),
                  pl.BlockSpec((1, bt, 1, dk, dk), lambda j, h: (0, j, h, 0, 0))],
        out_specs=[pl.BlockSpec((bt, steps, dk), lambda j, h: (j, 0, h)),
                   pl.BlockSpec((1, bt, 1, dk, dk), lambda j, h: (0, j, h, 0, 0))],
        out_shape=[jax.ShapeDtypeStruct((db, steps, heads * dk), F32),
                   jax.ShapeDtypeStruct(s0.shape, F32)],
        compiler_params=_params("arbitrary", "arbitrary"),
        name="hgrn_sample",
    )(proj_s, proj_s, proj_s, proj_s, lb, g_out, s0)


def _gmlp_kernel(gu_ref, gv_ref, w_ref, bias_ref, g_ref, o_ref, v_ref, *, groups, gd):
    u = jax.nn.gelu(gu_ref[...])
    vv = jax.nn.gelu(gv_ref[...])
    r = lax.rsqrt(jnp.mean(vv * vv, axis=-1, keepdims=True) + EPS)
    v = vv * r * g_ref[...]
    v_ref[...] = v
    for gi in range(groups):
        sl = slice(gi * gd, (gi + 1) * gd)
        z = jnp.dot(w_ref[0, gi], v[:, sl].astype(BF16), preferred_element_type=F32) + bias_ref[0, :, sl]
        o_ref[:, sl] = (u[:, sl] * z).astype(o_ref.dtype)


def _gmlp(proj, w_st, bias_st, g_v, n_prompt, gu_blk, gv_blk):
    n = proj.shape[0]
    _, groups, c, _ = w_st.shape
    gw = bias_st.shape[2]
    npc = n_prompt // c
    sel = lambda i: jnp.where(i >= npc, 1, 0)
    kern = functools.partial(_gmlp_kernel, groups=groups, gd=gw // groups)
    return pl.pallas_call(
        kern,
        grid=(n // c,),
        in_specs=[pl.BlockSpec((c, gw), lambda i: (i, gu_blk)),
                  pl.BlockSpec((c, gw), lambda i: (i, gv_blk)),
                  pl.BlockSpec((1, groups, c, c), lambda i: (sel(i), 0, 0, 0)),
                  pl.BlockSpec((1, c, gw), lambda i: (sel(i), 0, 0)),
                  pl.BlockSpec((1, gw), lambda i: (0, 0))],
        out_specs=[pl.BlockSpec((c, gw), lambda i: (i, 0)),
                   pl.BlockSpec((c, gw), lambda i: (jnp.maximum(i - npc, 0), 0))],
        out_shape=[jax.ShapeDtypeStruct((n, gw), BF16),
                   jax.ShapeDtypeStruct((n - n_prompt, gw), F32)],
        compiler_params=_params("arbitrary"),
        name="chunk_mlp",
    )(proj, proj, w_st, bias_st, g_v.reshape(1, gw))


def _branch_kernel(ap_ref, as_ref, b_ref, wa_ref, wb_ref, ga_ref, gb_ref, o_ref, *, n_prompt_tiles):
    a = jnp.where(pl.program_id(1) < n_prompt_tiles, ap_ref[...], as_ref[...].astype(BF16))
    ya = jnp.dot(a, wa_ref[...], preferred_element_type=F32)
    yb = jnp.dot(b_ref[...], wb_ref[...], preferred_element_type=F32)
    o_ref[...] = (jax.nn.sigmoid(ga_ref[...]) * ya + jax.nn.sigmoid(gb_ref[...]) * yb).astype(o_ref.dtype)


def _branches(a_p, a_s, bm, wa, wb, proj, ga_off, gb_off, n_prompt):
    n, kb = bm.shape
    ka = a_p.shape[1]
    d = wa.shape[1]
    tm = _pow2_tile(512, n_prompt, n - n_prompt)
    tn = _pow2_tile(1024, d, ga_off, gb_off)
    npt = n_prompt // tm
    return pl.pallas_call(
        functools.partial(_branch_kernel, n_prompt_tiles=npt),
        grid=(d // tn, n // tm),
        in_specs=[pl.BlockSpec((tm, ka), lambda j, i: (jnp.minimum(i, npt - 1), 0)),
                  pl.BlockSpec((tm, ka), lambda j, i: (jnp.maximum(i - npt, 0), 0)),
                  pl.BlockSpec((tm, kb), lambda j, i: (i, 0)),
                  pl.BlockSpec((ka, tn), lambda j, i: (0, j)),
                  pl.BlockSpec((kb, tn), lambda j, i: (0, j)),
                  pl.BlockSpec((tm, tn), lambda j, i: (i, ga_off // tn + j)),
                  pl.BlockSpec((tm, tn), lambda j, i: (i, gb_off // tn + j))],
        out_specs=pl.BlockSpec((tm, tn), lambda j, i: (i, j)),
        out_shape=jax.ShapeDtypeStruct((n, d), BF16),
        compiler_params=_params("arbitrary", "arbitrary"),
        name="branches",
    )(a_p, a_s, bm, wa, wb, proj, proj)


def _out_kernel(m_ref, w_ref, xp_ref, xs_ref, gp_ref, gs_ref, side_ref, o_ref, side_out_ref, *, n_prompt_tiles):
    i = pl.program_id(1)
    y = jnp.dot(m_ref[...], w_ref[...], preferred_element_type=F32)
    side_out_ref[...] = side_ref[...].astype(side_out_ref.dtype)

    @pl.when(i < n_prompt_tiles)
    def _():
        o_ref[...] = xp_ref[...] + gp_ref[0] * y

    @pl.when(i >= n_prompt_tiles)
    def _():
        o_ref[...] = xs_ref[...] + gs_ref[...] * y


def _out_proj(mix, w, x_p, x_s, mod_p, mod_s, k_gate, n_prompt, seq, side_table):
    n, d = mix.shape
    tm = _pow2_tile(512, seq, n - n_prompt)
    tn = _pow2_tile(1024, d)
    npt = n_prompt // tm
    per_b = seq // tm
    nj = d // tn
    side_specs, side_shapes = _side_cast_specs((side_table,), nj, n // tm)
    return pl.pallas_call(
        functools.partial(_out_kernel, n_prompt_tiles=npt),
        grid=(nj, n // tm),
        in_specs=[pl.BlockSpec((tm, d), lambda j, i: (i, 0)),
                  pl.BlockSpec((d, tn), lambda j, i: (0, j)),
                  pl.BlockSpec((tm, tn), lambda j, i: (jnp.minimum(i, npt - 1), j)),
                  pl.BlockSpec((tm, tn), lambda j, i: (jnp.maximum(i - npt, 0), j)),
                  pl.BlockSpec((1, 1, tn), lambda j, i: (jnp.minimum(i, npt - 1) // per_b, 0, k_gate * nj + j)),
                  pl.BlockSpec((tm, tn), lambda j, i: (jnp.maximum(i - npt, 0), k_gate * nj + j))] + side_specs,
        out_specs=[pl.BlockSpec((tm, tn), lambda j, i: (i, j))] + side_specs,
        out_shape=[jax.ShapeDtypeStruct((n, d), F32)] + side_shapes,
        compiler_params=_params("arbitrary", "arbitrary"),
        name="out_proj",
    )(mix, w, x_p, x_s, mod_p, mod_s, side_table)


def _cand_pairs(k):
    return [(i, j) for i in range(k) for j in range(k) if (i + 1) * (j + 1) <= k]


def _arg_rounds(s, iota, k, exact):
    rank = jnp.full(s.shape, float(k), F32)
    vals = []
    big = float(s.shape[0])
    for i in range(k):
        m = jnp.max(s, axis=0, keepdims=True)
        if exact:
            idx = jnp.min(jnp.where(s == m, iota, big), axis=0, keepdims=True)
            sel = iota == idx
        else:
            sel = s == m
        rank = jnp.where(sel, float(i), rank)
        s = jnp.where(sel, NEG_INF, s)
        vals.append(m)
    return vals, rank


def _peer_gates(s1, s2, jsum, cand_scr, cols, pairs, k, exact):
    n_iota = lax.broadcasted_iota(jnp.int32, s1.shape, 0).astype(F32)
    a, r1 = _arg_rounds(s1, n_iota, k, exact)
    b, r2 = _arg_rounds(s2, n_iota, k, exact)

    n_cand = cand_scr.shape[0]
    cand_scr[len(pairs):, cols] = jnp.full((n_cand - len(pairs), s1.shape[1]), NEG_INF, F32)
    for p, (i, j) in enumerate(pairs):
        cand_scr[p:p + 1, cols] = a[i] + b[j]
    c0 = cand_scr[:, cols]
    p_iota = lax.broadcasted_iota(jnp.int32, c0.shape, 0).astype(F32)
    _, prank = _arg_rounds(c0, p_iota, k, exact)
    taken = prank < float(k)
    w = jnp.where(taken, jnp.exp(c0 - (a[0] + b[0])), 0.0)
    inv_z = 1.0 / jnp.sum(w, axis=0, keepdims=True)
    ones = jnp.where(taken, 1.0, 0.0)
    jcnt = jnp.dot(jsum, ones.astype(BF16), preferred_element_type=F32)

    eaz = jnp.zeros(s1.shape, F32)
    jd = jnp.zeros(s1.shape, F32)
    eb = jnp.zeros(s1.shape, F32)
    for i in range(k):
        hit1 = r1 == float(i)
        eaz = jnp.where(hit1, jnp.exp(a[i] - a[0]) * inv_z, eaz)
        jd = jnp.where(hit1, jcnt[i:i + 1, :], jd)
        eb = jnp.where(r2 == float(i), jnp.exp(b[i] - b[0]), eb)

    def excess(rank):
        return jnp.abs(jnp.sum(jnp.where(rank < float(k), 1.0, 0.0), axis=0, keepdims=True) - float(k))

    return eaz, jd, eb, r2, excess(r1) + excess(r2) + excess(prank)


def _peer_topk_kernel(q_ref, sk_ref, jsum_ref, eaz_ref, jd_ref, eb_ref, r2_ref, cand_scr, *, half, pairs, k):
    nt = (((1,), (1,)), ((), ()))
    q = q_ref[...].astype(BF16)
    s1 = lax.dot_general(sk_ref[0, 0], q[:, :half], nt, preferred_element_type=F32)
    s2 = lax.dot_general(sk_ref[0, 1], q[:, half:], nt, preferred_element_type=F32)
    jsum = jsum_ref[...]
    n_blocks = s1.shape[1] // LANE_V7X

    def run(exact):
        bad = None
        for cb in range(n_blocks):
            cols = slice(cb * LANE_V7X, (cb + 1) * LANE_V7X)
            eaz, jd, eb, r2, dev = _peer_gates(s1[:, cols], s2[:, cols], jsum, cand_scr, cols, pairs, k, exact)
            eaz_ref[0, :, cols] = eaz
            jd_ref[0, :, cols] = jd
            eb_ref[0, :, cols] = eb.astype(eb_ref.dtype)
            r2_ref[0, :, cols] = r2.astype(r2_ref.dtype)
            bad = dev if bad is None else bad + dev
        return jnp.max(bad)

    miscount = run(exact=False)

    @pl.when(miscount > 0.5)
    def _():
        run(exact=True)


def _peer_topk(qp, sub_keys_b, n_prompt):
    n = qp.shape[0]
    heads, _, n_keys, half = sub_keys_b.shape
    tt = _pow2_tile(256, n_prompt, n - n_prompt)
    pairs = _cand_pairs(PEER_TOPK)
    n_cand = -(-len(pairs) // 64) * 64
    jsum = np.zeros((PEER_TOPK, n_cand), np.float32)
    for p, (i, _) in enumerate(pairs):
        jsum[i, p] = 1.0
    kern = functools.partial(_peer_topk_kernel, half=half, pairs=pairs, k=PEER_TOPK)
    maps = jax.ShapeDtypeStruct((heads, n_keys, n), F32)
    maps_b = jax.ShapeDtypeStruct((heads, n_keys, n), BF16)
    mspec = pl.BlockSpec((1, n_keys, tt), lambda i, h: (h, 0, i))
    return pl.pallas_call(
        kern,
        grid=(n // tt, heads),
        in_specs=[pl.BlockSpec((tt, 2 * half), lambda i, h: (i, h)),
                  pl.BlockSpec((1, 2, n_keys, half), lambda i, h: (h, 0, 0, 0)),
                  pl.BlockSpec(jsum.shape, lambda i, h: (0, 0))],
        out_specs=[mspec, mspec, mspec, mspec],
        out_shape=[maps, maps, maps_b, maps_b],
        scratch_shapes=[pltpu.VMEM((n_cand, tt), F32)],
        compiler_params=_params("arbitrary", "arbitrary"),
        name="peer_topk",
    )(qp, sub_keys_b, jnp.asarray(jsum, BF16))


def _peer_main_kernel(h_ref, u_ref, v_ref, eaz_ref, jd_ref, eb_ref, r2_ref, o_ref,
                      ga0_scr, ga1_scr, g0_scr, g1_scr, ht_scr, *, heads, n_keys, rows_per_step, n_steps):
    e = pl.program_id(1)
    blocks = [(slice(r * n_keys, (r + 1) * n_keys), slice(tb * LANE_V7X, (tb + 1) * LANE_V7X))
              for r in range(rows_per_step) for tb in range(h_ref.shape[0] // LANE_V7X)]
    n_slices = math.gcd(len(blocks), 16)

    def gates(chunk, g_scr, part=None):
        for sl, ts in (blocks if part is None else blocks[part::n_slices]):
            n1 = chunk * rows_per_step + sl.start // n_keys
            g = None
            for h in range(heads):
                ea = eaz_ref[h, pl.ds(n1, 1), :][:, ts].astype(BF16)
                jd = jd_ref[h, pl.ds(n1, 1), :][:, ts].astype(BF16)
                t = ea * jnp.where(r2_ref[h, :, ts] < jd, eb_ref[h, :, ts], jnp.zeros((), BF16))
                g = t if g is None else g + t
            g_scr[sl, ts] = g

    @pl.when(e == 0)
    def _():
        o_ref[...] = jnp.zeros_like(o_ref)
        ga1_scr[...] = jnp.zeros_like(ga1_scr)
        gates(0, g0_scr)
        ht_scr[...] = h_ref[...].T

    def stage(ga_cur, ga_prev, g_cur, g_next):
        act_t = jnp.dot(u_ref[...], ht_scr[...], preferred_element_type=F32)
        dq = o_ref.shape[1] // n_slices
        for q in range(n_slices):
            qs = slice(q * dq, (q + 1) * dq)
            o_ref[:, qs] += jnp.dot(ga_prev[...], v_ref[:, qs], preferred_element_type=F32)
            for sl, ts in blocks[q::n_slices]:
                ga_cur[ts, sl] = (g_cur[sl, ts] * jax.nn.gelu(act_t[sl, ts].astype(BF16))).T
            gates(jnp.minimum(e + 1, n_steps - 1), g_next, part=q)

    @pl.when(e % 2 == 0)
    def _():
        stage(ga0_scr, ga1_scr, g0_scr, g1_scr)

    @pl.when(e % 2 == 1)
    def _():
        stage(ga1_scr, ga0_scr, g1_scr, g0_scr)


def _peer_main(h2, u_b, v_b, maps, n_prompt):
    n, d = h2.shape
    n_exp = u_b.shape[0]
    heads, n_keys, _ = maps[0].shape
    tt = _pow2_tile(512, n_prompt, n - n_prompt)
    ec = _pow2_tile(512, n_exp)
    rps = ec // n_keys
    n_steps = n_exp // ec
    kern = functools.partial(_peer_main_kernel, heads=heads, n_keys=n_keys, rows_per_step=rps, n_steps=n_steps)
    once = pl.Buffered(1)
    mspec = pl.BlockSpec((heads, n_keys, tt), lambda i, e: (0, 0, i), pipeline_mode=once)
    return pl.pallas_call(
        kern,
        grid=(n // tt, n_steps + 1),
        in_specs=[pl.BlockSpec((tt, d), lambda i, e: (i, 0), pipeline_mode=once),
                  pl.BlockSpec((ec, d), lambda i, e: (jnp.minimum(e, n_steps - 1), 0)),
                  pl.BlockSpec((ec, d), lambda i, e: (jnp.maximum(e - 1, 0), 0)),
                  mspec, mspec, mspec, mspec],
        out_specs=pl.BlockSpec((tt, d), lambda i, e: (i, 0), pipeline_mode=once),
        out_shape=jax.ShapeDtypeStruct((n, d), F32),
        scratch_shapes=[pltpu.VMEM((tt, ec), BF16), pltpu.VMEM((tt, ec), BF16),
                        pltpu.VMEM((ec, tt), BF16), pltpu.VMEM((ec, tt), BF16),
                        pltpu.VMEM((d, tt), BF16)],
        compiler_params=_params("arbitrary", "arbitrary"),
        name="peer_main",
    )(h2, u_b, v_b, *maps)


def _final_kernel(x_ref, p_ref, gp_ref, gs_ref, g_ref, yp_ref, ys_ref, *, n_prompt_tiles):
    i = pl.program_id(0)

    def norm(x):
        return x * lax.rsqrt(jnp.mean(x * x, axis=-1, keepdims=True) + EPS) * g_ref[...]

    @pl.when(i < n_prompt_tiles)
    def _():
        yp_ref[...] = norm(x_ref[...] + gp_ref[0] * p_ref[...])

    @pl.when(i >= n_prompt_tiles)
    def _():
        ys_ref[...] = norm(x_ref[...] + gs_ref[...] * p_ref[...])


def _final(x1, peer_out, mod_p, mod_s, k_gate, g_final, n_prompt, seq):
    n, d = x1.shape
    tm = _pow2_tile(256, seq, n - n_prompt)
    npt = n_prompt // tm
    per_b = seq // tm
    return pl.pallas_call(
        functools.partial(_final_kernel, n_prompt_tiles=npt),
        grid=(n // tm,),
        in_specs=[pl.BlockSpec((tm, d), lambda i: (i, 0)),
                  pl.BlockSpec((tm, d), lambda i: (i, 0)),
                  pl.BlockSpec((1, 1, d), lambda i: (jnp.minimum(i, npt - 1) // per_b, 0, k_gate)),
                  pl.BlockSpec((tm, d), lambda i: (jnp.maximum(i - npt, 0), k_gate)),
                  pl.BlockSpec((1, d), lambda i: (0, 0))],
        out_specs=[pl.BlockSpec((tm, d), lambda i: (jnp.minimum(i, npt - 1), 0)),
                   pl.BlockSpec((tm, d), lambda i: (jnp.maximum(i - npt, 0), 0))],
        out_shape=[jax.ShapeDtypeStruct((n_prompt, d), F32),
                   jax.ShapeDtypeStruct((n - n_prompt, d), F32)],
        compiler_params=_params("arbitrary"),
        name="final_norm",
    )(x1, peer_out, mod_p, mod_s, g_final.reshape(1, d))


def kernel(x_prompt, x_sample, c_prompt, c_sample, state_hgrn, w_ada, b_ada, g_norm1, w_in, hgrn_lb_logits,
           g_hgrn_out, g_gmlp_v, w_spatial, b_spatial, w_branch_a, w_branch_b, w_out, g_norm2, w_peer_q,
           peer_sub_keys, peer_u, peer_v, g_final):
    batch, seq, d = x_prompt.shape
    db, dt, _ = x_sample.shape
    depth, _, heads, dk, dv = state_hgrn.shape
    assert depth == 1 and dk == dv == LANE_V7X
    hw = heads * dk
    groups, gc = w_spatial.shape[1], w_spatial.shape[2]
    gw = w_branch_b.shape[1]
    assert gc == LANE_V7X and gw // groups == LANE_V7X and dt <= gc and gc % dt == 0
    n_prompt, n_sample = batch * seq, db * dt
    n = n_prompt + n_sample

    c_all = jnp.concatenate([c_prompt, c_sample], axis=0)
    pad = (-c_all.shape[0]) % 8
    c_all = jnp.pad(c_all, ((0, pad), (0, 0)))
    mod = _ada(c_all, w_ada[0], b_ada[0])
    mod_p = mod[:batch].reshape(batch, 1, N_MOD * d)
    mod_s = jnp.repeat(mod[batch:batch + db], dt, axis=0)

    x_p = x_prompt.reshape(n_prompt, d)
    x_s = x_sample.reshape(n_sample, d)

    h1 = _norm_mod(x_p, x_s, g_norm1[0], mod_p, mod_s, 1, 0, n_prompt, n_sample, seq)
    proj, peer_u_b = _matmul(h1, w_in[0], F32, n_prompt, "in_proj", side_tables=(peer_u[0],))

    lb = jnp.cumsum(jax.nn.softmax(hgrn_lb_logits.astype(F32), axis=0), axis=0)[0].reshape(1, hw)
    g_ho = g_hgrn_out[0].reshape(1, hw)
    a_p, st_p = _hgrn_prompt(proj, lb, g_ho, batch, seq, heads, dk, n_prompt)
    proj_s = proj[n_prompt:].reshape(db, dt, proj.shape[1])
    a_s, st_s = _hgrn_sample(proj_s, lb, g_ho, state_hgrn, heads, dk)

    tril = jnp.tril(jnp.ones((gc, gc), F32))
    w_sp = w_spatial[0]
    blk = jnp.arange(gc) // dt
    w_samp = jnp.tile(w_sp[:, :dt, :dt], (1, gc // dt, gc // dt)) * (blk[:, None] == blk[None, :])
    w_st = jnp.stack([w_sp * tril, w_samp * tril]).astype(BF16)
    bias_full = jnp.repeat(b_spatial[0].T, gw // groups, axis=1)
    bias_st = jnp.stack([bias_full, jnp.tile(bias_full[:dt], (gc // dt, 1))])
    gu_blk = 4 * hw // gw
    assert gu_blk * gw == 4 * hw
    bm, v_s = _gmlp(proj, w_st, bias_st, g_gmlp_v[0], n_prompt, gu_blk, gu_blk + 1)

    ga_off = 4 * hw + 2 * gw
    mix = _branches(a_p, a_s.reshape(n_sample, hw), bm, w_branch_a[0].astype(BF16), w_branch_b[0].astype(BF16), proj,
                    ga_off, ga_off + d, n_prompt)
    x1, peer_v_b = _out_proj(mix, w_out[0].astype(BF16), x_p, x_s, mod_p, mod_s, 2, n_prompt, seq, peer_v[0])

    h2 = _norm_mod(x1, x1, g_norm2[0], mod_p, mod_s, 4, 3, n_prompt, n_sample, seq)
    qp = _matmul(h2, w_peer_q[0], F32, n_prompt, "peer_query")
    maps = _peer_topk(qp, peer_sub_keys[0].astype(BF16), n_prompt)
    peer_out = _peer_main(h2, peer_u_b, peer_v_b, maps, n_prompt)
    y_p, y_s = _final(x1, peer_out, mod_p, mod_s, 5, g_final, n_prompt, seq)

    state_p = jnp.swapaxes(st_p, -1, -2)
    return (y_p.reshape(batch, seq, d), y_s.reshape(db, dt, d), state_p, st_s,
            v_s.reshape(1, db, dt, gw))
```

```python
import functools
import math

import numpy as np
import jax
import jax.numpy as jnp
from jax import lax
from jax.experimental import pallas as pl
from jax.experimental.pallas import tpu as pltpu

EPS = 1e-6
N_MOD = 6
PEER_TOPK = 16
LANE_V7X = 128
VMEM_LIMIT_V7X = 60 * 1024 * 1024
F32 = jnp.float32
BF16 = jnp.bfloat16
NEG_INF = float("-inf")


def _params(*sem):
    return pltpu.CompilerParams(dimension_semantics=sem, vmem_limit_bytes=VMEM_LIMIT_V7X)


def _pow2_tile(target, *sizes):
    g = 0
    for s in sizes:
        g = math.gcd(g, s)
    t = 1
    while t * 2 <= target and g % (t * 2) == 0:
        t *= 2
    return t


def _silu(x):
    return x * jax.nn.sigmoid(x)


def _ada_kernel(c_ref, w_ref, b_ref, o_ref):
    s = _silu(c_ref[...]).astype(BF16)
    o_ref[...] = jnp.dot(s, w_ref[...].astype(BF16), preferred_element_type=F32) + b_ref[...]


def _ada(c_all, w_ada, b_ada):
    m, d = c_all.shape
    n = w_ada.shape[1]
    tn = _pow2_tile(512, n)
    return pl.pallas_call(
        _ada_kernel,
        grid=(n // tn,),
        in_specs=[pl.BlockSpec((m, d), lambda j: (0, 0)),
                  pl.BlockSpec((d, tn), lambda j: (0, j)),
                  pl.BlockSpec((1, tn), lambda j: (0, j))],
        out_specs=pl.BlockSpec((m, tn), lambda j: (0, j)),
        out_shape=jax.ShapeDtypeStruct((m, n), F32),
        compiler_params=_params("arbitrary"),
        name="ada_mod",
    )(c_all, w_ada, b_ada.reshape(1, n))


def _norm_mod_kernel(xp_ref, xs_ref, g_ref, scp_ref, shp_ref, scs_ref, shs_ref, o_ref, *, n_prompt_tiles):
    i = pl.program_id(0)

    def normed(x):
        return x * lax.rsqrt(jnp.mean(x * x, axis=-1, keepdims=True) + EPS) * g_ref[...]

    @pl.when(i < n_prompt_tiles)
    def _():
        o_ref[...] = (normed(xp_ref[...]) * (1.0 + scp_ref[0]) + shp_ref[0]).astype(o_ref.dtype)

    @pl.when(i >= n_prompt_tiles)
    def _():
        o_ref[...] = (normed(xs_ref[...]) * (1.0 + scs_ref[...]) + shs_ref[...]).astype(o_ref.dtype)


def _row_sources(x_p, x_s, n_prompt, tm):
    npt = n_prompt // tm
    s_off = npt if x_s is x_p else 0
    return (lambda i: (jnp.minimum(i, npt - 1), 0)), (lambda i: (s_off + jnp.maximum(i - npt, 0), 0))


def _norm_mod(x_p, x_s, g, mod_p, mod_s, k_scale, k_shift, n_prompt, n_sample, seq):
    d = x_p.shape[1]
    n = n_prompt + n_sample
    tm = _pow2_tile(256, seq, n_sample)
    npt = n_prompt // tm
    per_b = seq // tm
    p_map, s_map = _row_sources(x_p, x_s, n_prompt, tm)

    def pidx(k):
        return lambda i: (jnp.minimum(i, npt - 1) // per_b, 0, k)

    def sidx(k):
        return lambda i: (jnp.maximum(i - npt, 0), k)

    return pl.pallas_call(
        functools.partial(_norm_mod_kernel, n_prompt_tiles=npt),
        grid=(n // tm,),
        in_specs=[pl.BlockSpec((tm, d), p_map),
                  pl.BlockSpec((tm, d), s_map),
                  pl.BlockSpec((1, d), lambda i: (0, 0)),
                  pl.BlockSpec((1, 1, d), pidx(k_scale)),
                  pl.BlockSpec((1, 1, d), pidx(k_shift)),
                  pl.BlockSpec((tm, d), sidx(k_scale)),
                  pl.BlockSpec((tm, d), sidx(k_shift))],
        out_specs=pl.BlockSpec((tm, d), lambda i: (i, 0)),
        out_shape=jax.ShapeDtypeStruct((n, d), BF16),
        compiler_params=_params("arbitrary"),
        name="norm_mod",
    )(x_p, x_s, g.reshape(1, d), mod_p, mod_p, mod_s, mod_s)


def _mm_kernel(a_ref, w_ref, *rest, n_side):
    side_in, o_ref, side_out, wb_scr = rest[:n_side], rest[n_side], rest[n_side + 1:2 * n_side + 1], rest[-1]

    @pl.when(pl.program_id(1) == 0)
    def _():
        wb_scr[...] = w_ref[...].astype(BF16)

    o_ref[...] = jnp.dot(a_ref[...], wb_scr[...], preferred_element_type=F32).astype(o_ref.dtype)
    for src, dst in zip(side_in, side_out):
        dst[...] = src[...].astype(dst.dtype)


def _side_cast_specs(side_tables, nj, ni):
    specs, shapes = [], []
    for t in side_tables:
        rows = t.shape[0]
        rb = rows // _pow2_tile(nj * ni, rows)
        last = rows // rb - 1
        specs.append(pl.BlockSpec((rb, t.shape[1]), lambda j, i, last=last: (jnp.minimum(j * ni + i, last), 0)))
        shapes.append(jax.ShapeDtypeStruct(t.shape, BF16))
    return specs, shapes


def _matmul(a, w, out_dtype, n_prompt, name, side_tables=()):
    n, k = a.shape
    nc = w.shape[1]
    tm = _pow2_tile(512, n_prompt, n - n_prompt)
    tn = _pow2_tile(1024, nc)
    ni = n // tm
    side_specs, side_shapes = _side_cast_specs(side_tables, nc // tn, ni)
    outs = pl.pallas_call(
        functools.partial(_mm_kernel, n_side=len(side_tables)),
        grid=(nc // tn, ni),
        in_specs=[pl.BlockSpec((tm, k), lambda j, i: (i, 0)),
                  pl.BlockSpec((k, tn), lambda j, i: (0, j))] + side_specs,
        out_specs=[pl.BlockSpec((tm, tn), lambda j, i: (i, j))] + side_specs,
        out_shape=[jax.ShapeDtypeStruct((n, nc), out_dtype)] + side_shapes,
        scratch_shapes=[pltpu.VMEM((k, tn), BF16)],
        compiler_params=_params("arbitrary", "arbitrary"),
        name=name,
    )(a, w, *side_tables)
    return outs[0] if not side_tables else outs


def _hgrn_tables(c):
    nl = int(math.log2(c))
    r = np.arange(c)[:, None]
    j = np.arange(c)[None, :]
    mats = []
    lvl = np.full((c, c), -1, np.int32)
    for l in range(nl):
        m = c >> (l + 1)
        mid = (r // (2 * m)) * 2 * m + m
        upper = r >= mid
        t = np.where(upper, (j >= mid) & (j <= r), (j > r) & (j <= mid - 1))
        mats.append(t)
        same = (r // (2 * m)) == (j // (2 * m))
        lvl = np.where(same & upper & (j < mid), l, lvl)
    lvl = np.where(r == j, nl, lvl)
    mats.append(j <= r)
    mats.append(j > r)
    t_all = np.concatenate(mats, axis=0).astype(np.float32)
    t_all = np.concatenate([t_all, t_all], axis=1)
    return jnp.asarray(t_all, BF16), jnp.asarray(lvl), nl


def _hgrn_prompt_kernel(q_ref, f_ref, v_ref, og_ref, lb_ref, g_ref, tall_ref, lvl_ref,
                        a_ref, st_ref, s_scr, *, chunk, n_levels, n_chunks, heads_per_step, dk):
    c = chunk
    s_scr[...] = jnp.zeros_like(s_scr)
    lb = lb_ref[...]
    lvl = lvl_ref[...]
    rows = lax.broadcasted_iota(jnp.int32, (c, q_ref.shape[1]), 0)
    nt = (((1,), (1,)), ((), ()))

    def body(ci, carry):
        sl = pl.ds(pl.multiple_of(ci * c, c), c)
        qs = _silu(q_ref[sl, :])
        f = lb + (1.0 - lb) * jax.nn.sigmoid(f_ref[sl, :])
        k = 1.0 - f
        lf = jnp.log(f)
        v = v_ref[sl, :].astype(BF16)
        lf_hi = lf.astype(BF16)
        lf_lo = (lf - lf_hi.astype(F32)).astype(BF16)
        ex = jnp.dot(tall_ref[...], jnp.concatenate([lf_hi, lf_lo], axis=0), preferred_element_type=F32)
        qs_b = qs.astype(BF16)
        k_b = k.astype(BF16)
        ys = []
        for l in range(n_levels):
            m = c >> (l + 1)
            e_l = jnp.exp(ex[l * c:(l + 1) * c, :])
            ys.append((e_l * jnp.where((rows & m) != 0, qs, k)).astype(BF16))
        bcum = ex[n_levels * c:(n_levels + 1) * c, :]
        brev = ex[(n_levels + 1) * c:(n_levels + 2) * c, :]
        qh = (qs * jnp.exp(bcum)).astype(BF16)
        kh = (k * jnp.exp(brev)).astype(BF16)
        decay = jnp.exp(bcum[c - 1:c, :])
        outs = []
        for hh in range(heads_per_step):
            hs = slice(hh * dk, (hh + 1) * dk)
            scores = jnp.where(lvl == n_levels,
                               lax.dot_general(qs_b[:, hs], k_b[:, hs], nt, preferred_element_type=F32), 0.0)
            for l in range(n_levels):
                y = ys[l][:, hs]
                p = lax.dot_general(y, y, nt, preferred_element_type=F32)
                scores = scores + jnp.where(lvl == l, p, 0.0)
            st = s_scr[hh]
            o = jnp.dot(scores.astype(BF16), v[:, hs], preferred_element_type=F32)
            o = o + lax.dot_general(qh[:, hs], st.astype(BF16), nt, preferred_element_type=F32)
            kv_t = lax.dot_general(v[:, hs], kh[:, hs], (((0,), (0,)), ((), ())), preferred_element_type=F32)
            s_scr[hh] = st * decay[:, hs] + kv_t
            outs.append(o * lax.rsqrt(jnp.mean(o * o, axis=-1, keepdims=True) + EPS))
        on = jnp.concatenate(outs, axis=1) if heads_per_step > 1 else outs[0]
        a_ref[sl, :] = (on * g_ref[...] * _silu(og_ref[sl, :])).astype(a_ref.dtype)
        return carry

    lax.fori_loop(0, n_chunks, body, 0, unroll=2)
    st_ref[0, 0] = s_scr[...]


def _hgrn_prompt(proj, lb, g_out, batch, seq, heads, dk, n_rows):
    c = _pow2_tile(128, seq)
    hp = _pow2_tile(4, heads)
    t_all, lvl, nl = _hgrn_tables(c)
    w = hp * dk
    nh = heads // hp
    col = lambda off: (lambda b, h: (b, off + h))
    kern = functools.partial(_hgrn_prompt_kernel, chunk=c, n_levels=nl, n_chunks=seq // c,
                             heads_per_step=hp, dk=dk)
    return pl.pallas_call(
        kern,
        grid=(batch, nh),
        in_specs=[pl.BlockSpec((seq, w), col(0)),
                  pl.BlockSpec((seq, w), col(nh)),
                  pl.BlockSpec((seq, w), col(2 * nh)),
                  pl.BlockSpec((seq, w), col(3 * nh)),
                  pl.BlockSpec((1, w), lambda b, h: (0, h)),
                  pl.BlockSpec((1, w), lambda b, h: (0, h)),
                  pl.BlockSpec(t_all.shape, lambda b, h: (0, 0)),
                  pl.BlockSpec(lvl.shape, lambda b, h: (0, 0))],
        out_specs=[pl.BlockSpec((seq, w), lambda b, h: (b, h)),
                   pl.BlockSpec((1, 1, hp, dk, dk), lambda b, h: (0, b, h, 0, 0))],
        out_shape=[jax.ShapeDtypeStruct((n_rows, heads * dk), BF16),
                   jax.ShapeDtypeStruct((1, batch, heads, dk, dk), F32)],
        scratch_shapes=[pltpu.VMEM((hp, dk, dk), F32)],
        compiler_params=_params("arbitrary", "arbitrary"),
        name="hgrn_prompt",
    )(proj, proj, proj, proj, lb, g_out, t_all, lvl)


def _hgrn_sample_kernel(q_ref, f_ref, v_ref, og_ref, lb_ref, g_ref, s0_ref, a_ref, s_ref, *, bt, steps):
    lb = lb_ref[...]
    g = g_ref[...]
    rows = lax.broadcasted_iota(jnp.int32, (steps, q_ref.shape[2]), 0)

    def body(b, carry):
        qs = _silu(q_ref[b])
        f = lb + (1.0 - lb) * jax.nn.sigmoid(f_ref[b])
        k = 1.0 - f
        lf = jnp.log(f)
        v = v_ref[b]
        cum = []
        run = None
        for t in range(steps):
            run = lf[t:t + 1, :] if run is None else run + lf[t:t + 1, :]
            cum.append(run)
        bmat = jnp.broadcast_to(cum[0], lf.shape)
        for t in range(1, steps):
            bmat = jnp.where(rows == t, cum[t], bmat)
        o = jnp.zeros_like(qs)
        for s in range(steps):
            x = jnp.where(rows >= s, qs * k[s:s + 1, :] * jnp.exp(jnp.minimum(bmat - cum[s], 0.0)), 0.0)
            o = o + jnp.sum(x, axis=-1, keepdims=True) * v[s:s + 1, :]
        s0 = s0_ref[0, b, 0]
        o = o + jnp.dot((qs * jnp.exp(bmat)).astype(BF16), s0.astype(BF16), preferred_element_type=F32)
        kh = (k * jnp.exp(cum[steps - 1] - bmat)).astype(BF16)
        kv = lax.dot_general(kh, v.astype(BF16), (((0,), (0,)), ((), ())), preferred_element_type=F32)
        decay_col = jnp.broadcast_to(jnp.exp(cum[steps - 1]), s0.shape).T
        s_ref[0, b, 0] = s0 * decay_col + kv
        on = o * lax.rsqrt(jnp.mean(o * o, axis=-1, keepdims=True) + EPS)
        a_ref[b] = on * g * _silu(og_ref[b])
        return carry

    lax.fori_loop(0, bt, body, 0, unroll=8)


def _hgrn_sample(proj_s, lb, g_out, s0, heads, dk):
    db, steps, _ = proj_s.shape
    bt = _pow2_tile(32, db)
    col = lambda off: (lambda j, h: (j, 0, off + h))
    kern = functools.partial(_hgrn_sample_kernel, bt=bt, steps=steps)
    return pl.pallas_call(
        kern,
        grid=(db // bt, heads),
        in_specs=[pl.BlockSpec((bt, steps, dk), col(0)),
                  pl.BlockSpec((bt, steps, dk), col(heads)),
                  pl.BlockSpec((bt, steps, dk), col(2 * heads)),
                  pl.BlockSpec((bt, steps, dk), col(3 * heads)),
                  pl.BlockSpec((1, dk), lambda j, h: (0, h)),
                  pl.BlockSpec((1, dk), lambda j, h: (0, h)),
                  pl.BlockSpec((1, bt, 1, dk, dk), lambda j, h: (0, j, h, 0, 0))],
        out_specs=[pl.BlockSpec((bt, steps, dk), lambda j, h: (j, 0, h)),
                   pl.BlockSpec((1, bt, 1, dk, dk), lambda j, h: (0, j, h, 0, 0))],
        out_shape=[jax.ShapeDtypeStruct((db, steps, heads * dk), F32),
                   jax.ShapeDtypeStruct(s0.shape, F32)],
        compiler_params=_params("arbitrary", "arbitrary"),
        name="hgrn_sample",
    )(proj_s, proj_s, proj_s, proj_s, lb, g_out, s0)


def _gmlp_kernel(gu_ref, gv_ref, w_ref, bias_ref, g_ref, o_ref, v_ref, *, groups, gd):
    u = jax.nn.gelu(gu_ref[...])
    vv = jax.nn.gelu(gv_ref[...])
    r = lax.rsqrt(jnp.mean(vv * vv, axis=-1, keepdims=True) + EPS)
    v = vv * r * g_ref[...]
    v_ref[...] = v
    for gi in range(groups):
        sl = slice(gi * gd, (gi + 1) * gd)
        z = jnp.dot(w_ref[0, gi], v[:, sl].astype(BF16), preferred_element_type=F32) + bias_ref[0, :, sl]
        o_ref[:, sl] = (u[:, sl] * z).astype(o_ref.dtype)


def _gmlp(proj, w_st, bias_st, g_v, n_prompt, gu_blk, gv_blk):
    n = proj.shape[0]
    _, groups, c, _ = w_st.shape
    gw = bias_st.shape[2]
    npc = n_prompt // c
    sel = lambda i: jnp.where(i >= npc, 1, 0)
    kern = functools.partial(_gmlp_kernel, groups=groups, gd=gw // groups)
    return pl.pallas_call(
        kern,
        grid=(n // c,),
        in_specs=[pl.BlockSpec((c, gw), lambda i: (i, gu_blk)),
                  pl.BlockSpec((c, gw), lambda i: (i, gv_blk)),
                  pl.BlockSpec((1, groups, c, c), lambda i: (sel(i), 0, 0, 0)),
                  pl.BlockSpec((1, c, gw), lambda i: (sel(i), 0, 0)),
                  pl.BlockSpec((1, gw), lambda i: (0, 0))],
        out_specs=[pl.BlockSpec((c, gw), lambda i: (i, 0)),
                   pl.BlockSpec((c, gw), lambda i: (jnp.maximum(i - npc, 0), 0))],
        out_shape=[jax.ShapeDtypeStruct((n, gw), BF16),
                   jax.ShapeDtypeStruct((n - n_prompt, gw), F32)],
        compiler_params=_params("arbitrary"),
        name="chunk_mlp",
    )(proj, proj, w_st, bias_st, g_v.reshape(1, gw))


def _branch_kernel(ap_ref, as_ref, b_ref, wa_ref, wb_ref, ga_ref, gb_ref, o_ref, *, n_prompt_tiles):
    a = jnp.where(pl.program_id(1) < n_prompt_tiles, ap_ref[...], as_ref[...].astype(BF16))
    ya = jnp.dot(a, wa_ref[...], preferred_element_type=F32)
    yb = jnp.dot(b_ref[...], wb_ref[...], preferred_element_type=F32)
    o_ref[...] = (jax.nn.sigmoid(ga_ref[...]) * ya + jax.nn.sigmoid(gb_ref[...]) * yb).astype(o_ref.dtype)


def _branches(a_p, a_s, bm, wa, wb, proj, ga_off, gb_off, n_prompt):
    n, kb = bm.shape
    ka = a_p.shape[1]
    d = wa.shape[1]
    tm = _pow2_tile(512, n_prompt, n - n_prompt)
    tn = _pow2_tile(1024, d, ga_off, gb_off)
    npt = n_prompt // tm
    return pl.pallas_call(
        functools.partial(_branch_kernel, n_prompt_tiles=npt),
        grid=(d // tn, n // tm),
        in_specs=[pl.BlockSpec((tm, ka), lambda j, i: (jnp.minimum(i, npt - 1), 0)),
                  pl.BlockSpec((tm, ka), lambda j, i: (jnp.maximum(i - npt, 0), 0)),
                  pl.BlockSpec((tm, kb), lambda j, i: (i, 0)),
                  pl.BlockSpec((ka, tn), lambda j, i: (0, j)),
                  pl.BlockSpec((kb, tn), lambda j, i: (0, j)),
                  pl.BlockSpec((tm, tn), lambda j, i: (i, ga_off // tn + j)),
                  pl.BlockSpec((tm, tn), lambda j, i: (i, gb_off // tn + j))],
        out_specs=pl.BlockSpec((tm, tn), lambda j, i: (i, j)),
        out_shape=jax.ShapeDtypeStruct((n, d), BF16),
        compiler_params=_params("arbitrary", "arbitrary"),
        name="branches",
    )(a_p, a_s, bm, wa, wb, proj, proj)


def _out_kernel(m_ref, w_ref, xp_ref, xs_ref, gp_ref, gs_ref, side_ref, o_ref, side_out_ref, *, n_prompt_tiles):
    i = pl.program_id(1)
    y = jnp.dot(m_ref[...], w_ref[...], preferred_element_type=F32)
    side_out_ref[...] = side_ref[...].astype(side_out_ref.dtype)

    @pl.when(i < n_prompt_tiles)
    def _():
        o_ref[...] = xp_ref[...] + gp_ref[0] * y

    @pl.when(i >= n_prompt_tiles)
    def _():
        o_ref[...] = xs_ref[...] + gs_ref[...] * y


def _out_proj(mix, w, x_p, x_s, mod_p, mod_s, k_gate, n_prompt, seq, side_table):
    n, d = mix.shape
    tm = _pow2_tile(512, seq, n - n_prompt)
    tn = _pow2_tile(1024, d)
    npt = n_prompt // tm
    per_b = seq // tm
    nj = d // tn
    side_specs, side_shapes = _side_cast_specs((side_table,), nj, n // tm)
    return pl.pallas_call(
        functools.partial(_out_kernel, n_prompt_tiles=npt),
        grid=(nj, n // tm),
        in_specs=[pl.BlockSpec((tm, d), lambda j, i: (i, 0)),
                  pl.BlockSpec((d, tn), lambda j, i: (0, j)),
                  pl.BlockSpec((tm, tn), lambda j, i: (jnp.minimum(i, npt - 1), j)),
                  pl.BlockSpec((tm, tn), lambda j, i: (jnp.maximum(i - npt, 0), j)),
                  pl.BlockSpec((1, 1, tn), lambda j, i: (jnp.minimum(i, npt - 1) // per_b, 0, k_gate * nj + j)),
                  pl.BlockSpec((tm, tn), lambda j, i: (jnp.maximum(i - npt, 0), k_gate * nj + j))] + side_specs,
        out_specs=[pl.BlockSpec((tm, tn), lambda j, i: (i, j))] + side_specs,
        out_shape=[jax.ShapeDtypeStruct((n, d), F32)] + side_shapes,
        compiler_params=_params("arbitrary", "arbitrary"),
        name="out_proj",
    )(mix, w, x_p, x_s, mod_p, mod_s, side_table)


def _cand_pairs(k):
    return [(i, j) for i in range(k) for j in range(k) if (i + 1) * (j + 1) <= k]


def _arg_rounds(s, iota, k, exact):
    rank = jnp.full(s.shape, float(k), F32)
    vals = []
    big = float(s.shape[0])
    for i in range(k):
        m = jnp.max(s, axis=0, keepdims=True)
        if exact:
            idx = jnp.min(jnp.where(s == m, iota, big), axis=0, keepdims=True)
            sel = iota == idx
        else:
            sel = s == m
        rank = jnp.where(sel, float(i), rank)
        s = jnp.where(sel, NEG_INF, s)
        vals.append(m)
    return vals, rank


def _peer_gates(s1, s2, jsum, cand_scr, cols, pairs, k, exact):
    n_iota = lax.broadcasted_iota(jnp.int32, s1.shape, 0).astype(F32)
    a, r1 = _arg_rounds(s1, n_iota, k, exact)
    b, r2 = _arg_rounds(s2, n_iota, k, exact)

    n_cand = cand_scr.shape[0]
    cand_scr[len(pairs):, cols] = jnp.full((n_cand - len(pairs), s1.shape[1]), NEG_INF, F32)
    for p, (i, j) in enumerate(pairs):
        cand_scr[p:p + 1, cols] = a[i] + b[j]
    c0 = cand_scr[:, cols]
    p_iota = lax.broadcasted_iota(jnp.int32, c0.shape, 0).astype(F32)
    _, prank = _arg_rounds(c0, p_iota, k, exact)
    taken = prank < float(k)
    w = jnp.where(taken, jnp.exp(c0 - (a[0] + b[0])), 0.0)
    inv_z = 1.0 / jnp.sum(w, axis=0, keepdims=True)
    ones = jnp.where(taken, 1.0, 0.0)
    jcnt = jnp.dot(jsum, ones.astype(BF16), preferred_element_type=F32)

    eaz = jnp.zeros(s1.shape, F32)
    jd = jnp.zeros(s1.shape, F32)
    eb = jnp.zeros(s1.shape, F32)
    for i in range(k):
        hit1 = r1 == float(i)
        eaz = jnp.where(hit1, jnp.exp(a[i] - a[0]) * inv_z, eaz)
        jd = jnp.where(hit1, jcnt[i:i + 1, :], jd)
        eb = jnp.where(r2 == float(i), jnp.exp(b[i] - b[0]), eb)

    def excess(rank):
        return jnp.abs(jnp.sum(jnp.where(rank < float(k), 1.0, 0.0), axis=0, keepdims=True) - float(k))

    return eaz, jd, eb, r2, excess(r1) + excess(r2) + excess(prank)


def _peer_topk_kernel(q_ref, sk_ref, jsum_ref, eaz_ref, jd_ref, eb_ref, r2_ref, cand_scr, *, half, pairs, k):
    nt = (((1,), (1,)), ((), ()))
    q = q_ref[...].astype(BF16)
    s1 = lax.dot_general(sk_ref[0, 0], q[:, :half], nt, preferred_element_type=F32)
    s2 = lax.dot_general(sk_ref[0, 1], q[:, half:], nt, preferred_element_type=F32)
    jsum = jsum_ref[...]
    n_blocks = s1.shape[1] // LANE_V7X

    def run(exact):
        bad = None
        for cb in range(n_blocks):
            cols = slice(cb * LANE_V7X, (cb + 1) * LANE_V7X)
            eaz, jd, eb, r2, dev = _peer_gates(s1[:, cols], s2[:, cols], jsum, cand_scr, cols, pairs, k, exact)
            eaz_ref[0, cb] = eaz
            jd_ref[0, cb] = jd
            eb_ref[0, cb] = eb.astype(eb_ref.dtype)
            r2_ref[0, cb] = r2.astype(r2_ref.dtype)
            bad = dev if bad is None else bad + dev
        return jnp.max(bad)

    miscount = run(exact=False)

    @pl.when(miscount > 0.5)
    def _():
        run(exact=True)


def _peer_topk(qp, sub_keys_b, n_prompt):
    n = qp.shape[0]
    heads, _, n_keys, half = sub_keys_b.shape
    tt = _pow2_tile(256, n_prompt, n - n_prompt)
    pairs = _cand_pairs(PEER_TOPK)
    n_cand = -(-len(pairs) // 64) * 64
    jsum = np.zeros((PEER_TOPK, n_cand), np.float32)
    for p, (i, _) in enumerate(pairs):
        jsum[i, p] = 1.0
    kern = functools.partial(_peer_topk_kernel, half=half, pairs=pairs, k=PEER_TOPK)
    maps = jax.ShapeDtypeStruct((heads, n // LANE_V7X, n_keys, LANE_V7X), F32)
    maps_b = jax.ShapeDtypeStruct((heads, n // LANE_V7X, n_keys, LANE_V7X), BF16)
    mspec = pl.BlockSpec((1, tt // LANE_V7X, n_keys, LANE_V7X), lambda i, h: (h, i, 0, 0))
    return pl.pallas_call(
        kern,
        grid=(n // tt, heads),
        in_specs=[pl.BlockSpec((tt, 2 * half), lambda i, h: (i, h)),
                  pl.BlockSpec((1, 2, n_keys, half), lambda i, h: (h, 0, 0, 0)),
                  pl.BlockSpec(jsum.shape, lambda i, h: (0, 0))],
        out_specs=[mspec, mspec, mspec, mspec],
        out_shape=[maps, maps, maps_b, maps_b],
        scratch_shapes=[pltpu.VMEM((n_cand, tt), F32)],
        compiler_params=_params("arbitrary", "arbitrary"),
        name="peer_topk",
    )(qp, sub_keys_b, jnp.asarray(jsum, BF16))


def _peer_main_kernel(h_ref, u_ref, v_ref, eaz_ref, jd_ref, eb_ref, r2_ref, o_ref,
                      ga0_scr, ga1_scr, g0_scr, g1_scr, ht_scr, *, heads, n_keys, rows_per_step, n_steps):
    e = pl.program_id(1)
    n_tb = h_ref.shape[0] // LANE_V7X
    blocks = [(r, tb) for r in range(rows_per_step) for tb in range(n_tb)]
    n_slices = math.gcd(len(blocks), 16)

    def gates(chunk, g_scr, part=None):
        for r, tb in (blocks if part is None else blocks[part::n_slices]):
            n1 = chunk * rows_per_step + r
            g = None
            for h in range(heads):
                ea = eaz_ref[h, tb, pl.ds(n1, 1), :].astype(BF16)
                jd = jd_ref[h, tb, pl.ds(n1, 1), :].astype(BF16)
                t = ea * jnp.where(r2_ref[h, tb] < jd, eb_ref[h, tb], jnp.zeros((), BF16))
                g = t if g is None else g + t
            g_scr[r * n_tb + tb] = g

    @pl.when(e == 0)
    def _():
        o_ref[...] = jnp.zeros_like(o_ref)
        ga1_scr[...] = jnp.zeros_like(ga1_scr)
        gates(0, g0_scr)
        ht_scr[...] = h_ref[...].T

    def stage(ga_cur, ga_prev, g_cur, g_next):
        act_t = jnp.dot(u_ref[...], ht_scr[...], preferred_element_type=F32)
        dq = o_ref.shape[1] // n_slices
        for q in range(n_slices):
            qs = slice(q * dq, (q + 1) * dq)
            o_ref[:, qs] += jnp.dot(ga_prev[...], v_ref[:, qs], preferred_element_type=F32)
            for r, tb in blocks[q::n_slices]:
                sl = slice(r * n_keys, (r + 1) * n_keys)
                ts = slice(tb * LANE_V7X, (tb + 1) * LANE_V7X)
                ga_cur[ts, sl] = (g_cur[r * n_tb + tb] * jax.nn.gelu(act_t[sl, ts].astype(BF16))).T
            gates(jnp.minimum(e + 1, n_steps - 1), g_next, part=q)

    @pl.when(e % 2 == 0)
    def _():
        stage(ga0_scr, ga1_scr, g0_scr, g1_scr)

    @pl.when(e % 2 == 1)
    def _():
        stage(ga1_scr, ga0_scr, g1_scr, g0_scr)


def _peer_main(h2, u_b, v_b, maps, n_prompt):
    n, d = h2.shape
    n_exp = u_b.shape[0]
    heads, _, n_keys, lane = maps[0].shape
    tt = _pow2_tile(512, n_prompt, n - n_prompt)
    ec = _pow2_tile(512, n_exp)
    rps = ec // n_keys
    n_steps = n_exp // ec
    kern = functools.partial(_peer_main_kernel, heads=heads, n_keys=n_keys, rows_per_step=rps, n_steps=n_steps)
    once = pl.Buffered(1)
    mspec = pl.BlockSpec((heads, tt // lane, n_keys, lane), lambda i, e: (0, i, 0, 0), pipeline_mode=once)
    return pl.pallas_call(
        kern,
        grid=(n // tt, n_steps + 1),
        in_specs=[pl.BlockSpec((tt, d), lambda i, e: (i, 0), pipeline_mode=once),
                  pl.BlockSpec((ec, d), lambda i, e: (jnp.minimum(e, n_steps - 1), 0)),
                  pl.BlockSpec((ec, d), lambda i, e: (jnp.maximum(e - 1, 0), 0)),
                  mspec, mspec, mspec, mspec],
        out_specs=pl.BlockSpec((tt, d), lambda i, e: (i, 0), pipeline_mode=once),
        out_shape=jax.ShapeDtypeStruct((n, d), F32),
        scratch_shapes=[pltpu.VMEM((tt, ec), BF16), pltpu.VMEM((tt, ec), BF16),
                        pltpu.VMEM((rps * tt // lane, n_keys, lane), BF16),
                        pltpu.VMEM((rps * tt // lane, n_keys, lane), BF16),
                        pltpu.VMEM((d, tt), BF16)],
        compiler_params=_params("arbitrary", "arbitrary"),
        name="peer_main",
    )(h2, u_b, v_b, *maps)


def _final_kernel(x_ref, p_ref, gp_ref, gs_ref, g_ref, yp_ref, ys_ref, *, n_prompt_tiles):
    i = pl.program_id(0)

    def norm(x):
        return x * lax.rsqrt(jnp.mean(x * x, axis=-1, keepdims=True) + EPS) * g_ref[...]

    @pl.when(i < n_prompt_tiles)
    def _():
        yp_ref[...] = norm(x_ref[...] + gp_ref[0] * p_ref[...])

    @pl.when(i >= n_prompt_tiles)
    def _():
        ys_ref[...] = norm(x_ref[...] + gs_ref[...] * p_ref[...])


def _final(x1, peer_out, mod_p, mod_s, k_gate, g_final, n_prompt, seq):
    n, d = x1.shape
    tm = _pow2_tile(256, seq, n - n_prompt)
    npt = n_prompt // tm
    per_b = seq // tm
    return pl.pallas_call(
        functools.partial(_final_kernel, n_prompt_tiles=npt),
        grid=(n // tm,),
        in_specs=[pl.BlockSpec((tm, d), lambda i: (i, 0)),
                  pl.BlockSpec((tm, d), lambda i: (i, 0)),
                  pl.BlockSpec((1, 1, d), lambda i: (jnp.minimum(i, npt - 1) // per_b, 0, k_gate)),
                  pl.BlockSpec((tm, d), lambda i: (jnp.maximum(i - npt, 0), k_gate)),
                  pl.BlockSpec((1, d), lambda i: (0, 0))],
        out_specs=[pl.BlockSpec((tm, d), lambda i: (jnp.minimum(i, npt - 1), 0)),
                   pl.BlockSpec((tm, d), lambda i: (jnp.maximum(i - npt, 0), 0))],
        out_shape=[jax.ShapeDtypeStruct((n_prompt, d), F32),
                   jax.ShapeDtypeStruct((n - n_prompt, d), F32)],
        compiler_params=_params("arbitrary"),
        name="final_norm",
    )(x1, peer_out, mod_p, mod_s, g_final.reshape(1, d))


def kernel(x_prompt, x_sample, c_prompt, c_sample, state_hgrn, w_ada, b_ada, g_norm1, w_in, hgrn_lb_logits,
           g_hgrn_out, g_gmlp_v, w_spatial, b_spatial, w_branch_a, w_branch_b, w_out, g_norm2, w_peer_q,
           peer_sub_keys, peer_u, peer_v, g_final):
    batch, seq, d = x_prompt.shape
    db, dt, _ = x_sample.shape
    depth, _, heads, dk, dv = state_hgrn.shape
    assert depth == 1 and dk == dv == LANE_V7X
    hw = heads * dk
    groups, gc = w_spatial.shape[1], w_spatial.shape[2]
    gw = w_branch_b.shape[1]
    assert gc == LANE_V7X and gw // groups == LANE_V7X and dt <= gc and gc % dt == 0
    n_prompt, n_sample = batch * seq, db * dt
    n = n_prompt + n_sample

    c_all = jnp.concatenate([c_prompt, c_sample], axis=0)
    pad = (-c_all.shape[0]) % 8
    c_all = jnp.pad(c_all, ((0, pad), (0, 0)))
    mod = _ada(c_all, w_ada[0], b_ada[0])
    mod_p = mod[:batch].reshape(batch, 1, N_MOD * d)
    mod_s = jnp.repeat(mod[batch:batch + db], dt, axis=0)

    x_p = x_prompt.reshape(n_prompt, d)
    x_s = x_sample.reshape(n_sample, d)

    h1 = _norm_mod(x_p, x_s, g_norm1[0], mod_p, mod_s, 1, 0, n_prompt, n_sample, seq)
    proj, peer_u_b = _matmul(h1, w_in[0], F32, n_prompt, "in_proj", side_tables=(peer_u[0],))

    lb = jnp.cumsum(jax.nn.softmax(hgrn_lb_logits.astype(F32), axis=0), axis=0)[0].reshape(1, hw)
    g_ho = g_hgrn_out[0].reshape(1, hw)
    a_p, st_p = _hgrn_prompt(proj, lb, g_ho, batch, seq, heads, dk, n_prompt)
    proj_s = proj[n_prompt:].reshape(db, dt, proj.shape[1])
    a_s, st_s = _hgrn_sample(proj_s, lb, g_ho, state_hgrn, heads, dk)

    tril = jnp.tril(jnp.ones((gc, gc), F32))
    w_sp = w_spatial[0]
    blk = jnp.arange(gc) // dt
    w_samp = jnp.tile(w_sp[:, :dt, :dt], (1, gc // dt, gc // dt)) * (blk[:, None] == blk[None, :])
    w_st = jnp.stack([w_sp * tril, w_samp * tril]).astype(BF16)
    bias_full = jnp.repeat(b_spatial[0].T, gw // groups, axis=1)
    bias_st = jnp.stack([bias_full, jnp.tile(bias_full[:dt], (gc // dt, 1))])
    gu_blk = 4 * hw // gw
    assert gu_blk * gw == 4 * hw
    bm, v_s = _gmlp(proj, w_st, bias_st, g_gmlp_v[0], n_prompt, gu_blk, gu_blk + 1)

    ga_off = 4 * hw + 2 * gw
    mix = _branches(a_p, a_s.reshape(n_sample, hw), bm, w_branch_a[0].astype(BF16), w_branch_b[0].astype(BF16), proj,
                    ga_off, ga_off + d, n_prompt)
    x1, peer_v_b = _out_proj(mix, w_out[0].astype(BF16), x_p, x_s, mod_p, mod_s, 2, n_prompt, seq, peer_v[0])

    h2 = _norm_mod(x1, x1, g_norm2[0], mod_p, mod_s, 4, 3, n_prompt, n_sample, seq)
    qp = _matmul(h2, w_peer_q[0], F32, n_prompt, "peer_query")
    maps = _peer_topk(qp, peer_sub_keys[0].astype(BF16), n_prompt)
    peer_out = _peer_main(h2, peer_u_b, peer_v_b, maps, n_prompt)
    y_p, y_s = _final(x1, peer_out, mod_p, mod_s, 5, g_final, n_prompt, seq)

    state_p = jnp.swapaxes(st_p, -1, -2)
    return (y_p.reshape(batch, seq, d), y_s.reshape(db, dt, d), state_p, st_s,
            v_s.reshape(1, db, dt, gw))
```

```python
import functools
import math

import numpy as np
import jax
import jax.numpy as jnp
from jax import lax
from jax.experimental import pallas as pl
from jax.experimental.pallas import tpu as pltpu

EPS = 1e-6
N_MOD = 6
PEER_TOPK = 16
LANE_V7X = 128
VMEM_LIMIT_V7X = 60 * 1024 * 1024
F32 = jnp.float32
BF16 = jnp.bfloat16
NEG_INF = float("-inf")


def _params(*sem):
    return pltpu.CompilerParams(dimension_semantics=sem, vmem_limit_bytes=VMEM_LIMIT_V7X)


def _pow2_tile(target, *sizes):
    g = 0
    for s in sizes:
        g = math.gcd(g, s)
    t = 1
    while t * 2 <= target and g % (t * 2) == 0:
        t *= 2
    return t


def _silu(x):
    return x * jax.nn.sigmoid(x)


def _ada_kernel(c_ref, w_ref, b_ref, o_ref):
    s = _silu(c_ref[...]).astype(BF16)
    o_ref[...] = jnp.dot(s, w_ref[...].astype(BF16), preferred_element_type=F32) + b_ref[...]


def _ada(c_all, w_ada, b_ada):
    m, d = c_all.shape
    n = w_ada.shape[1]
    tn = _pow2_tile(512, n)
    return pl.pallas_call(
        _ada_kernel,
        grid=(n // tn,),
        in_specs=[pl.BlockSpec((m, d), lambda j: (0, 0)),
                  pl.BlockSpec((d, tn), lambda j: (0, j)),
                  pl.BlockSpec((1, tn), lambda j: (0, j))],
        out_specs=pl.BlockSpec((m, tn), lambda j: (0, j)),
        out_shape=jax.ShapeDtypeStruct((m, n), F32),
        compiler_params=_params("arbitrary"),
        name="ada_mod",
    )(c_all, w_ada, b_ada.reshape(1, n))


def _norm_mod_kernel(xp_ref, xs_ref, g_ref, scp_ref, shp_ref, scs_ref, shs_ref, o_ref, *, n_prompt_tiles):
    i = pl.program_id(0)

    def normed(x):
        return x * lax.rsqrt(jnp.mean(x * x, axis=-1, keepdims=True) + EPS) * g_ref[...]

    @pl.when(i < n_prompt_tiles)
    def _():
        o_ref[...] = (normed(xp_ref[...]) * (1.0 + scp_ref[0]) + shp_ref[0]).astype(o_ref.dtype)

    @pl.when(i >= n_prompt_tiles)
    def _():
        o_ref[...] = (normed(xs_ref[...]) * (1.0 + scs_ref[...]) + shs_ref[...]).astype(o_ref.dtype)


def _row_sources(x_p, x_s, n_prompt, tm):
    npt = n_prompt // tm
    s_off = npt if x_s is x_p else 0
    return (lambda i: (jnp.minimum(i, npt - 1), 0)), (lambda i: (s_off + jnp.maximum(i - npt, 0), 0))


def _norm_mod(x_p, x_s, g, mod_p, mod_s, k_scale, k_shift, n_prompt, n_sample, seq):
    d = x_p.shape[1]
    n = n_prompt + n_sample
    tm = _pow2_tile(256, seq, n_sample)
    npt = n_prompt // tm
    per_b = seq // tm
    p_map, s_map = _row_sources(x_p, x_s, n_prompt, tm)

    def pidx(k):
        return lambda i: (jnp.minimum(i, npt - 1) // per_b, 0, k)

    def sidx(k):
        return lambda i: (jnp.maximum(i - npt, 0), k)

    return pl.pallas_call(
        functools.partial(_norm_mod_kernel, n_prompt_tiles=npt),
        grid=(n // tm,),
        in_specs=[pl.BlockSpec((tm, d), p_map),
                  pl.BlockSpec((tm, d), s_map),
                  pl.BlockSpec((1, d), lambda i: (0, 0)),
                  pl.BlockSpec((1, 1, d), pidx(k_scale)),
                  pl.BlockSpec((1, 1, d), pidx(k_shift)),
                  pl.BlockSpec((tm, d), sidx(k_scale)),
                  pl.BlockSpec((tm, d), sidx(k_shift))],
        out_specs=pl.BlockSpec((tm, d), lambda i: (i, 0)),
        out_shape=jax.ShapeDtypeStruct((n, d), BF16),
        compiler_params=_params("arbitrary"),
        name="norm_mod",
    )(x_p, x_s, g.reshape(1, d), mod_p, mod_p, mod_s, mod_s)


def _mm_kernel(a_ref, w_ref, *rest, n_side):
    side_in, o_ref, side_out, wb_scr = rest[:n_side], rest[n_side], rest[n_side + 1:2 * n_side + 1], rest[-1]

    @pl.when(pl.program_id(1) == 0)
    def _():
        wb_scr[...] = w_ref[...].astype(BF16)

    o_ref[...] = jnp.dot(a_ref[...], wb_scr[...], preferred_element_type=F32).astype(o_ref.dtype)
    for src, dst in zip(side_in, side_out):
        dst[...] = src[...].astype(dst.dtype)


def _side_cast_specs(side_tables, nj, ni):
    specs, shapes = [], []
    for t in side_tables:
        rows = t.shape[0]
        rb = rows // _pow2_tile(nj * ni, rows)
        last = rows // rb - 1
        specs.append(pl.BlockSpec((rb, t.shape[1]), lambda j, i, last=last: (jnp.minimum(j * ni + i, last), 0)))
        shapes.append(jax.ShapeDtypeStruct(t.shape, BF16))
    return specs, shapes


def _matmul(a, w, out_dtype, n_prompt, name, side_tables=()):
    n, k = a.shape
    nc = w.shape[1]
    tm = _pow2_tile(512, n_prompt, n - n_prompt)
    tn = _pow2_tile(1024, nc)
    ni = n // tm
    side_specs, side_shapes = _side_cast_specs(side_tables, nc // tn, ni)
    outs = pl.pallas_call(
        functools.partial(_mm_kernel, n_side=len(side_tables)),
        grid=(nc // tn, ni),
        in_specs=[pl.BlockSpec((tm, k), lambda j, i: (i, 0)),
                  pl.BlockSpec((k, tn), lambda j, i: (0, j))] + side_specs,
        out_specs=[pl.BlockSpec((tm, tn), lambda j, i: (i, j))] + side_specs,
        out_shape=[jax.ShapeDtypeStruct((n, nc), out_dtype)] + side_shapes,
        scratch_shapes=[pltpu.VMEM((k, tn), BF16)],
        compiler_params=_params("arbitrary", "arbitrary"),
        name=name,
    )(a, w, *side_tables)
    return outs[0] if not side_tables else outs


def _hgrn_tables(c):
    nl = int(math.log2(c))
    r = np.arange(c)[:, None]
    j = np.arange(c)[None, :]
    mats = []
    lvl = np.full((c, c), -1, np.int32)
    for l in range(nl):
        m = c >> (l + 1)
        mid = (r // (2 * m)) * 2 * m + m
        upper = r >= mid
        t = np.where(upper, (j >= mid) & (j <= r), (j > r) & (j <= mid - 1))
        mats.append(t)
        same = (r // (2 * m)) == (j // (2 * m))
        lvl = np.where(same & upper & (j < mid), l, lvl)
    lvl = np.where(r == j, nl, lvl)
    mats.append(j <= r)
    mats.append(j > r)
    t_all = np.concatenate(mats, axis=0).astype(np.float32)
    t_all = np.concatenate([t_all, t_all], axis=1)
    return jnp.asarray(t_all, BF16), jnp.asarray(lvl), nl


def _hgrn_prompt_kernel(q_ref, f_ref, v_ref, og_ref, lb_ref, g_ref, tall_ref, lvl_ref,
                        a_ref, st_ref, s_scr, *, chunk, n_levels, n_chunks, heads_per_step, dk):
    c = chunk
    s_scr[...] = jnp.zeros_like(s_scr)
    lb = lb_ref[...]
    lvl = lvl_ref[...]
    rows = lax.broadcasted_iota(jnp.int32, (c, q_ref.shape[1]), 0)
    nt = (((1,), (1,)), ((), ()))

    def body(ci, carry):
        sl = pl.ds(pl.multiple_of(ci * c, c), c)
        qs = _silu(q_ref[sl, :])
        f = lb + (1.0 - lb) * jax.nn.sigmoid(f_ref[sl, :])
        k = 1.0 - f
        lf = jnp.log(f)
        v = v_ref[sl, :].astype(BF16)
        lf_hi = lf.astype(BF16)
        lf_lo = (lf - lf_hi.astype(F32)).astype(BF16)
        ex = jnp.dot(tall_ref[...], jnp.concatenate([lf_hi, lf_lo], axis=0), preferred_element_type=F32)
        qs_b = qs.astype(BF16)
        k_b = k.astype(BF16)
        ys = []
        for l in range(n_levels):
            m = c >> (l + 1)
            e_l = jnp.exp(ex[l * c:(l + 1) * c, :])
            ys.append((e_l * jnp.where((rows & m) != 0, qs, k)).astype(BF16))
        bcum = ex[n_levels * c:(n_levels + 1) * c, :]
        brev = ex[(n_levels + 1) * c:(n_levels + 2) * c, :]
        qh = (qs * jnp.exp(bcum)).astype(BF16)
        kh = (k * jnp.exp(brev)).astype(BF16)
        decay = jnp.exp(bcum[c - 1:c, :])
        outs = []
        for hh in range(heads_per_step):
            hs = slice(hh * dk, (hh + 1) * dk)
            scores = jnp.where(lvl == n_levels,
                               lax.dot_general(qs_b[:, hs], k_b[:, hs], nt, preferred_element_type=F32), 0.0)
            for l in range(n_levels):
                y = ys[l][:, hs]
                p = lax.dot_general(y, y, nt, preferred_element_type=F32)
                scores = scores + jnp.where(lvl == l, p, 0.0)
            st = s_scr[hh]
            o = jnp.dot(scores.astype(BF16), v[:, hs], preferred_element_type=F32)
            o = o + lax.dot_general(qh[:, hs], st.astype(BF16), nt, preferred_element_type=F32)
            kv_t = lax.dot_general(v[:, hs], kh[:, hs], (((0,), (0,)), ((), ())), preferred_element_type=F32)
            s_scr[hh] = st * decay[:, hs] + kv_t
            outs.append(o * lax.rsqrt(jnp.mean(o * o, axis=-1, keepdims=True) + EPS))
        on = jnp.concatenate(outs, axis=1) if heads_per_step > 1 else outs[0]
        a_ref[sl, :] = (on * g_ref[...] * _silu(og_ref[sl, :])).astype(a_ref.dtype)
        return carry

    lax.fori_loop(0, n_chunks, body, 0, unroll=2)
    st_ref[0, 0] = s_scr[...]


def _hgrn_prompt(proj, lb, g_out, batch, seq, heads, dk, n_rows):
    c = _pow2_tile(128, seq)
    hp = _pow2_tile(4, heads)
    t_all, lvl, nl = _hgrn_tables(c)
    w = hp * dk
    nh = heads // hp
    col = lambda off: (lambda b, h: (b, off + h))
    kern = functools.partial(_hgrn_prompt_kernel, chunk=c, n_levels=nl, n_chunks=seq // c,
                             heads_per_step=hp, dk=dk)
    return pl.pallas_call(
        kern,
        grid=(batch, nh),
        in_specs=[pl.BlockSpec((seq, w), col(0)),
                  pl.BlockSpec((seq, w), col(nh)),
                  pl.BlockSpec((seq, w), col(2 * nh)),
                  pl.BlockSpec((seq, w), col(3 * nh)),
                  pl.BlockSpec((1, w), lambda b, h: (0, h)),
                  pl.BlockSpec((1, w), lambda b, h: (0, h)),
                  pl.BlockSpec(t_all.shape, lambda b, h: (0, 0)),
                  pl.BlockSpec(lvl.shape, lambda b, h: (0, 0))],
        out_specs=[pl.BlockSpec((seq, w), lambda b, h: (b, h)),
                   pl.BlockSpec((1, 1, hp, dk, dk), lambda b, h: (0, b, h, 0, 0))],
        out_shape=[jax.ShapeDtypeStruct((n_rows, heads * dk), BF16),
                   jax.ShapeDtypeStruct((1, batch, heads, dk, dk), F32)],
        scratch_shapes=[pltpu.VMEM((hp, dk, dk), F32)],
        compiler_params=_params("arbitrary", "arbitrary"),
        name="hgrn_prompt",
    )(proj, proj, proj, proj, lb, g_out, t_all, lvl)


def _hgrn_sample_kernel(q_ref, f_ref, v_ref, og_ref, lb_ref, g_ref, s0_ref, a_ref, s_ref, *, bt, steps):
    sub = 8
    per_tile = sub // steps
    lb = lb_ref[...]
    g = g_ref[...]
    rows = lax.broadcasted_iota(jnp.int32, (sub, q_ref.shape[1]), 0)
    tpos = rows % steps
    grp = rows // steps

    def per_batch(x, t):
        out = x[t:t + 1, :]
        for j in range(1, per_tile):
            out = jnp.where(grp == j, x[j * steps + t:j * steps + t + 1, :], out)
        return out

    def body(p, carry):
        sl = pl.ds(pl.multiple_of(p * sub, sub), sub)
        qs = _silu(q_ref[sl, :])
        f = lb + (1.0 - lb) * jax.nn.sigmoid(f_ref[sl, :])
        k = 1.0 - f
        v = v_ref[sl, :]
        bmat = jnp.log(f)
        shift = 1
        while shift < steps:
            bmat = jnp.where(tpos >= shift, bmat + pltpu.roll(bmat, shift, axis=0), bmat)
            shift *= 2
        blast = per_batch(bmat, steps - 1)
        o = jnp.zeros_like(qs)
        for t in range(steps):
            x = jnp.where(tpos >= t, qs * per_batch(k, t) * jnp.exp(jnp.minimum(bmat - per_batch(bmat, t), 0.0)), 0.0)
            o = o + jnp.sum(x, axis=-1, keepdims=True) * per_batch(v, t)
        qh = (qs * jnp.exp(bmat)).astype(BF16)
        kh = k * jnp.exp(blast - bmat)
        vb = v.astype(BF16)
        for j in range(per_tile):
            b = p * per_tile + j
            s0 = s0_ref[0, b, 0]
            mine = grp == j
            o = o + jnp.where(mine, jnp.dot(qh, s0.astype(BF16), preferred_element_type=F32), 0.0)
            kv = lax.dot_general(jnp.where(mine, kh, 0.0).astype(BF16), vb, (((0,), (0,)), ((), ())),
                                 preferred_element_type=F32)
            decay_row = jnp.exp(bmat[(j + 1) * steps - 1:(j + 1) * steps, :])
            decay_col = jnp.broadcast_to(decay_row, s0.shape).T
            s_ref[0, b, 0] = s0 * decay_col + kv
        on = o * lax.rsqrt(jnp.mean(o * o, axis=-1, keepdims=True) + EPS)
        a_ref[sl, :] = on * g * _silu(og_ref[sl, :])
        return carry

    lax.fori_loop(0, bt // per_tile, body, 0, unroll=4)


def _hgrn_sample(proj, lb, g_out, s0, heads, dk, n_prompt, steps):
    db = s0.shape[1]
    bt = _pow2_tile(32, db)
    rb = bt * steps
    assert 8 % steps == 0 and n_prompt % rb == 0
    col = lambda off: (lambda j, h: (n_prompt // rb + j, off + h))
    kern = functools.partial(_hgrn_sample_kernel, bt=bt, steps=steps)
    return pl.pallas_call(
        kern,
        grid=(db // bt, heads),
        in_specs=[pl.BlockSpec((rb, dk), col(0)),
                  pl.BlockSpec((rb, dk), col(heads)),
                  pl.BlockSpec((rb, dk), col(2 * heads)),
                  pl.BlockSpec((rb, dk), col(3 * heads)),
                  pl.BlockSpec((1, dk), lambda j, h: (0, h)),
                  pl.BlockSpec((1, dk), lambda j, h: (0, h)),
                  pl.BlockSpec((1, bt, 1, dk, dk), lambda j, h: (0, j, h, 0, 0))],
        out_specs=[pl.BlockSpec((rb, dk), lambda j, h: (j, h)),
                   pl.BlockSpec((1, bt, 1, dk, dk), lambda j, h: (0, j, h, 0, 0))],
        out_shape=[jax.ShapeDtypeStruct((db * steps, heads * dk), F32),
                   jax.ShapeDtypeStruct(s0.shape, F32)],
        compiler_params=_params("arbitrary", "arbitrary"),
        name="hgrn_sample",
    )(proj, proj, proj, proj, lb, g_out, s0)


def _gmlp_kernel(gu_ref, gv_ref, w_ref, bias_ref, g_ref, o_ref, v_ref, *, groups, gd):
    u = jax.nn.gelu(gu_ref[...])
    vv = jax.nn.gelu(gv_ref[...])
    r = lax.rsqrt(jnp.mean(vv * vv, axis=-1, keepdims=True) + EPS)
    v = vv * r * g_ref[...]
    v_ref[...] = v
    for gi in range(groups):
        sl = slice(gi * gd, (gi + 1) * gd)
        z = jnp.dot(w_ref[0, gi], v[:, sl].astype(BF16), preferred_element_type=F32) + bias_ref[0, :, sl]
        o_ref[:, sl] = (u[:, sl] * z).astype(o_ref.dtype)


def _gmlp(proj, w_st, bias_st, g_v, n_prompt, gu_blk, gv_blk):
    n = proj.shape[0]
    _, groups, c, _ = w_st.shape
    gw = bias_st.shape[2]
    npc = n_prompt // c
    sel = lambda i: jnp.where(i >= npc, 1, 0)
    kern = functools.partial(_gmlp_kernel, groups=groups, gd=gw // groups)
    return pl.pallas_call(
        kern,
        grid=(n // c,),
        in_specs=[pl.BlockSpec((c, gw), lambda i: (i, gu_blk)),
                  pl.BlockSpec((c, gw), lambda i: (i, gv_blk)),
                  pl.BlockSpec((1, groups, c, c), lambda i: (sel(i), 0, 0, 0)),
                  pl.BlockSpec((1, c, gw), lambda i: (sel(i), 0, 0)),
                  pl.BlockSpec((1, gw), lambda i: (0, 0))],
        out_specs=[pl.BlockSpec((c, gw), lambda i: (i, 0)),
                   pl.BlockSpec((c, gw), lambda i: (jnp.maximum(i - npc, 0), 0))],
        out_shape=[jax.ShapeDtypeStruct((n, gw), BF16),
                   jax.ShapeDtypeStruct((n - n_prompt, gw), F32)],
        compiler_params=_params("arbitrary"),
        name="chunk_mlp",
    )(proj, proj, w_st, bias_st, g_v.reshape(1, gw))


def _branch_kernel(ap_ref, as_ref, b_ref, wa_ref, wb_ref, ga_ref, gb_ref, o_ref, *, n_prompt_tiles):
    a = jnp.where(pl.program_id(1) < n_prompt_tiles, ap_ref[...], as_ref[...].astype(BF16))
    ya = jnp.dot(a, wa_ref[...], preferred_element_type=F32)
    yb = jnp.dot(b_ref[...], wb_ref[...], preferred_element_type=F32)
    o_ref[...] = (jax.nn.sigmoid(ga_ref[...]) * ya + jax.nn.sigmoid(gb_ref[...]) * yb).astype(o_ref.dtype)


def _branches(a_p, a_s, bm, wa, wb, proj, ga_off, gb_off, n_prompt):
    n, kb = bm.shape
    ka = a_p.shape[1]
    d = wa.shape[1]
    tm = _pow2_tile(512, n_prompt, n - n_prompt)
    tn = _pow2_tile(1024, d, ga_off, gb_off)
    npt = n_prompt // tm
    return pl.pallas_call(
        functools.partial(_branch_kernel, n_prompt_tiles=npt),
        grid=(d // tn, n // tm),
        in_specs=[pl.BlockSpec((tm, ka), lambda j, i: (jnp.minimum(i, npt - 1), 0)),
                  pl.BlockSpec((tm, ka), lambda j, i: (jnp.maximum(i - npt, 0), 0)),
                  pl.BlockSpec((tm, kb), lambda j, i: (i, 0)),
                  pl.BlockSpec((ka, tn), lambda j, i: (0, j)),
                  pl.BlockSpec((kb, tn), lambda j, i: (0, j)),
                  pl.BlockSpec((tm, tn), lambda j, i: (i, ga_off // tn + j)),
                  pl.BlockSpec((tm, tn), lambda j, i: (i, gb_off // tn + j))],
        out_specs=pl.BlockSpec((tm, tn), lambda j, i: (i, j)),
        out_shape=jax.ShapeDtypeStruct((n, d), BF16),
        compiler_params=_params("arbitrary", "arbitrary"),
        name="branches",
    )(a_p, a_s, bm, wa, wb, proj, proj)


def _out_kernel(m_ref, w_ref, xp_ref, xs_ref, gp_ref, gs_ref, side_ref, o_ref, side_out_ref, *, n_prompt_tiles):
    i = pl.program_id(1)
    y = jnp.dot(m_ref[...], w_ref[...], preferred_element_type=F32)
    side_out_ref[...] = side_ref[...].astype(side_out_ref.dtype)

    @pl.when(i < n_prompt_tiles)
    def _():
        o_ref[...] = xp_ref[...] + gp_ref[0] * y

    @pl.when(i >= n_prompt_tiles)
    def _():
        o_ref[...] = xs_ref[...] + gs_ref[...] * y


def _out_proj(mix, w, x_p, x_s, mod_p, mod_s, k_gate, n_prompt, seq, side_table):
    n, d = mix.shape
    tm = _pow2_tile(512, seq, n - n_prompt)
    tn = _pow2_tile(1024, d)
    npt = n_prompt // tm
    per_b = seq // tm
    nj = d // tn
    side_specs, side_shapes = _side_cast_specs((side_table,), nj, n // tm)
    return pl.pallas_call(
        functools.partial(_out_kernel, n_prompt_tiles=npt),
        grid=(nj, n // tm),
        in_specs=[pl.BlockSpec((tm, d), lambda j, i: (i, 0)),
                  pl.BlockSpec((d, tn), lambda j, i: (0, j)),
                  pl.BlockSpec((tm, tn), lambda j, i: (jnp.minimum(i, npt - 1), j)),
                  pl.BlockSpec((tm, tn), lambda j, i: (jnp.maximum(i - npt, 0), j)),
                  pl.BlockSpec((1, 1, tn), lambda j, i: (jnp.minimum(i, npt - 1) // per_b, 0, k_gate * nj + j)),
                  pl.BlockSpec((tm, tn), lambda j, i: (jnp.maximum(i - npt, 0), k_gate * nj + j))] + side_specs,
        out_specs=[pl.BlockSpec((tm, tn), lambda j, i: (i, j))] + side_specs,
        out_shape=[jax.ShapeDtypeStruct((n, d), F32)] + side_shapes,
        compiler_params=_params("arbitrary", "arbitrary"),
        name="out_proj",
    )(mix, w, x_p, x_s, mod_p, mod_s, side_table)


def _cand_pairs(k):
    return [(i, j) for i in range(k) for j in range(k) if (i + 1) * (j + 1) <= k]


def _arg_rounds(s, iota, k, exact):
    rank = jnp.full(s.shape, float(k), F32)
    vals = []
    big = float(s.shape[0])
    for i in range(k):
        m = jnp.max(s, axis=0, keepdims=True)
        if exact:
            idx = jnp.min(jnp.where(s == m, iota, big), axis=0, keepdims=True)
            sel = iota == idx
        else:
            sel = s == m
        rank = jnp.where(sel, float(i), rank)
        s = jnp.where(sel, NEG_INF, s)
        vals.append(m)
    return vals, rank


def _peer_gates(s1, s2, jsum, cand_scr, cols, pairs, k, exact):
    n_iota = lax.broadcasted_iota(jnp.int32, s1.shape, 0).astype(F32)
    a, r1 = _arg_rounds(s1, n_iota, k, exact)
    b, r2 = _arg_rounds(s2, n_iota, k, exact)

    n_cand = cand_scr.shape[0]
    cand_scr[len(pairs):, cols] = jnp.full((n_cand - len(pairs), s1.shape[1]), NEG_INF, F32)
    for p, (i, j) in enumerate(pairs):
        cand_scr[p:p + 1, cols] = a[i] + b[j]
    c0 = cand_scr[:, cols]
    p_iota = lax.broadcasted_iota(jnp.int32, c0.shape, 0).astype(F32)
    _, prank = _arg_rounds(c0, p_iota, k, exact)
    taken = prank < float(k)
    w = jnp.where(taken, jnp.exp(c0 - (a[0] + b[0])), 0.0)
    inv_z = 1.0 / jnp.sum(w, axis=0, keepdims=True)
    ones = jnp.where(taken, 1.0, 0.0)
    jcnt = jnp.dot(jsum, ones.astype(BF16), preferred_element_type=F32)

    eaz = jnp.zeros(s1.shape, F32)
    jd = jnp.zeros(s1.shape, F32)
    eb = jnp.zeros(s1.shape, F32)
    for i in range(k):
        hit1 = r1 == float(i)
        eaz = jnp.where(hit1, jnp.exp(a[i] - a[0]) * inv_z, eaz)
        jd = jnp.where(hit1, jcnt[i:i + 1, :], jd)
        eb = jnp.where(r2 == float(i), jnp.exp(b[i] - b[0]), eb)

    def excess(rank):
        return jnp.abs(jnp.sum(jnp.where(rank < float(k), 1.0, 0.0), axis=0, keepdims=True) - float(k))

    return eaz, jd, eb, r2, excess(r1) + excess(r2) + excess(prank)


def _peer_topk_kernel(q_ref, sk_ref, jsum_ref, eaz_ref, jd_ref, eb_ref, r2_ref, cand_scr, *, half, pairs, k):
    nt = (((1,), (1,)), ((), ()))
    q = q_ref[...].astype(BF16)
    s1 = lax.dot_general(sk_ref[0, 0], q[:, :half], nt, preferred_element_type=F32)
    s2 = lax.dot_general(sk_ref[0, 1], q[:, half:], nt, preferred_element_type=F32)
    jsum = jsum_ref[...]
    n_blocks = s1.shape[1] // LANE_V7X

    def run(exact):
        bad = None
        for cb in range(n_blocks):
            cols = slice(cb * LANE_V7X, (cb + 1) * LANE_V7X)
            eaz, jd, eb, r2, dev = _peer_gates(s1[:, cols], s2[:, cols], jsum, cand_scr, cols, pairs, k, exact)
            eaz_ref[0, cb] = eaz
            jd_ref[0, cb] = jd
            eb_ref[0, cb] = eb.astype(eb_ref.dtype)
            r2_ref[0, cb] = r2.astype(r2_ref.dtype)
            bad = dev if bad is None else bad + dev
        return jnp.max(bad)

    miscount = run(exact=False)

    @pl.when(miscount > 0.5)
    def _():
        run(exact=True)


def _peer_topk(qp, sub_keys_b, n_prompt):
    n = qp.shape[0]
    heads, _, n_keys, half = sub_keys_b.shape
    tt = _pow2_tile(256, n_prompt, n - n_prompt)
    pairs = _cand_pairs(PEER_TOPK)
    n_cand = -(-len(pairs) // 64) * 64
    jsum = np.zeros((PEER_TOPK, n_cand), np.float32)
    for p, (i, _) in enumerate(pairs):
        jsum[i, p] = 1.0
    kern = functools.partial(_peer_topk_kernel, half=half, pairs=pairs, k=PEER_TOPK)
    maps = jax.ShapeDtypeStruct((heads, n // LANE_V7X, n_keys, LANE_V7X), F32)
    maps_b = jax.ShapeDtypeStruct((heads, n // LANE_V7X, n_keys, LANE_V7X), BF16)
    mspec = pl.BlockSpec((1, tt // LANE_V7X, n_keys, LANE_V7X), lambda i, h: (h, i, 0, 0))
    return pl.pallas_call(
        kern,
        grid=(n // tt, heads),
        in_specs=[pl.BlockSpec((tt, 2 * half), lambda i, h: (i, h)),
                  pl.BlockSpec((1, 2, n_keys, half), lambda i, h: (h, 0, 0, 0)),
                  pl.BlockSpec(jsum.shape, lambda i, h: (0, 0))],
        out_specs=[mspec, mspec, mspec, mspec],
        out_shape=[maps, maps, maps_b, maps_b],
        scratch_shapes=[pltpu.VMEM((n_cand, tt), F32)],
        compiler_params=_params("arbitrary", "arbitrary"),
        name="peer_topk",
    )(qp, sub_keys_b, jnp.asarray(jsum, BF16))


def _peer_main_kernel(h_ref, u_ref, v_ref, eaz_ref, jd_ref, eb_ref, r2_ref, o_ref,
                      ga0_scr, ga1_scr, g0_scr, g1_scr, ht_scr, *, heads, n_keys, rows_per_step, n_steps):
    e = pl.program_id(1)
    n_tb = h_ref.shape[0] // LANE_V7X
    blocks = [(r, tb) for r in range(rows_per_step) for tb in range(n_tb)]
    n_slices = math.gcd(len(blocks), 16)

    def gates(chunk, g_scr, part=None):
        for r, tb in (blocks if part is None else blocks[part::n_slices]):
            n1 = chunk * rows_per_step + r
            g = None
            for h in range(heads):
                ea = eaz_ref[h, tb, pl.ds(n1, 1), :].astype(BF16)
                jd = jd_ref[h, tb, pl.ds(n1, 1), :].astype(BF16)
                t = ea * jnp.where(r2_ref[h, tb] < jd, eb_ref[h, tb], jnp.zeros((), BF16))
                g = t if g is None else g + t
            g_scr[r * n_tb + tb] = g

    @pl.when(e == 0)
    def _():
        o_ref[...] = jnp.zeros_like(o_ref)
        ga1_scr[...] = jnp.zeros_like(ga1_scr)
        gates(0, g0_scr)
        ht_scr[...] = h_ref[...].T

    def stage(ga_cur, ga_prev, g_cur, g_next):
        act_t = jnp.dot(u_ref[...], ht_scr[...], preferred_element_type=F32)
        dq = o_ref.shape[1] // n_slices
        for q in range(n_slices):
            qs = slice(q * dq, (q + 1) * dq)
            o_ref[:, qs] += jnp.dot(ga_prev[...], v_ref[:, qs], preferred_element_type=F32)
            for r, tb in blocks[q::n_slices]:
                sl = slice(r * n_keys, (r + 1) * n_keys)
                ts = slice(tb * LANE_V7X, (tb + 1) * LANE_V7X)
                ga_cur[ts, sl] = (g_cur[r * n_tb + tb] * jax.nn.gelu(act_t[sl, ts].astype(BF16))).T
            gates(jnp.minimum(e + 1, n_steps - 1), g_next, part=q)

    @pl.when(e % 2 == 0)
    def _():
        stage(ga0_scr, ga1_scr, g0_scr, g1_scr)

    @pl.when(e % 2 == 1)
    def _():
        stage(ga1_scr, ga0_scr, g1_scr, g0_scr)


def _peer_main(h2, u_b, v_b, maps, n_prompt):
    n, d = h2.shape
    n_exp = u_b.shape[0]
    heads, _, n_keys, lane = maps[0].shape
    tt = _pow2_tile(512, n_prompt, n - n_prompt)
    ec = _pow2_tile(512, n_exp)
    rps = ec // n_keys
    n_steps = n_exp // ec
    kern = functools.partial(_peer_main_kernel, heads=heads, n_keys=n_keys, rows_per_step=rps, n_steps=n_steps)
    once = pl.Buffered(1)
    mspec = pl.BlockSpec((heads, tt // lane, n_keys, lane), lambda i, e: (0, i, 0, 0), pipeline_mode=once)
    return pl.pallas_call(
        kern,
        grid=(n // tt, n_steps + 1),
        in_specs=[pl.BlockSpec((tt, d), lambda i, e: (i, 0), pipeline_mode=once),
                  pl.BlockSpec((ec, d), lambda i, e: (jnp.minimum(e, n_steps - 1), 0)),
                  pl.BlockSpec((ec, d), lambda i, e: (jnp.maximum(e - 1, 0), 0)),
                  mspec, mspec, mspec, mspec],
        out_specs=pl.BlockSpec((tt, d), lambda i, e: (i, 0), pipeline_mode=once),
        out_shape=jax.ShapeDtypeStruct((n, d), F32),
        scratch_shapes=[pltpu.VMEM((tt, ec), BF16), pltpu.VMEM((tt, ec), BF16),
                        pltpu.VMEM((rps * tt // lane, n_keys, lane), BF16),
                        pltpu.VMEM((rps * tt // lane, n_keys, lane), BF16),
                        pltpu.VMEM((d, tt), BF16)],
        compiler_params=_params("arbitrary", "arbitrary"),
        name="peer_main",
    )(h2, u_b, v_b, *maps)


def _final_kernel(x_ref, p_ref, gp_ref, gs_ref, g_ref, yp_ref, ys_ref, *, n_prompt_tiles):
    i = pl.program_id(0)

    def norm(x):
        return x * lax.rsqrt(jnp.mean(x * x, axis=-1, keepdims=True) + EPS) * g_ref[...]

    @pl.when(i < n_prompt_tiles)
    def _():
        yp_ref[...] = norm(x_ref[...] + gp_ref[0] * p_ref[...])

    @pl.when(i >= n_prompt_tiles)
    def _():
        ys_ref[...] = norm(x_ref[...] + gs_ref[...] * p_ref[...])


def _final(x1, peer_out, mod_p, mod_s, k_gate, g_final, n_prompt, seq):
    n, d = x1.shape
    tm = _pow2_tile(256, seq, n - n_prompt)
    npt = n_prompt // tm
    per_b = seq // tm
    return pl.pallas_call(
        functools.partial(_final_kernel, n_prompt_tiles=npt),
        grid=(n // tm,),
        in_specs=[pl.BlockSpec((tm, d), lambda i: (i, 0)),
                  pl.BlockSpec((tm, d), lambda i: (i, 0)),
                  pl.BlockSpec((1, 1, d), lambda i: (jnp.minimum(i, npt - 1) // per_b, 0, k_gate)),
                  pl.BlockSpec((tm, d), lambda i: (jnp.maximum(i - npt, 0), k_gate)),
                  pl.BlockSpec((1, d), lambda i: (0, 0))],
        out_specs=[pl.BlockSpec((tm, d), lambda i: (jnp.minimum(i, npt - 1), 0)),
                   pl.BlockSpec((tm, d), lambda i: (jnp.maximum(i - npt, 0), 0))],
        out_shape=[jax.ShapeDtypeStruct((n_prompt, d), F32),
                   jax.ShapeDtypeStruct((n - n_prompt, d), F32)],
        compiler_params=_params("arbitrary"),
        name="final_norm",
    )(x1, peer_out, mod_p, mod_s, g_final.reshape(1, d))


def kernel(x_prompt, x_sample, c_prompt, c_sample, state_hgrn, w_ada, b_ada, g_norm1, w_in, hgrn_lb_logits,
           g_hgrn_out, g_gmlp_v, w_spatial, b_spatial, w_branch_a, w_branch_b, w_out, g_norm2, w_peer_q,
           peer_sub_keys, peer_u, peer_v, g_final):
    batch, seq, d = x_prompt.shape
    db, dt, _ = x_sample.shape
    depth, _, heads, dk, dv = state_hgrn.shape
    assert depth == 1 and dk == dv == LANE_V7X
    hw = heads * dk
    groups, gc = w_spatial.shape[1], w_spatial.shape[2]
    gw = w_branch_b.shape[1]
    assert gc == LANE_V7X and gw // groups == LANE_V7X and dt <= gc and gc % dt == 0
    n_prompt, n_sample = batch * seq, db * dt
    n = n_prompt + n_sample

    c_all = jnp.concatenate([c_prompt, c_sample], axis=0)
    pad = (-c_all.shape[0]) % 8
    c_all = jnp.pad(c_all, ((0, pad), (0, 0)))
    mod = _ada(c_all, w_ada[0], b_ada[0])
    mod_p = mod[:batch].reshape(batch, 1, N_MOD * d)
    mod_s = jnp.repeat(mod[batch:batch + db], dt, axis=0)

    x_p = x_prompt.reshape(n_prompt, d)
    x_s = x_sample.reshape(n_sample, d)

    h1 = _norm_mod(x_p, x_s, g_norm1[0], mod_p, mod_s, 1, 0, n_prompt, n_sample, seq)
    proj, peer_u_b = _matmul(h1, w_in[0], F32, n_prompt, "in_proj", side_tables=(peer_u[0],))

    lb = jnp.cumsum(jax.nn.softmax(hgrn_lb_logits.astype(F32), axis=0), axis=0)[0].reshape(1, hw)
    g_ho = g_hgrn_out[0].reshape(1, hw)
    a_p, st_p = _hgrn_prompt(proj, lb, g_ho, batch, seq, heads, dk, n_prompt)
    a_s, st_s = _hgrn_sample(proj, lb, g_ho, state_hgrn, heads, dk, n_prompt, dt)

    tril = jnp.tril(jnp.ones((gc, gc), F32))
    w_sp = w_spatial[0]
    blk = jnp.arange(gc) // dt
    w_samp = jnp.tile(w_sp[:, :dt, :dt], (1, gc // dt, gc // dt)) * (blk[:, None] == blk[None, :])
    w_st = jnp.stack([w_sp * tril, w_samp * tril]).astype(BF16)
    bias_full = jnp.repeat(b_spatial[0].T, gw // groups, axis=1)
    bias_st = jnp.stack([bias_full, jnp.tile(bias_full[:dt], (gc // dt, 1))])
    gu_blk = 4 * hw // gw
    assert gu_blk * gw == 4 * hw
    bm, v_s = _gmlp(proj, w_st, bias_st, g_gmlp_v[0], n_prompt, gu_blk, gu_blk + 1)

    ga_off = 4 * hw + 2 * gw
    mix = _branches(a_p, a_s, bm, w_branch_a[0].astype(BF16), w_branch_b[0].astype(BF16), proj,
                    ga_off, ga_off + d, n_prompt)
    x1, peer_v_b = _out_proj(mix, w_out[0].astype(BF16), x_p, x_s, mod_p, mod_s, 2, n_prompt, seq, peer_v[0])

    h2 = _norm_mod(x1, x1, g_norm2[0], mod_p, mod_s, 4, 3, n_prompt, n_sample, seq)
    qp = _matmul(h2, w_peer_q[0], F32, n_prompt, "peer_query")
    maps = _peer_topk(qp, peer_sub_keys[0].astype(BF16), n_prompt)
    peer_out = _peer_main(h2, peer_u_b, peer_v_b, maps, n_prompt)
    y_p, y_s = _final(x1, peer_out, mod_p, mod_s, 5, g_final, n_prompt, seq)

    state_p = jnp.swapaxes(st_p, -1, -2)
    return (y_p.reshape(batch, seq, d), y_s.reshape(db, dt, d), state_p, st_s,
            v_s.reshape(1, db, dt, gw))
```

```python
import functools
import math

import numpy as np
import jax
import jax.numpy as jnp
from jax import lax
from jax.experimental import pallas as pl
from jax.experimental.pallas import tpu as pltpu

EPS = 1e-6
N_MOD = 6
PEER_TOPK = 16
LANE_V7X = 128
VMEM_LIMIT_V7X = 60 * 1024 * 1024
F32 = jnp.float32
BF16 = jnp.bfloat16
NEG_INF = float("-inf")


def _params(*sem):
    return pltpu.CompilerParams(dimension_semantics=sem, vmem_limit_bytes=VMEM_LIMIT_V7X)


def _pow2_tile(target, *sizes):
    g = 0
    for s in sizes:
        g = math.gcd(g, s)
    t = 1
    while t * 2 <= target and g % (t * 2) == 0:
        t *= 2
    return t


def _silu(x):
    return x * jax.nn.sigmoid(x)


def _ada_kernel(c_ref, w_ref, b_ref, o_ref, s_scr):
    @pl.when(pl.program_id(0) == 0)
    def _():
        s_scr[...] = _silu(c_ref[...]).astype(BF16)

    o_ref[...] = jnp.dot(s_scr[...], w_ref[...].astype(BF16), preferred_element_type=F32) + b_ref[...]


def _ada(c_rows, w_ada, b_ada):
    m, d = c_rows.shape
    n = w_ada.shape[1]
    tn = _pow2_tile(512, n)
    return pl.pallas_call(
        _ada_kernel,
        grid=(n // tn,),
        in_specs=[pl.BlockSpec((m, d), lambda j: (0, 0)),
                  pl.BlockSpec((d, tn), lambda j: (0, j)),
                  pl.BlockSpec((1, tn), lambda j: (0, j))],
        out_specs=pl.BlockSpec((m, tn), lambda j: (0, j)),
        out_shape=jax.ShapeDtypeStruct((m, n), F32),
        scratch_shapes=[pltpu.VMEM((m, d), BF16)],
        compiler_params=_params("arbitrary"),
        name="ada_mod",
    )(c_rows, w_ada, b_ada.reshape(1, n))


def _norm_mod_kernel(xp_ref, xs_ref, g_ref, scp_ref, shp_ref, scs_ref, shs_ref, o_ref, *, n_prompt_tiles):
    i = pl.program_id(0)

    def normed(x):
        return x * lax.rsqrt(jnp.mean(x * x, axis=-1, keepdims=True) + EPS) * g_ref[...]

    @pl.when(i < n_prompt_tiles)
    def _():
        o_ref[...] = (normed(xp_ref[...]) * (1.0 + scp_ref[0]) + shp_ref[0]).astype(o_ref.dtype)

    @pl.when(i >= n_prompt_tiles)
    def _():
        o_ref[...] = (normed(xs_ref[...]) * (1.0 + scs_ref[...]) + shs_ref[...]).astype(o_ref.dtype)


def _row_sources(x_p, x_s, n_prompt, tm):
    npt = n_prompt // tm
    s_off = npt if x_s is x_p else 0
    return (lambda i: (jnp.minimum(i, npt - 1), 0)), (lambda i: (s_off + jnp.maximum(i - npt, 0), 0))


def _norm_mod(x_p, x_s, g, mod_p, mod_s, k_scale, k_shift, n_prompt, n_sample, seq):
    d = x_p.shape[1]
    n = n_prompt + n_sample
    tm = _pow2_tile(256, seq, n_sample)
    npt = n_prompt // tm
    per_b = seq // tm
    p_map, s_map = _row_sources(x_p, x_s, n_prompt, tm)

    def pidx(k):
        return lambda i: (jnp.minimum(i, npt - 1) // per_b, 0, k)

    def sidx(k):
        return lambda i: (jnp.maximum(i - npt, 0), k)

    return pl.pallas_call(
        functools.partial(_norm_mod_kernel, n_prompt_tiles=npt),
        grid=(n // tm,),
        in_specs=[pl.BlockSpec((tm, d), p_map),
                  pl.BlockSpec((tm, d), s_map),
                  pl.BlockSpec((1, d), lambda i: (0, 0)),
                  pl.BlockSpec((1, 1, d), pidx(k_scale)),
                  pl.BlockSpec((1, 1, d), pidx(k_shift)),
                  pl.BlockSpec((tm, d), sidx(k_scale)),
                  pl.BlockSpec((tm, d), sidx(k_shift))],
        out_specs=pl.BlockSpec((tm, d), lambda i: (i, 0)),
        out_shape=jax.ShapeDtypeStruct((n, d), BF16),
        compiler_params=_params("arbitrary"),
        name="norm_mod",
    )(x_p, x_s, g.reshape(1, d), mod_p, mod_p, mod_s, mod_s)


def _mm_kernel(a_ref, w_ref, *rest, n_side):
    side_in, o_ref, side_out, wb_scr = rest[:n_side], rest[n_side], rest[n_side + 1:2 * n_side + 1], rest[-1]

    @pl.when(pl.program_id(1) == 0)
    def _():
        wb_scr[...] = w_ref[...].astype(BF16)

    o_ref[...] = jnp.dot(a_ref[...], wb_scr[...], preferred_element_type=F32).astype(o_ref.dtype)
    for src, dst in zip(side_in, side_out):
        dst[...] = src[...].astype(dst.dtype)


def _side_cast_specs(side_tables, nj, ni):
    specs, shapes = [], []
    for t in side_tables:
        rows = t.shape[0]
        rb = rows // _pow2_tile(nj * ni, rows)
        last = rows // rb - 1
        specs.append(pl.BlockSpec((rb, t.shape[1]), lambda j, i, last=last: (jnp.minimum(j * ni + i, last), 0)))
        shapes.append(jax.ShapeDtypeStruct(t.shape, BF16))
    return specs, shapes


def _matmul(a, w, out_dtype, n_prompt, name, side_tables=()):
    n, k = a.shape
    nc = w.shape[1]
    tm = _pow2_tile(512, n_prompt, n - n_prompt)
    tn = _pow2_tile(1024, nc)
    ni = n // tm
    side_specs, side_shapes = _side_cast_specs(side_tables, nc // tn, ni)
    outs = pl.pallas_call(
        functools.partial(_mm_kernel, n_side=len(side_tables)),
        grid=(nc // tn, ni),
        in_specs=[pl.BlockSpec((tm, k), lambda j, i: (i, 0)),
                  pl.BlockSpec((k, tn), lambda j, i: (0, j))] + side_specs,
        out_specs=[pl.BlockSpec((tm, tn), lambda j, i: (i, j))] + side_specs,
        out_shape=[jax.ShapeDtypeStruct((n, nc), out_dtype)] + side_shapes,
        scratch_shapes=[pltpu.VMEM((k, tn), BF16)],
        compiler_params=_params("arbitrary", "arbitrary"),
        name=name,
    )(a, w, *side_tables)
    return outs[0] if not side_tables else outs


def _hgrn_tables(c):
    nl = int(math.log2(c))
    r = np.arange(c)[:, None]
    j = np.arange(c)[None, :]
    mats = []
    lvl = np.full((c, c), -1, np.int32)
    for l in range(nl):
        m = c >> (l + 1)
        mid = (r // (2 * m)) * 2 * m + m
        upper = r >= mid
        t = np.where(upper, (j >= mid) & (j <= r), (j > r) & (j <= mid - 1))
        mats.append(t)
        same = (r // (2 * m)) == (j // (2 * m))
        lvl = np.where(same & upper & (j < mid), l, lvl)
    lvl = np.where(r == j, nl, lvl)
    mats.append(j <= r)
    mats.append(j > r)
    t_all = np.concatenate(mats, axis=0).astype(np.float32)
    t_all = np.concatenate([t_all, t_all], axis=1)
    return jnp.asarray(t_all, BF16), jnp.asarray(lvl), nl


def _hgrn_prompt_kernel(q_ref, f_ref, v_ref, og_ref, lb_ref, g_ref, tall_ref, lvl_ref,
                        a_ref, st_ref, s_scr, *, chunk, n_levels, n_chunks, heads_per_step, dk):
    c = chunk
    s_scr[...] = jnp.zeros_like(s_scr)
    lb = lb_ref[...]
    lvl = lvl_ref[...]
    rows = lax.broadcasted_iota(jnp.int32, (c, q_ref.shape[1]), 0)
    nt = (((1,), (1,)), ((), ()))

    def body(ci, carry):
        sl = pl.ds(pl.multiple_of(ci * c, c), c)
        qs = _silu(q_ref[sl, :])
        f = lb + (1.0 - lb) * jax.nn.sigmoid(f_ref[sl, :])
        k = 1.0 - f
        lf = jnp.log(f)
        v = v_ref[sl, :].astype(BF16)
        lf_hi = lf.astype(BF16)
        lf_lo = (lf - lf_hi.astype(F32)).astype(BF16)
        ex = jnp.dot(tall_ref[...], jnp.concatenate([lf_hi, lf_lo], axis=0), preferred_element_type=F32)
        qs_b = qs.astype(BF16)
        k_b = k.astype(BF16)
        ys = []
        for l in range(n_levels):
            m = c >> (l + 1)
            e_l = jnp.exp(ex[l * c:(l + 1) * c, :])
            ys.append((e_l * jnp.where((rows & m) != 0, qs, k)).astype(BF16))
        bcum = ex[n_levels * c:(n_levels + 1) * c, :]
        brev = ex[(n_levels + 1) * c:(n_levels + 2) * c, :]
        qh = (qs * jnp.exp(bcum)).astype(BF16)
        kh = (k * jnp.exp(brev)).astype(BF16)
        decay = jnp.exp(bcum[c - 1:c, :])
        outs = []
        for hh in range(heads_per_step):
            hs = slice(hh * dk, (hh + 1) * dk)
            scores = jnp.where(lvl == n_levels,
                               lax.dot_general(qs_b[:, hs], k_b[:, hs], nt, preferred_element_type=F32), 0.0)
            for l in range(n_levels):
                y = ys[l][:, hs]
                p = lax.dot_general(y, y, nt, preferred_element_type=F32)
                scores = scores + jnp.where(lvl == l, p, 0.0)
            st = s_scr[hh]
            o = jnp.dot(scores.astype(BF16), v[:, hs], preferred_element_type=F32)
            o = o + lax.dot_general(qh[:, hs], st.astype(BF16), nt, preferred_element_type=F32)
            kv_t = lax.dot_general(v[:, hs], kh[:, hs], (((0,), (0,)), ((), ())), preferred_element_type=F32)
            s_scr[hh] = st * decay[:, hs] + kv_t
            outs.append(o * lax.rsqrt(jnp.mean(o * o, axis=-1, keepdims=True) + EPS))
        on = jnp.concatenate(outs, axis=1) if heads_per_step > 1 else outs[0]
        a_ref[sl, :] = (on * g_ref[...] * _silu(og_ref[sl, :])).astype(a_ref.dtype)
        return carry

    lax.fori_loop(0, n_chunks, body, 0, unroll=2)
    st_ref[0, 0] = s_scr[...]


def _hgrn_prompt(proj, lb, g_out, batch, seq, heads, dk, n_rows):
    c = _pow2_tile(128, seq)
    hp = _pow2_tile(4, heads)
    t_all, lvl, nl = _hgrn_tables(c)
    w = hp * dk
    nh = heads // hp
    col = lambda off: (lambda b, h: (b, off + h))
    kern = functools.partial(_hgrn_prompt_kernel, chunk=c, n_levels=nl, n_chunks=seq // c,
                             heads_per_step=hp, dk=dk)
    return pl.pallas_call(
        kern,
        grid=(batch, nh),
        in_specs=[pl.BlockSpec((seq, w), col(0)),
                  pl.BlockSpec((seq, w), col(nh)),
                  pl.BlockSpec((seq, w), col(2 * nh)),
                  pl.BlockSpec((seq, w), col(3 * nh)),
                  pl.BlockSpec((1, w), lambda b, h: (0, h)),
                  pl.BlockSpec((1, w), lambda b, h: (0, h)),
                  pl.BlockSpec(t_all.shape, lambda b, h: (0, 0)),
                  pl.BlockSpec(lvl.shape, lambda b, h: (0, 0))],
        out_specs=[pl.BlockSpec((seq, w), lambda b, h: (b, h)),
                   pl.BlockSpec((1, 1, hp, dk, dk), lambda b, h: (0, b, h, 0, 0))],
        out_shape=[jax.ShapeDtypeStruct((n_rows, heads * dk), BF16),
                   jax.ShapeDtypeStruct((1, batch, heads, dk, dk), F32)],
        scratch_shapes=[pltpu.VMEM((hp, dk, dk), F32)],
        compiler_params=_params("arbitrary", "arbitrary"),
        name="hgrn_prompt",
    )(proj, proj, proj, proj, lb, g_out, t_all, lvl)


def _hgrn_sample_kernel(q_ref, f_ref, v_ref, og_ref, lb_ref, g_ref, s0_ref, a_ref, s_ref, *, bt, steps):
    sub = 8
    per_tile = sub // steps
    lb = lb_ref[...]
    g = g_ref[...]
    rows = lax.broadcasted_iota(jnp.int32, (sub, q_ref.shape[1]), 0)
    tpos = rows % steps
    grp = rows // steps

    def per_batch(x, t):
        out = x[t:t + 1, :]
        for j in range(1, per_tile):
            out = jnp.where(grp == j, x[j * steps + t:j * steps + t + 1, :], out)
        return out

    def body(p, carry):
        sl = pl.ds(pl.multiple_of(p * sub, sub), sub)
        qs = _silu(q_ref[sl, :])
        f = lb + (1.0 - lb) * jax.nn.sigmoid(f_ref[sl, :])
        k = 1.0 - f
        v = v_ref[sl, :]
        bmat = jnp.log(f)
        shift = 1
        while shift < steps:
            bmat = jnp.where(tpos >= shift, bmat + pltpu.roll(bmat, shift, axis=0), bmat)
            shift *= 2
        blast = per_batch(bmat, steps - 1)
        o = jnp.zeros_like(qs)
        for t in range(steps):
            x = jnp.where(tpos >= t, qs * per_batch(k, t) * jnp.exp(jnp.minimum(bmat - per_batch(bmat, t), 0.0)), 0.0)
            o = o + jnp.sum(x, axis=-1, keepdims=True) * per_batch(v, t)
        qh = (qs * jnp.exp(bmat)).astype(BF16)
        kh = k * jnp.exp(blast - bmat)
        vb = v.astype(BF16)
        for j in range(per_tile):
            b = p * per_tile + j
            s0 = s0_ref[0, b, 0]
            mine = grp == j
            o = o + jnp.where(mine, jnp.dot(qh, s0.astype(BF16), preferred_element_type=F32), 0.0)
            kv = lax.dot_general(jnp.where(mine, kh, 0.0).astype(BF16), vb, (((0,), (0,)), ((), ())),
                                 preferred_element_type=F32)
            decay_row = jnp.exp(bmat[(j + 1) * steps - 1:(j + 1) * steps, :])
            decay_col = jnp.broadcast_to(decay_row, s0.shape).T
            s_ref[0, b, 0] = s0 * decay_col + kv
        on = o * lax.rsqrt(jnp.mean(o * o, axis=-1, keepdims=True) + EPS)
        a_ref[sl, :] = on * g * _silu(og_ref[sl, :])
        return carry

    lax.fori_loop(0, bt // per_tile, body, 0, unroll=4)


def _hgrn_sample(proj, lb, g_out, s0, heads, dk, n_prompt, steps):
    db = s0.shape[1]
    bt = _pow2_tile(32, db)
    rb = bt * steps
    assert 8 % steps == 0 and n_prompt % rb == 0
    col = lambda off: (lambda j, h: (n_prompt // rb + j, off + h))
    kern = functools.partial(_hgrn_sample_kernel, bt=bt, steps=steps)
    return pl.pallas_call(
        kern,
        grid=(db // bt, heads),
        in_specs=[pl.BlockSpec((rb, dk), col(0)),
                  pl.BlockSpec((rb, dk), col(heads)),
                  pl.BlockSpec((rb, dk), col(2 * heads)),
                  pl.BlockSpec((rb, dk), col(3 * heads)),
                  pl.BlockSpec((1, dk), lambda j, h: (0, h)),
                  pl.BlockSpec((1, dk), lambda j, h: (0, h)),
                  pl.BlockSpec((1, bt, 1, dk, dk), lambda j, h: (0, j, h, 0, 0))],
        out_specs=[pl.BlockSpec((rb, dk), lambda j, h: (j, h)),
                   pl.BlockSpec((1, bt, 1, dk, dk), lambda j, h: (0, j, h, 0, 0))],
        out_shape=[jax.ShapeDtypeStruct((db * steps, heads * dk), F32),
                   jax.ShapeDtypeStruct(s0.shape, F32)],
        compiler_params=_params("arbitrary", "arbitrary"),
        name="hgrn_sample",
    )(proj, proj, proj, proj, lb, g_out, s0)


def _gmlp_kernel(gu_ref, gv_ref, w_ref, bias_ref, g_ref, o_ref, v_ref, *, groups, gd):
    u = jax.nn.gelu(gu_ref[...])
    vv = jax.nn.gelu(gv_ref[...])
    r = lax.rsqrt(jnp.mean(vv * vv, axis=-1, keepdims=True) + EPS)
    v = vv * r * g_ref[...]
    v_ref[...] = v
    for gi in range(groups):
        sl = slice(gi * gd, (gi + 1) * gd)
        z = jnp.dot(w_ref[0, gi], v[:, sl].astype(BF16), preferred_element_type=F32) + bias_ref[0, :, sl]
        o_ref[:, sl] = (u[:, sl] * z).astype(o_ref.dtype)


def _gmlp(proj, w_st, bias_st, g_v, n_prompt, gu_blk, gv_blk):
    n = proj.shape[0]
    _, groups, c, _ = w_st.shape
    gw = bias_st.shape[2]
    npc = n_prompt // c
    sel = lambda i: jnp.where(i >= npc, 1, 0)
    kern = functools.partial(_gmlp_kernel, groups=groups, gd=gw // groups)
    return pl.pallas_call(
        kern,
        grid=(n // c,),
        in_specs=[pl.BlockSpec((c, gw), lambda i: (i, gu_blk)),
                  pl.BlockSpec((c, gw), lambda i: (i, gv_blk)),
                  pl.BlockSpec((1, groups, c, c), lambda i: (sel(i), 0, 0, 0)),
                  pl.BlockSpec((1, c, gw), lambda i: (sel(i), 0, 0)),
                  pl.BlockSpec((1, gw), lambda i: (0, 0))],
        out_specs=[pl.BlockSpec((c, gw), lambda i: (i, 0)),
                   pl.BlockSpec((c, gw), lambda i: (jnp.maximum(i - npc, 0), 0))],
        out_shape=[jax.ShapeDtypeStruct((n, gw), BF16),
                   jax.ShapeDtypeStruct((n - n_prompt, gw), F32)],
        compiler_params=_params("arbitrary"),
        name="chunk_mlp",
    )(proj, proj, w_st, bias_st, g_v.reshape(1, gw))


def _branch_kernel(ap_ref, as_ref, b_ref, wa_ref, wb_ref, ga_ref, gb_ref, o_ref, *, n_prompt_tiles):
    a = jnp.where(pl.program_id(1) < n_prompt_tiles, ap_ref[...], as_ref[...].astype(BF16))
    ya = jnp.dot(a, wa_ref[...], preferred_element_type=F32)
    yb = jnp.dot(b_ref[...], wb_ref[...], preferred_element_type=F32)
    o_ref[...] = (jax.nn.sigmoid(ga_ref[...]) * ya + jax.nn.sigmoid(gb_ref[...]) * yb).astype(o_ref.dtype)


def _branches(a_p, a_s, bm, wa, wb, proj, ga_off, gb_off, n_prompt):
    n, kb = bm.shape
    ka = a_p.shape[1]
    d = wa.shape[1]
    tm = _pow2_tile(512, n_prompt, n - n_prompt)
    tn = _pow2_tile(1024, d, ga_off, gb_off)
    npt = n_prompt // tm
    return pl.pallas_call(
        functools.partial(_branch_kernel, n_prompt_tiles=npt),
        grid=(d // tn, n // tm),
        in_specs=[pl.BlockSpec((tm, ka), lambda j, i: (jnp.minimum(i, npt - 1), 0)),
                  pl.BlockSpec((tm, ka), lambda j, i: (jnp.maximum(i - npt, 0), 0)),
                  pl.BlockSpec((tm, kb), lambda j, i: (i, 0)),
                  pl.BlockSpec((ka, tn), lambda j, i: (0, j)),
                  pl.BlockSpec((kb, tn), lambda j, i: (0, j)),
                  pl.BlockSpec((tm, tn), lambda j, i: (i, ga_off // tn + j)),
                  pl.BlockSpec((tm, tn), lambda j, i: (i, gb_off // tn + j))],
        out_specs=pl.BlockSpec((tm, tn), lambda j, i: (i, j)),
        out_shape=jax.ShapeDtypeStruct((n, d), BF16),
        compiler_params=_params("arbitrary", "arbitrary"),
        name="branches",
    )(a_p, a_s, bm, wa, wb, proj, proj)


def _out_kernel(m_ref, w_ref, xp_ref, xs_ref, gp_ref, gs_ref, side_ref, o_ref, side_out_ref, *, n_prompt_tiles):
    i = pl.program_id(1)
    y = jnp.dot(m_ref[...], w_ref[...], preferred_element_type=F32)
    side_out_ref[...] = side_ref[...].astype(side_out_ref.dtype)

    @pl.when(i < n_prompt_tiles)
    def _():
        o_ref[...] = xp_ref[...] + gp_ref[0] * y

    @pl.when(i >= n_prompt_tiles)
    def _():
        o_ref[...] = xs_ref[...] + gs_ref[...] * y


def _out_proj(mix, w, x_p, x_s, mod_p, mod_s, k_gate, n_prompt, seq, side_table):
    n, d = mix.shape
    tm = _pow2_tile(512, seq, n - n_prompt)
    tn = _pow2_tile(1024, d)
    npt = n_prompt // tm
    per_b = seq // tm
    nj = d // tn
    side_specs, side_shapes = _side_cast_specs((side_table,), nj, n // tm)
    return pl.pallas_call(
        functools.partial(_out_kernel, n_prompt_tiles=npt),
        grid=(nj, n // tm),
        in_specs=[pl.BlockSpec((tm, d), lambda j, i: (i, 0)),
                  pl.BlockSpec((d, tn), lambda j, i: (0, j)),
                  pl.BlockSpec((tm, tn), lambda j, i: (jnp.minimum(i, npt - 1), j)),
                  pl.BlockSpec((tm, tn), lambda j, i: (jnp.maximum(i - npt, 0), j)),
                  pl.BlockSpec((1, 1, tn), lambda j, i: (jnp.minimum(i, npt - 1) // per_b, 0, k_gate * nj + j)),
                  pl.BlockSpec((tm, tn), lambda j, i: (jnp.maximum(i - npt, 0), k_gate * nj + j))] + side_specs,
        out_specs=[pl.BlockSpec((tm, tn), lambda j, i: (i, j))] + side_specs,
        out_shape=[jax.ShapeDtypeStruct((n, d), F32)] + side_shapes,
        compiler_params=_params("arbitrary", "arbitrary"),
        name="out_proj",
    )(mix, w, x_p, x_s, mod_p, mod_s, side_table)


def _cand_pairs(k):
    return [(i, j) for i in range(k) for j in range(k) if (i + 1) * (j + 1) <= k]


def _arg_rounds(s, iota, k, exact):
    rank = jnp.full(s.shape, float(k), F32)
    vals = []
    big = float(s.shape[0])
    for i in range(k):
        m = jnp.max(s, axis=0, keepdims=True)
        if exact:
            idx = jnp.min(jnp.where(s == m, iota, big), axis=0, keepdims=True)
            sel = iota == idx
        else:
            sel = s == m
        rank = jnp.where(sel, float(i), rank)
        s = jnp.where(sel, NEG_INF, s)
        vals.append(m)
    return vals, rank


def _peer_gates(s1, s2, jsum, cand_scr, cols, pairs, k, exact):
    n_iota = lax.broadcasted_iota(jnp.int32, s1.shape, 0).astype(F32)
    a, r1 = _arg_rounds(s1, n_iota, k, exact)
    b, r2 = _arg_rounds(s2, n_iota, k, exact)

    n_cand = cand_scr.shape[0]
    cand_scr[len(pairs):, cols] = jnp.full((n_cand - len(pairs), s1.shape[1]), NEG_INF, F32)
    for p, (i, j) in enumerate(pairs):
        cand_scr[p:p + 1, cols] = a[i] + b[j]
    c0 = cand_scr[:, cols]
    p_iota = lax.broadcasted_iota(jnp.int32, c0.shape, 0).astype(F32)
    _, prank = _arg_rounds(c0, p_iota, k, exact)
    taken = prank < float(k)
    w = jnp.where(taken, jnp.exp(c0 - (a[0] + b[0])), 0.0)
    inv_z = 1.0 / jnp.sum(w, axis=0, keepdims=True)
    ones = jnp.where(taken, 1.0, 0.0)
    jcnt = jnp.dot(jsum, ones.astype(BF16), preferred_element_type=F32)

    eaz = jnp.zeros(s1.shape, F32)
    jd = jnp.zeros(s1.shape, F32)
    eb = jnp.zeros(s1.shape, F32)
    for i in range(k):
        hit1 = r1 == float(i)
        eaz = jnp.where(hit1, jnp.exp(a[i] - a[0]) * inv_z, eaz)
        jd = jnp.where(hit1, jcnt[i:i + 1, :], jd)
        eb = jnp.where(r2 == float(i), jnp.exp(b[i] - b[0]), eb)

    def excess(rank):
        return jnp.abs(jnp.sum(jnp.where(rank < float(k), 1.0, 0.0), axis=0, keepdims=True) - float(k))

    return eaz, jd, eb, r2, excess(r1) + excess(r2) + excess(prank)


def _peer_topk_kernel(h_ref, wq_ref, sk_ref, jsum_ref, eaz_ref, jd_ref, eb_ref, r2_ref, cand_scr, wq_scr, *,
                      half, pairs, k):
    @pl.when(pl.program_id(1) == 0)
    def _():
        wq_scr[...] = wq_ref[...].astype(BF16)

    nt = (((1,), (1,)), ((), ()))
    q = jnp.dot(h_ref[...], wq_scr[...], preferred_element_type=F32).astype(BF16)
    s1 = lax.dot_general(sk_ref[0, 0], q[:, :half], nt, preferred_element_type=F32)
    s2 = lax.dot_general(sk_ref[0, 1], q[:, half:], nt, preferred_element_type=F32)
    jsum = jsum_ref[...]
    n_blocks = s1.shape[1] // LANE_V7X

    def run(exact):
        bad = None
        for cb in range(n_blocks):
            cols = slice(cb * LANE_V7X, (cb + 1) * LANE_V7X)
            eaz, jd, eb, r2, dev = _peer_gates(s1[:, cols], s2[:, cols], jsum, cand_scr, cols, pairs, k, exact)
            eaz_ref[0, cb] = eaz
            jd_ref[0, cb] = jd
            eb_ref[0, cb] = eb.astype(eb_ref.dtype)
            r2_ref[0, cb] = r2.astype(r2_ref.dtype)
            bad = dev if bad is None else bad + dev
        return jnp.max(bad)

    miscount = run(exact=False)

    @pl.when(miscount > 0.5)
    def _():
        run(exact=True)


def _peer_topk(h2, w_q, sub_keys_b, n_prompt):
    n, d = h2.shape
    heads, _, n_keys, half = sub_keys_b.shape
    tt = _pow2_tile(256, n_prompt, n - n_prompt)
    pairs = _cand_pairs(PEER_TOPK)
    n_cand = -(-len(pairs) // 64) * 64
    jsum = np.zeros((PEER_TOPK, n_cand), np.float32)
    for p, (i, _) in enumerate(pairs):
        jsum[i, p] = 1.0
    kern = functools.partial(_peer_topk_kernel, half=half, pairs=pairs, k=PEER_TOPK)
    maps = jax.ShapeDtypeStruct((heads, n // LANE_V7X, n_keys, LANE_V7X), F32)
    maps_b = jax.ShapeDtypeStruct((heads, n // LANE_V7X, n_keys, LANE_V7X), BF16)
    mspec = pl.BlockSpec((1, tt // LANE_V7X, n_keys, LANE_V7X), lambda h, i: (h, i, 0, 0))
    return pl.pallas_call(
        kern,
        grid=(heads, n // tt),
        in_specs=[pl.BlockSpec((tt, d), lambda h, i: (i, 0)),
                  pl.BlockSpec((d, 2 * half), lambda h, i: (0, h)),
                  pl.BlockSpec((1, 2, n_keys, half), lambda h, i: (h, 0, 0, 0)),
                  pl.BlockSpec(jsum.shape, lambda h, i: (0, 0))],
        out_specs=[mspec, mspec, mspec, mspec],
        out_shape=[maps, maps, maps_b, maps_b],
        scratch_shapes=[pltpu.VMEM((n_cand, tt), F32), pltpu.VMEM((d, 2 * half), BF16)],
        compiler_params=_params("arbitrary", "arbitrary"),
        name="peer_topk",
    )(h2, w_q, sub_keys_b, jnp.asarray(jsum, BF16))


def _peer_main_kernel(h_ref, u_ref, v_ref, eaz_ref, jd_ref, eb_ref, r2_ref, o_ref,
                      ga0_scr, ga1_scr, g0_scr, g1_scr, ht_scr, *, heads, n_keys, rows_per_step, n_steps):
    e = pl.program_id(1)
    n_tb = h_ref.shape[0] // LANE_V7X
    blocks = [(r, tb) for r in range(rows_per_step) for tb in range(n_tb)]
    n_slices = math.gcd(len(blocks), 16)

    def gates(chunk, g_scr, part=None):
        for r, tb in (blocks if part is None else blocks[part::n_slices]):
            n1 = chunk * rows_per_step + r
            g = None
            for h in range(heads):
                ea = eaz_ref[h, tb, pl.ds(n1, 1), :].astype(BF16)
                jd = jd_ref[h, tb, pl.ds(n1, 1), :].astype(BF16)
                t = ea * jnp.where(r2_ref[h, tb] < jd, eb_ref[h, tb], jnp.zeros((), BF16))
                g = t if g is None else g + t
            g_scr[r * n_tb + tb] = g

    @pl.when(e == 0)
    def _():
        o_ref[...] = jnp.zeros_like(o_ref)
        ga1_scr[...] = jnp.zeros_like(ga1_scr)
        gates(0, g0_scr)
        ht_scr[...] = h_ref[...].T

    def stage(ga_cur, ga_prev, g_cur, g_next):
        act_t = jnp.dot(u_ref[...], ht_scr[...], preferred_element_type=F32)
        dq = o_ref.shape[1] // n_slices
        for q in range(n_slices):
            qs = slice(q * dq, (q + 1) * dq)
            o_ref[:, qs] += jnp.dot(ga_prev[...], v_ref[:, qs], preferred_element_type=F32)
            for r, tb in blocks[q::n_slices]:
                sl = slice(r * n_keys, (r + 1) * n_keys)
                ts = slice(tb * LANE_V7X, (tb + 1) * LANE_V7X)
                ga_cur[ts, sl] = (g_cur[r * n_tb + tb] * jax.nn.gelu(act_t[sl, ts].astype(BF16))).T
            gates(jnp.minimum(e + 1, n_steps - 1), g_next, part=q)

    @pl.when(e % 2 == 0)
    def _():
        stage(ga0_scr, ga1_scr, g0_scr, g1_scr)

    @pl.when(e % 2 == 1)
    def _():
        stage(ga1_scr, ga0_scr, g1_scr, g0_scr)


def _peer_main(h2, u_b, v_b, maps, n_prompt):
    n, d = h2.shape
    n_exp = u_b.shape[0]
    heads, _, n_keys, lane = maps[0].shape
    tt = _pow2_tile(512, n_prompt, n - n_prompt)
    ec = _pow2_tile(512, n_exp)
    rps = ec // n_keys
    n_steps = n_exp // ec
    kern = functools.partial(_peer_main_kernel, heads=heads, n_keys=n_keys, rows_per_step=rps, n_steps=n_steps)
    once = pl.Buffered(1)
    mspec = pl.BlockSpec((heads, tt // lane, n_keys, lane), lambda i, e: (0, i, 0, 0), pipeline_mode=once)
    return pl.pallas_call(
        kern,
        grid=(n // tt, n_steps + 1),
        in_specs=[pl.BlockSpec((tt, d), lambda i, e: (i, 0), pipeline_mode=once),
                  pl.BlockSpec((ec, d), lambda i, e: (jnp.minimum(e, n_steps - 1), 0)),
                  pl.BlockSpec((ec, d), lambda i, e: (jnp.maximum(e - 1, 0), 0)),
                  mspec, mspec, mspec, mspec],
        out_specs=pl.BlockSpec((tt, d), lambda i, e: (i, 0), pipeline_mode=once),
        out_shape=jax.ShapeDtypeStruct((n, d), F32),
        scratch_shapes=[pltpu.VMEM((tt, ec), BF16), pltpu.VMEM((tt, ec), BF16),
                        pltpu.VMEM((rps * tt // lane, n_keys, lane), BF16),
                        pltpu.VMEM((rps * tt // lane, n_keys, lane), BF16),
                        pltpu.VMEM((d, tt), BF16)],
        compiler_params=_params("arbitrary", "arbitrary"),
        name="peer_main",
    )(h2, u_b, v_b, *maps)


def _final_kernel(x_ref, p_ref, gp_ref, gs_ref, g_ref, yp_ref, ys_ref, *, n_prompt_tiles):
    i = pl.program_id(0)

    def norm(x):
        return x * lax.rsqrt(jnp.mean(x * x, axis=-1, keepdims=True) + EPS) * g_ref[...]

    @pl.when(i < n_prompt_tiles)
    def _():
        yp_ref[...] = norm(x_ref[...] + gp_ref[0] * p_ref[...])

    @pl.when(i >= n_prompt_tiles)
    def _():
        ys_ref[...] = norm(x_ref[...] + gs_ref[...] * p_ref[...])


def _final(x1, peer_out, mod_p, mod_s, k_gate, g_final, n_prompt, seq):
    n, d = x1.shape
    tm = _pow2_tile(256, seq, n - n_prompt)
    npt = n_prompt // tm
    per_b = seq // tm
    return pl.pallas_call(
        functools.partial(_final_kernel, n_prompt_tiles=npt),
        grid=(n // tm,),
        in_specs=[pl.BlockSpec((tm, d), lambda i: (i, 0)),
                  pl.BlockSpec((tm, d), lambda i: (i, 0)),
                  pl.BlockSpec((1, 1, d), lambda i: (jnp.minimum(i, npt - 1) // per_b, 0, k_gate)),
                  pl.BlockSpec((tm, d), lambda i: (jnp.maximum(i - npt, 0), k_gate)),
                  pl.BlockSpec((1, d), lambda i: (0, 0))],
        out_specs=[pl.BlockSpec((tm, d), lambda i: (jnp.minimum(i, npt - 1), 0)),
                   pl.BlockSpec((tm, d), lambda i: (jnp.maximum(i - npt, 0), 0))],
        out_shape=[jax.ShapeDtypeStruct((n_prompt, d), F32),
                   jax.ShapeDtypeStruct((n - n_prompt, d), F32)],
        compiler_params=_params("arbitrary"),
        name="final_norm",
    )(x1, peer_out, mod_p, mod_s, g_final.reshape(1, d))


def kernel(x_prompt, x_sample, c_prompt, c_sample, state_hgrn, w_ada, b_ada, g_norm1, w_in, hgrn_lb_logits,
           g_hgrn_out, g_gmlp_v, w_spatial, b_spatial, w_branch_a, w_branch_b, w_out, g_norm2, w_peer_q,
           peer_sub_keys, peer_u, peer_v, g_final):
    batch, seq, d = x_prompt.shape
    db, dt, _ = x_sample.shape
    depth, _, heads, dk, dv = state_hgrn.shape
    assert depth == 1 and dk == dv == LANE_V7X
    hw = heads * dk
    groups, gc = w_spatial.shape[1], w_spatial.shape[2]
    gw = w_branch_b.shape[1]
    assert gc == LANE_V7X and gw // groups == LANE_V7X and dt <= gc and gc % dt == 0
    n_prompt, n_sample = batch * seq, db * dt
    n = n_prompt + n_sample

    c_rows = jnp.concatenate([jnp.repeat(c_sample, dt, axis=0), c_prompt], axis=0)
    c_rows = jnp.pad(c_rows, ((0, (-c_rows.shape[0]) % 16), (0, 0)))
    mod = _ada(c_rows, w_ada[0], b_ada[0])
    mod_s = mod
    mod_p = mod[n_sample:n_sample + batch].reshape(batch, 1, N_MOD * d)

    x_p = x_prompt.reshape(n_prompt, d)
    x_s = x_sample.reshape(n_sample, d)

    h1 = _norm_mod(x_p, x_s, g_norm1[0], mod_p, mod_s, 1, 0, n_prompt, n_sample, seq)
    proj, peer_u_b = _matmul(h1, w_in[0], F32, n_prompt, "in_proj", side_tables=(peer_u[0],))

    lb = jnp.cumsum(jax.nn.softmax(hgrn_lb_logits.astype(F32), axis=0), axis=0)[0].reshape(1, hw)
    g_ho = g_hgrn_out[0].reshape(1, hw)
    a_p, st_p = _hgrn_prompt(proj, lb, g_ho, batch, seq, heads, dk, n_prompt)
    a_s, st_s = _hgrn_sample(proj, lb, g_ho, state_hgrn, heads, dk, n_prompt, dt)

    tril = jnp.tril(jnp.ones((gc, gc), F32))
    w_sp = w_spatial[0]
    blk = jnp.arange(gc) // dt
    w_samp = jnp.tile(w_sp[:, :dt, :dt], (1, gc // dt, gc // dt)) * (blk[:, None] == blk[None, :])
    w_st = jnp.stack([w_sp * tril, w_samp * tril]).astype(BF16)
    bias_full = jnp.repeat(b_spatial[0].T, gw // groups, axis=1)
    bias_st = jnp.stack([bias_full, jnp.tile(bias_full[:dt], (gc // dt, 1))])
    gu_blk = 4 * hw // gw
    assert gu_blk * gw == 4 * hw
    bm, v_s = _gmlp(proj, w_st, bias_st, g_gmlp_v[0], n_prompt, gu_blk, gu_blk + 1)

    ga_off = 4 * hw + 2 * gw
    mix = _branches(a_p, a_s, bm, w_branch_a[0].astype(BF16), w_branch_b[0].astype(BF16), proj,
                    ga_off, ga_off + d, n_prompt)
    x1, peer_v_b = _out_proj(mix, w_out[0].astype(BF16), x_p, x_s, mod_p, mod_s, 2, n_prompt, seq, peer_v[0])

    h2 = _norm_mod(x1, x1, g_norm2[0], mod_p, mod_s, 4, 3, n_prompt, n_sample, seq)
    maps = _peer_topk(h2, w_peer_q[0], peer_sub_keys[0].astype(BF16), n_prompt)
    peer_out = _peer_main(h2, peer_u_b, peer_v_b, maps, n_prompt)
    y_p, y_s = _final(x1, peer_out, mod_p, mod_s, 5, g_final, n_prompt, seq)

    state_p = jnp.swapaxes(st_p, -1, -2)
    return (y_p.reshape(batch, seq, d), y_s.reshape(db, dt, d), state_p, st_s,
            v_s.reshape(1, db, dt, gw))
```

```python
import functools
import math

import numpy as np
import jax
import jax.numpy as jnp
from jax import lax
from jax.experimental import pallas as pl
from jax.experimental.pallas import tpu as pltpu

EPS = 1e-6
N_MOD = 6
PEER_TOPK = 16
LANE_V7X = 128
VMEM_LIMIT_V7X = 60 * 1024 * 1024
F32 = jnp.float32
BF16 = jnp.bfloat16
NEG_INF = float("-inf")


def _params(*sem):
    return pltpu.CompilerParams(dimension_semantics=sem, vmem_limit_bytes=VMEM_LIMIT_V7X)


def _pow2_tile(target, *sizes):
    g = 0
    for s in sizes:
        g = math.gcd(g, s)
    t = 1
    while t * 2 <= target and g % (t * 2) == 0:
        t *= 2
    return t


def _silu(x):
    return x * jax.nn.sigmoid(x)


def _ada_kernel(c_ref, w_ref, b_ref, o_ref, s_scr):
    @pl.when(pl.program_id(0) == 0)
    def _():
        s_scr[...] = _silu(c_ref[...]).astype(BF16)

    o_ref[...] = jnp.dot(s_scr[...], w_ref[...].astype(BF16), preferred_element_type=F32) + b_ref[...]


def _ada(c_rows, w_ada, b_ada):
    m, d = c_rows.shape
    n = w_ada.shape[1]
    tn = _pow2_tile(512, n)
    return pl.pallas_call(
        _ada_kernel,
        grid=(n // tn,),
        in_specs=[pl.BlockSpec((m, d), lambda j: (0, 0)),
                  pl.BlockSpec((d, tn), lambda j: (0, j)),
                  pl.BlockSpec((1, tn), lambda j: (0, j))],
        out_specs=pl.BlockSpec((m, tn), lambda j: (0, j)),
        out_shape=jax.ShapeDtypeStruct((m, n), F32),
        scratch_shapes=[pltpu.VMEM((m, d), BF16)],
        compiler_params=_params("arbitrary"),
        name="ada_mod",
    )(c_rows, w_ada, b_ada.reshape(1, n))


def _norm_mod_kernel(xp_ref, xs_ref, g_ref, scp_ref, shp_ref, scs_ref, shs_ref, o_ref, *, n_prompt_tiles):
    i = pl.program_id(0)

    def normed(x):
        return x * lax.rsqrt(jnp.mean(x * x, axis=-1, keepdims=True) + EPS) * g_ref[...]

    @pl.when(i < n_prompt_tiles)
    def _():
        o_ref[...] = (normed(xp_ref[...]) * (1.0 + scp_ref[0]) + shp_ref[0]).astype(o_ref.dtype)

    @pl.when(i >= n_prompt_tiles)
    def _():
        o_ref[...] = (normed(xs_ref[...]) * (1.0 + scs_ref[...]) + shs_ref[...]).astype(o_ref.dtype)


def _row_sources(x_p, x_s, n_prompt, tm):
    npt = n_prompt // tm
    s_off = npt if x_s is x_p else 0
    return (lambda i: (jnp.minimum(i, npt - 1), 0)), (lambda i: (s_off + jnp.maximum(i - npt, 0), 0))


def _norm_mod(x_p, x_s, g, mod_p, mod_s, k_scale, k_shift, n_prompt, n_sample, seq):
    d = x_p.shape[1]
    n = n_prompt + n_sample
    tm = _pow2_tile(256, seq, n_sample)
    npt = n_prompt // tm
    per_b = seq // tm
    p_map, s_map = _row_sources(x_p, x_s, n_prompt, tm)

    def pidx(k):
        return lambda i: (jnp.minimum(i, npt - 1) // per_b, 0, k)

    def sidx(k):
        return lambda i: (jnp.maximum(i - npt, 0), k)

    return pl.pallas_call(
        functools.partial(_norm_mod_kernel, n_prompt_tiles=npt),
        grid=(n // tm,),
        in_specs=[pl.BlockSpec((tm, d), p_map),
                  pl.BlockSpec((tm, d), s_map),
                  pl.BlockSpec((1, d), lambda i: (0, 0)),
                  pl.BlockSpec((1, 1, d), pidx(k_scale)),
                  pl.BlockSpec((1, 1, d), pidx(k_shift)),
                  pl.BlockSpec((tm, d), sidx(k_scale)),
                  pl.BlockSpec((tm, d), sidx(k_shift))],
        out_specs=pl.BlockSpec((tm, d), lambda i: (i, 0)),
        out_shape=jax.ShapeDtypeStruct((n, d), BF16),
        compiler_params=_params("arbitrary"),
        name="norm_mod",
    )(x_p, x_s, g.reshape(1, d), mod_p, mod_p, mod_s, mod_s)


def _mm_kernel(a_ref, w_ref, *rest, n_side):
    side_in, o_ref, side_out, wb_scr = rest[:n_side], rest[n_side], rest[n_side + 1:2 * n_side + 1], rest[-1]

    @pl.when(pl.program_id(1) == 0)
    def _():
        wb_scr[...] = w_ref[...].astype(BF16)

    o_ref[...] = jnp.dot(a_ref[...], wb_scr[...], preferred_element_type=F32).astype(o_ref.dtype)
    for src, dst in zip(side_in, side_out):
        dst[...] = src[...].astype(dst.dtype)


def _side_cast_specs(side_tables, nj, ni):
    specs, shapes = [], []
    for t in side_tables:
        rows = t.shape[0]
        rb = rows // _pow2_tile(nj * ni, rows)
        last = rows // rb - 1
        specs.append(pl.BlockSpec((rb, t.shape[1]), lambda j, i, last=last: (jnp.minimum(j * ni + i, last), 0)))
        shapes.append(jax.ShapeDtypeStruct(t.shape, BF16))
    return specs, shapes


def _matmul(a, w, out_dtype, n_prompt, name, side_tables=()):
    n, k = a.shape
    nc = w.shape[1]
    tm = _pow2_tile(512, n_prompt, n - n_prompt)
    tn = _pow2_tile(1024, nc)
    ni = n // tm
    side_specs, side_shapes = _side_cast_specs(side_tables, nc // tn, ni)
    outs = pl.pallas_call(
        functools.partial(_mm_kernel, n_side=len(side_tables)),
        grid=(nc // tn, ni),
        in_specs=[pl.BlockSpec((tm, k), lambda j, i: (i, 0)),
                  pl.BlockSpec((k, tn), lambda j, i: (0, j))] + side_specs,
        out_specs=[pl.BlockSpec((tm, tn), lambda j, i: (i, j))] + side_specs,
        out_shape=[jax.ShapeDtypeStruct((n, nc), out_dtype)] + side_shapes,
        scratch_shapes=[pltpu.VMEM((k, tn), BF16)],
        compiler_params=_params("arbitrary", "arbitrary"),
        name=name,
    )(a, w, *side_tables)
    return outs[0] if not side_tables else outs


def _hgrn_tables(c):
    nl = int(math.log2(c))
    r = np.arange(c)[:, None]
    j = np.arange(c)[None, :]
    mats = []
    lvl = np.full((c, c), -1, np.int32)
    for l in range(nl):
        m = c >> (l + 1)
        mid = (r // (2 * m)) * 2 * m + m
        upper = r >= mid
        t = np.where(upper, (j >= mid) & (j <= r), (j > r) & (j <= mid - 1))
        mats.append(t)
        same = (r // (2 * m)) == (j // (2 * m))
        lvl = np.where(same & upper & (j < mid), l, lvl)
    lvl = np.where(r == j, nl, lvl)
    mats.append(j <= r)
    mats.append(j > r)
    t_all = np.concatenate(mats, axis=0).astype(np.float32)
    t_all = np.concatenate([t_all, t_all], axis=1)
    return jnp.asarray(t_all, BF16), jnp.asarray(lvl), nl


def _hgrn_prompt_kernel(q_ref, f_ref, v_ref, og_ref, lb_ref, g_ref, tall_ref, lvl_ref,
                        a_ref, st_ref, s_scr, *, chunk, n_levels, n_chunks, heads_per_step, dk):
    c = chunk
    s_scr[...] = jnp.zeros_like(s_scr)
    lb = lb_ref[...]
    lvl = lvl_ref[...]
    rows = lax.broadcasted_iota(jnp.int32, (c, q_ref.shape[1]), 0)
    nt = (((1,), (1,)), ((), ()))

    def body(ci, carry):
        sl = pl.ds(pl.multiple_of(ci * c, c), c)
        qs = _silu(q_ref[sl, :])
        f = lb + (1.0 - lb) * jax.nn.sigmoid(f_ref[sl, :])
        k = 1.0 - f
        lf = jnp.log(f)
        v = v_ref[sl, :].astype(BF16)
        lf_hi = lf.astype(BF16)
        lf_lo = (lf - lf_hi.astype(F32)).astype(BF16)
        ex = jnp.dot(tall_ref[...], jnp.concatenate([lf_hi, lf_lo], axis=0), preferred_element_type=F32)
        qs_b = qs.astype(BF16)
        k_b = k.astype(BF16)
        ys = []
        for l in range(n_levels):
            m = c >> (l + 1)
            e_l = jnp.exp(ex[l * c:(l + 1) * c, :])
            ys.append((e_l * jnp.where((rows & m) != 0, qs, k)).astype(BF16))
        bcum = ex[n_levels * c:(n_levels + 1) * c, :]
        brev = ex[(n_levels + 1) * c:(n_levels + 2) * c, :]
        qh = (qs * jnp.exp(bcum)).astype(BF16)
        kh = (k * jnp.exp(brev)).astype(BF16)
        decay = jnp.exp(bcum[c - 1:c, :])
        outs = []
        for hh in range(heads_per_step):
            hs = slice(hh * dk, (hh + 1) * dk)
            scores = jnp.where(lvl == n_levels,
                               lax.dot_general(qs_b[:, hs], k_b[:, hs], nt, preferred_element_type=F32), 0.0)
            for l in range(n_levels):
                y = ys[l][:, hs]
                p = lax.dot_general(y, y, nt, preferred_element_type=F32)
                scores = scores + jnp.where(lvl == l, p, 0.0)
            st = s_scr[hh]
            o = jnp.dot(scores.astype(BF16), v[:, hs], preferred_element_type=F32)
            o = o + lax.dot_general(qh[:, hs], st.astype(BF16), nt, preferred_element_type=F32)
            kv_t = lax.dot_general(v[:, hs], kh[:, hs], (((0,), (0,)), ((), ())), preferred_element_type=F32)
            s_scr[hh] = st * decay[:, hs] + kv_t
            outs.append(o * lax.rsqrt(jnp.mean(o * o, axis=-1, keepdims=True) + EPS))
        on = jnp.concatenate(outs, axis=1) if heads_per_step > 1 else outs[0]
        a_ref[sl, :] = (on * g_ref[...] * _silu(og_ref[sl, :])).astype(a_ref.dtype)
        return carry

    lax.fori_loop(0, n_chunks, body, 0, unroll=4)
    st_ref[0, 0] = s_scr[...]


def _hgrn_prompt(proj, lb, g_out, batch, seq, heads, dk, n_rows):
    c = _pow2_tile(128, seq)
    hp = _pow2_tile(4, heads)
    t_all, lvl, nl = _hgrn_tables(c)
    w = hp * dk
    nh = heads // hp
    col = lambda off: (lambda b, h: (b, off + h))
    kern = functools.partial(_hgrn_prompt_kernel, chunk=c, n_levels=nl, n_chunks=seq // c,
                             heads_per_step=hp, dk=dk)
    return pl.pallas_call(
        kern,
        grid=(batch, nh),
        in_specs=[pl.BlockSpec((seq, w), col(0)),
                  pl.BlockSpec((seq, w), col(nh)),
                  pl.BlockSpec((seq, w), col(2 * nh)),
                  pl.BlockSpec((seq, w), col(3 * nh)),
                  pl.BlockSpec((1, w), lambda b, h: (0, h)),
                  pl.BlockSpec((1, w), lambda b, h: (0, h)),
                  pl.BlockSpec(t_all.shape, lambda b, h: (0, 0)),
                  pl.BlockSpec(lvl.shape, lambda b, h: (0, 0))],
        out_specs=[pl.BlockSpec((seq, w), lambda b, h: (b, h)),
                   pl.BlockSpec((1, 1, hp, dk, dk), lambda b, h: (0, b, h, 0, 0))],
        out_shape=[jax.ShapeDtypeStruct((n_rows, heads * dk), BF16),
                   jax.ShapeDtypeStruct((1, batch, heads, dk, dk), F32)],
        scratch_shapes=[pltpu.VMEM((hp, dk, dk), F32)],
        compiler_params=_params("arbitrary", "arbitrary"),
        name="hgrn_prompt",
    )(proj, proj, proj, proj, lb, g_out, t_all, lvl)


def _hgrn_sample_kernel(q_ref, f_ref, v_ref, og_ref, lb_ref, g_ref, s0_ref, a_ref, s_ref, *, bt, steps):
    sub = 8
    per_tile = sub // steps
    lb = lb_ref[...]
    g = g_ref[...]
    rows = lax.broadcasted_iota(jnp.int32, (sub, q_ref.shape[1]), 0)
    tpos = rows % steps
    grp = rows // steps

    def per_batch(x, t):
        out = x[t:t + 1, :]
        for j in range(1, per_tile):
            out = jnp.where(grp == j, x[j * steps + t:j * steps + t + 1, :], out)
        return out

    def body(p, carry):
        sl = pl.ds(pl.multiple_of(p * sub, sub), sub)
        qs = _silu(q_ref[sl, :])
        f = lb + (1.0 - lb) * jax.nn.sigmoid(f_ref[sl, :])
        k = 1.0 - f
        v = v_ref[sl, :]
        bmat = jnp.log(f)
        shift = 1
        while shift < steps:
            bmat = jnp.where(tpos >= shift, bmat + pltpu.roll(bmat, shift, axis=0), bmat)
            shift *= 2
        blast = per_batch(bmat, steps - 1)
        o = jnp.zeros_like(qs)
        for t in range(steps):
            x = jnp.where(tpos >= t, qs * per_batch(k, t) * jnp.exp(jnp.minimum(bmat - per_batch(bmat, t), 0.0)), 0.0)
            o = o + jnp.sum(x, axis=-1, keepdims=True) * per_batch(v, t)
        qh = (qs * jnp.exp(bmat)).astype(BF16)
        kh = k * jnp.exp(blast - bmat)
        vb = v.astype(BF16)
        for j in range(per_tile):
            b = p * per_tile + j
            s0 = s0_ref[0, b, 0]
            mine = grp == j
            o = o + jnp.where(mine, jnp.dot(qh, s0.astype(BF16), preferred_element_type=F32), 0.0)
            kv = lax.dot_general(jnp.where(mine, kh, 0.0).astype(BF16), vb, (((0,), (0,)), ((), ())),
                                 preferred_element_type=F32)
            decay_row = jnp.exp(bmat[(j + 1) * steps - 1:(j + 1) * steps, :])
            decay_col = jnp.broadcast_to(decay_row, s0.shape).T
            s_ref[0, b, 0] = s0 * decay_col + kv
        on = o * lax.rsqrt(jnp.mean(o * o, axis=-1, keepdims=True) + EPS)
        a_ref[sl, :] = on * g * _silu(og_ref[sl, :])
        return carry

    lax.fori_loop(0, bt // per_tile, body, 0, unroll=4)


def _hgrn_sample(proj, lb, g_out, s0, heads, dk, n_prompt, steps):
    db = s0.shape[1]
    bt = _pow2_tile(32, db)
    rb = bt * steps
    assert 8 % steps == 0 and n_prompt % rb == 0
    col = lambda off: (lambda j, h: (n_prompt // rb + j, off + h))
    kern = functools.partial(_hgrn_sample_kernel, bt=bt, steps=steps)
    return pl.pallas_call(
        kern,
        grid=(db // bt, heads),
        in_specs=[pl.BlockSpec((rb, dk), col(0)),
                  pl.BlockSpec((rb, dk), col(heads)),
                  pl.BlockSpec((rb, dk), col(2 * heads)),
                  pl.BlockSpec((rb, dk), col(3 * heads)),
                  pl.BlockSpec((1, dk), lambda j, h: (0, h)),
                  pl.BlockSpec((1, dk), lambda j, h: (0, h)),
                  pl.BlockSpec((1, bt, 1, dk, dk), lambda j, h: (0, j, h, 0, 0))],
        out_specs=[pl.BlockSpec((rb, dk), lambda j, h: (j, h)),
                   pl.BlockSpec((1, bt, 1, dk, dk), lambda j, h: (0, j, h, 0, 0))],
        out_shape=[jax.ShapeDtypeStruct((db * steps, heads * dk), F32),
                   jax.ShapeDtypeStruct(s0.shape, F32)],
        compiler_params=_params("arbitrary", "arbitrary"),
        name="hgrn_sample",
    )(proj, proj, proj, proj, lb, g_out, s0)


def _gmlp_kernel(gu_ref, gv_ref, w_ref, bias_ref, g_ref, o_ref, v_ref, *, groups, gd):
    u = jax.nn.gelu(gu_ref[...])
    vv = jax.nn.gelu(gv_ref[...])
    r = lax.rsqrt(jnp.mean(vv * vv, axis=-1, keepdims=True) + EPS)
    v = vv * r * g_ref[...]
    v_ref[...] = v
    for gi in range(groups):
        sl = slice(gi * gd, (gi + 1) * gd)
        z = jnp.dot(w_ref[0, gi], v[:, sl].astype(BF16), preferred_element_type=F32) + bias_ref[0, :, sl]
        o_ref[:, sl] = (u[:, sl] * z).astype(o_ref.dtype)


def _gmlp(proj, w_st, bias_st, g_v, n_prompt, gu_blk, gv_blk):
    n = proj.shape[0]
    _, groups, c, _ = w_st.shape
    gw = bias_st.shape[2]
    npc = n_prompt // c
    sel = lambda i: jnp.where(i >= npc, 1, 0)
    kern = functools.partial(_gmlp_kernel, groups=groups, gd=gw // groups)
    return pl.pallas_call(
        kern,
        grid=(n // c,),
        in_specs=[pl.BlockSpec((c, gw), lambda i: (i, gu_blk)),
                  pl.BlockSpec((c, gw), lambda i: (i, gv_blk)),
                  pl.BlockSpec((1, groups, c, c), lambda i: (sel(i), 0, 0, 0)),
                  pl.BlockSpec((1, c, gw), lambda i: (sel(i), 0, 0)),
                  pl.BlockSpec((1, gw), lambda i: (0, 0))],
        out_specs=[pl.BlockSpec((c, gw), lambda i: (i, 0)),
                   pl.BlockSpec((c, gw), lambda i: (jnp.maximum(i - npc, 0), 0))],
        out_shape=[jax.ShapeDtypeStruct((n, gw), BF16),
                   jax.ShapeDtypeStruct((n - n_prompt, gw), F32)],
        compiler_params=_params("arbitrary"),
        name="chunk_mlp",
    )(proj, proj, w_st, bias_st, g_v.reshape(1, gw))


def _branch_kernel(ap_ref, as_ref, b_ref, wa_ref, wb_ref, ga_ref, gb_ref, o_ref, *, n_prompt_tiles):
    a = jnp.where(pl.program_id(1) < n_prompt_tiles, ap_ref[...], as_ref[...].astype(BF16))
    ya = jnp.dot(a, wa_ref[...], preferred_element_type=F32)
    yb = jnp.dot(b_ref[...], wb_ref[...], preferred_element_type=F32)
    o_ref[...] = (jax.nn.sigmoid(ga_ref[...]) * ya + jax.nn.sigmoid(gb_ref[...]) * yb).astype(o_ref.dtype)


def _branches(a_p, a_s, bm, wa, wb, proj, ga_off, gb_off, n_prompt):
    n, kb = bm.shape
    ka = a_p.shape[1]
    d = wa.shape[1]
    tm = _pow2_tile(512, n_prompt, n - n_prompt)
    tn = _pow2_tile(1024, d, ga_off, gb_off)
    npt = n_prompt // tm
    return pl.pallas_call(
        functools.partial(_branch_kernel, n_prompt_tiles=npt),
        grid=(d // tn, n // tm),
        in_specs=[pl.BlockSpec((tm, ka), lambda j, i: (jnp.minimum(i, npt - 1), 0)),
                  pl.BlockSpec((tm, ka), lambda j, i: (jnp.maximum(i - npt, 0), 0)),
                  pl.BlockSpec((tm, kb), lambda j, i: (i, 0)),
                  pl.BlockSpec((ka, tn), lambda j, i: (0, j)),
                  pl.BlockSpec((kb, tn), lambda j, i: (0, j)),
                  pl.BlockSpec((tm, tn), lambda j, i: (i, ga_off // tn + j)),
                  pl.BlockSpec((tm, tn), lambda j, i: (i, gb_off // tn + j))],
        out_specs=pl.BlockSpec((tm, tn), lambda j, i: (i, j)),
        out_shape=jax.ShapeDtypeStruct((n, d), BF16),
        compiler_params=_params("arbitrary", "arbitrary"),
        name="branches",
    )(a_p, a_s, bm, wa, wb, proj, proj)


def _out_kernel(m_ref, w_ref, xp_ref, xs_ref, gp_ref, gs_ref, side_ref, o_ref, side_out_ref, *, n_prompt_tiles):
    i = pl.program_id(1)
    y = jnp.dot(m_ref[...], w_ref[...], preferred_element_type=F32)
    side_out_ref[...] = side_ref[...].astype(side_out_ref.dtype)

    @pl.when(i < n_prompt_tiles)
    def _():
        o_ref[...] = xp_ref[...] + gp_ref[0] * y

    @pl.when(i >= n_prompt_tiles)
    def _():
        o_ref[...] = xs_ref[...] + gs_ref[...] * y


def _out_proj(mix, w, x_p, x_s, mod_p, mod_s, k_gate, n_prompt, seq, side_table):
    n, d = mix.shape
    tm = _pow2_tile(512, seq, n - n_prompt)
    tn = _pow2_tile(1024, d)
    npt = n_prompt // tm
    per_b = seq // tm
    nj = d // tn
    side_specs, side_shapes = _side_cast_specs((side_table,), nj, n // tm)
    return pl.pallas_call(
        functools.partial(_out_kernel, n_prompt_tiles=npt),
        grid=(nj, n // tm),
        in_specs=[pl.BlockSpec((tm, d), lambda j, i: (i, 0)),
                  pl.BlockSpec((d, tn), lambda j, i: (0, j)),
                  pl.BlockSpec((tm, tn), lambda j, i: (jnp.minimum(i, npt - 1), j)),
                  pl.BlockSpec((tm, tn), lambda j, i: (jnp.maximum(i - npt, 0), j)),
                  pl.BlockSpec((1, 1, tn), lambda j, i: (jnp.minimum(i, npt - 1) // per_b, 0, k_gate * nj + j)),
                  pl.BlockSpec((tm, tn), lambda j, i: (jnp.maximum(i - npt, 0), k_gate * nj + j))] + side_specs,
        out_specs=[pl.BlockSpec((tm, tn), lambda j, i: (i, j))] + side_specs,
        out_shape=[jax.ShapeDtypeStruct((n, d), F32)] + side_shapes,
        compiler_params=_params("arbitrary", "arbitrary"),
        name="out_proj",
    )(mix, w, x_p, x_s, mod_p, mod_s, side_table)


def _cand_pairs(k):
    return [(i, j) for i in range(k) for j in range(k) if (i + 1) * (j + 1) <= k]


def _arg_rounds(s, iota, k, exact):
    rank = jnp.full(s.shape, float(k), F32)
    vals = []
    big = float(s.shape[0])
    for i in range(k):
        m = jnp.max(s, axis=0, keepdims=True)
        if exact:
            idx = jnp.min(jnp.where(s == m, iota, big), axis=0, keepdims=True)
            sel = iota == idx
        else:
            sel = s == m
        rank = jnp.where(sel, float(i), rank)
        s = jnp.where(sel, NEG_INF, s)
        vals.append(m)
    return vals, rank


def _peer_gates(s1, s2, jsum, cand_scr, cols, pairs, k, exact):
    n_iota = lax.broadcasted_iota(jnp.int32, s1.shape, 0).astype(F32)
    a, r1 = _arg_rounds(s1, n_iota, k, exact)
    b, r2 = _arg_rounds(s2, n_iota, k, exact)

    n_cand = cand_scr.shape[0]
    cand_scr[len(pairs):, cols] = jnp.full((n_cand - len(pairs), s1.shape[1]), NEG_INF, F32)
    for p, (i, j) in enumerate(pairs):
        cand_scr[p:p + 1, cols] = a[i] + b[j]
    c0 = cand_scr[:, cols]
    p_iota = lax.broadcasted_iota(jnp.int32, c0.shape, 0).astype(F32)
    _, prank = _arg_rounds(c0, p_iota, k, exact)
    taken = prank < float(k)
    w = jnp.where(taken, jnp.exp(c0 - (a[0] + b[0])), 0.0)
    inv_z = 1.0 / jnp.sum(w, axis=0, keepdims=True)
    ones = jnp.where(taken, 1.0, 0.0)
    jcnt = jnp.dot(jsum, ones.astype(BF16), preferred_element_type=F32)

    eaz = jnp.zeros(s1.shape, F32)
    jd = jnp.zeros(s1.shape, F32)
    eb = jnp.zeros(s1.shape, F32)
    for i in range(k):
        hit1 = r1 == float(i)
        eaz = jnp.where(hit1, jnp.exp(a[i] - a[0]) * inv_z, eaz)
        jd = jnp.where(hit1, jcnt[i:i + 1, :], jd)
        eb = jnp.where(r2 == float(i), jnp.exp(b[i] - b[0]), eb)

    def excess(rank):
        return jnp.abs(jnp.sum(jnp.where(rank < float(k), 1.0, 0.0), axis=0, keepdims=True) - float(k))

    return eaz, jd, eb, r2, excess(r1) + excess(r2) + excess(prank)


def _peer_topk_kernel(q_ref, sk_ref, jsum_ref, eaz_ref, jd_ref, eb_ref, r2_ref, cand_scr, *, half, pairs, k):
    nt = (((1,), (1,)), ((), ()))
    q = q_ref[...].astype(BF16)
    s1 = lax.dot_general(sk_ref[0, 0], q[:, :half], nt, preferred_element_type=F32)
    s2 = lax.dot_general(sk_ref[0, 1], q[:, half:], nt, preferred_element_type=F32)
    jsum = jsum_ref[...]
    n_blocks = s1.shape[1] // LANE_V7X

    def run(exact):
        bad = None
        for cb in range(n_blocks):
            cols = slice(cb * LANE_V7X, (cb + 1) * LANE_V7X)
            eaz, jd, eb, r2, dev = _peer_gates(s1[:, cols], s2[:, cols], jsum, cand_scr, cols, pairs, k, exact)
            eaz_ref[0, cb] = eaz
            jd_ref[0, cb] = jd
            eb_ref[0, cb] = eb.astype(eb_ref.dtype)
            r2_ref[0, cb] = r2.astype(r2_ref.dtype)
            bad = dev if bad is None else bad + dev
        return jnp.max(bad)

    miscount = run(exact=False)

    @pl.when(miscount > 0.5)
    def _():
        run(exact=True)


def _peer_topk(qp, sub_keys_b, n_prompt):
    n = qp.shape[0]
    heads, _, n_keys, half = sub_keys_b.shape
    tt = _pow2_tile(512, n_prompt, n - n_prompt)
    pairs = _cand_pairs(PEER_TOPK)
    n_cand = -(-len(pairs) // 64) * 64
    jsum = np.zeros((PEER_TOPK, n_cand), np.float32)
    for p, (i, _) in enumerate(pairs):
        jsum[i, p] = 1.0
    kern = functools.partial(_peer_topk_kernel, half=half, pairs=pairs, k=PEER_TOPK)
    maps = jax.ShapeDtypeStruct((heads, n // LANE_V7X, n_keys, LANE_V7X), F32)
    maps_b = jax.ShapeDtypeStruct((heads, n // LANE_V7X, n_keys, LANE_V7X), BF16)
    mspec = pl.BlockSpec((1, tt // LANE_V7X, n_keys, LANE_V7X), lambda i, h: (h, i, 0, 0))
    return pl.pallas_call(
        kern,
        grid=(n // tt, heads),
        in_specs=[pl.BlockSpec((tt, 2 * half), lambda i, h: (i, h)),
                  pl.BlockSpec((1, 2, n_keys, half), lambda i, h: (h, 0, 0, 0)),
                  pl.BlockSpec(jsum.shape, lambda i, h: (0, 0))],
        out_specs=[mspec, mspec, mspec, mspec],
        out_shape=[maps, maps, maps_b, maps_b],
        scratch_shapes=[pltpu.VMEM((n_cand, tt), F32)],
        compiler_params=_params("arbitrary", "arbitrary"),
        name="peer_topk",
    )(qp, sub_keys_b, jnp.asarray(jsum, BF16))


def _peer_main_kernel(h_ref, u_ref, v_ref, eaz_ref, jd_ref, eb_ref, r2_ref, o_ref,
                      ga0_scr, ga1_scr, g0_scr, g1_scr, ht_scr, *, heads, n_keys, rows_per_step, n_steps):
    e = pl.program_id(1)
    n_tb = h_ref.shape[0] // LANE_V7X
    blocks = [(r, tb) for r in range(rows_per_step) for tb in range(n_tb)]
    pair = 2 if rows_per_step % 2 == 0 else 1
    gate_items = [(r0, tb) for r0 in range(0, rows_per_step, pair) for tb in range(n_tb)]
    n_slices = math.gcd(len(gate_items), 16)

    def gates(chunk, g_scr, part=None):
        for r0, tb in (gate_items if part is None else gate_items[part::n_slices]):
            acc = [None] * pair
            for h in range(heads):
                r2 = r2_ref[h, tb]
                eb = eb_ref[h, tb]
                for kk in range(pair):
                    n1 = chunk * rows_per_step + r0 + kk
                    ea = eaz_ref[h, tb, pl.ds(n1, 1), :].astype(BF16)
                    jd = jd_ref[h, tb, pl.ds(n1, 1), :].astype(BF16)
                    t = ea * jnp.where(r2 < jd, eb, jnp.zeros((), BF16))
                    acc[kk] = t if acc[kk] is None else acc[kk] + t
            for kk in range(pair):
                g_scr[(r0 + kk) * n_tb + tb] = acc[kk]

    @pl.when(e == 0)
    def _():
        o_ref[...] = jnp.zeros_like(o_ref)
        ga1_scr[...] = jnp.zeros_like(ga1_scr)
        gates(0, g0_scr)
        ht_scr[...] = h_ref[...].T

    def stage(ga_cur, ga_prev, g_cur, g_next):
        act_t = jnp.dot(u_ref[...], ht_scr[...], preferred_element_type=F32)
        m_slices = math.gcd(len(blocks), 16)
        dq = o_ref.shape[1] // m_slices
        for q in range(m_slices):
            qs = slice(q * dq, (q + 1) * dq)
            o_ref[:, qs] += jnp.dot(ga_prev[...], v_ref[:, qs], preferred_element_type=F32)
            for r, tb in blocks[q::m_slices]:
                sl = slice(r * n_keys, (r + 1) * n_keys)
                ts = slice(tb * LANE_V7X, (tb + 1) * LANE_V7X)
                ga_cur[ts, sl] = (g_cur[r * n_tb + tb] * jax.nn.gelu(act_t[sl, ts].astype(BF16))).T
            if q % (m_slices // n_slices) == 0:
                gates(jnp.minimum(e + 1, n_steps - 1), g_next, part=q // (m_slices // n_slices))

    @pl.when(e % 2 == 0)
    def _():
        stage(ga0_scr, ga1_scr, g0_scr, g1_scr)

    @pl.when(e % 2 == 1)
    def _():
        stage(ga1_scr, ga0_scr, g1_scr, g0_scr)


def _peer_main(h2, u_b, v_b, maps, n_prompt):
    n, d = h2.shape
    n_exp = u_b.shape[0]
    heads, _, n_keys, lane = maps[0].shape
    tt = _pow2_tile(512, n_prompt, n - n_prompt)
    ec = _pow2_tile(512, n_exp)
    rps = ec // n_keys
    n_steps = n_exp // ec
    kern = functools.partial(_peer_main_kernel, heads=heads, n_keys=n_keys, rows_per_step=rps, n_steps=n_steps)
    once = pl.Buffered(1)
    mspec = pl.BlockSpec((heads, tt // lane, n_keys, lane), lambda i, e: (0, i, 0, 0), pipeline_mode=once)
    return pl.pallas_call(
        kern,
        grid=(n // tt, n_steps + 1),
        in_specs=[pl.BlockSpec((tt, d), lambda i, e: (i, 0), pipeline_mode=once),
                  pl.BlockSpec((ec, d), lambda i, e: (jnp.minimum(e, n_steps - 1), 0)),
                  pl.BlockSpec((ec, d), lambda i, e: (jnp.maximum(e - 1, 0), 0)),
                  mspec, mspec, mspec, mspec],
        out_specs=pl.BlockSpec((tt, d), lambda i, e: (i, 0), pipeline_mode=once),
        out_shape=jax.ShapeDtypeStruct((n, d), F32),
        scratch_shapes=[pltpu.VMEM((tt, ec), BF16), pltpu.VMEM((tt, ec), BF16),
                        pltpu.VMEM((rps * tt // lane, n_keys, lane), BF16),
                        pltpu.VMEM((rps * tt // lane, n_keys, lane), BF16),
                        pltpu.VMEM((d, tt), BF16)],
        compiler_params=_params("arbitrary", "arbitrary"),
        name="peer_main",
    )(h2, u_b, v_b, *maps)


def _final_kernel(x_ref, p_ref, gp_ref, gs_ref, g_ref, yp_ref, ys_ref, *, n_prompt_tiles):
    i = pl.program_id(0)

    def norm(x):
        return x * lax.rsqrt(jnp.mean(x * x, axis=-1, keepdims=True) + EPS) * g_ref[...]

    @pl.when(i < n_prompt_tiles)
    def _():
        yp_ref[...] = norm(x_ref[...] + gp_ref[0] * p_ref[...])

    @pl.when(i >= n_prompt_tiles)
    def _():
        ys_ref[...] = norm(x_ref[...] + gs_ref[...] * p_ref[...])


def _final(x1, peer_out, mod_p, mod_s, k_gate, g_final, n_prompt, seq):
    n, d = x1.shape
    tm = _pow2_tile(256, seq, n - n_prompt)
    npt = n_prompt // tm
    per_b = seq // tm
    return pl.pallas_call(
        functools.partial(_final_kernel, n_prompt_tiles=npt),
        grid=(n // tm,),
        in_specs=[pl.BlockSpec((tm, d), lambda i: (i, 0)),
                  pl.BlockSpec((tm, d), lambda i: (i, 0)),
                  pl.BlockSpec((1, 1, d), lambda i: (jnp.minimum(i, npt - 1) // per_b, 0, k_gate)),
                  pl.BlockSpec((tm, d), lambda i: (jnp.maximum(i - npt, 0), k_gate)),
                  pl.BlockSpec((1, d), lambda i: (0, 0))],
        out_specs=[pl.BlockSpec((tm, d), lambda i: (jnp.minimum(i, npt - 1), 0)),
                   pl.BlockSpec((tm, d), lambda i: (jnp.maximum(i - npt, 0), 0))],
        out_shape=[jax.ShapeDtypeStruct((n_prompt, d), F32),
                   jax.ShapeDtypeStruct((n - n_prompt, d), F32)],
        compiler_params=_params("arbitrary"),
        name="final_norm",
    )(x1, peer_out, mod_p, mod_s, g_final.reshape(1, d))


def kernel(x_prompt, x_sample, c_prompt, c_sample, state_hgrn, w_ada, b_ada, g_norm1, w_in, hgrn_lb_logits,
           g_hgrn_out, g_gmlp_v, w_spatial, b_spatial, w_branch_a, w_branch_b, w_out, g_norm2, w_peer_q,
           peer_sub_keys, peer_u, peer_v, g_final):
    batch, seq, d = x_prompt.shape
    db, dt, _ = x_sample.shape
    depth, _, heads, dk, dv = state_hgrn.shape
    assert depth == 1 and dk == dv == LANE_V7X
    hw = heads * dk
    groups, gc = w_spatial.shape[1], w_spatial.shape[2]
    gw = w_branch_b.shape[1]
    assert gc == LANE_V7X and gw // groups == LANE_V7X and dt <= gc and gc % dt == 0
    n_prompt, n_sample = batch * seq, db * dt
    n = n_prompt + n_sample

    c_rows = jnp.concatenate([jnp.repeat(c_sample, dt, axis=0), c_prompt], axis=0)
    c_rows = jnp.pad(c_rows, ((0, (-c_rows.shape[0]) % 16), (0, 0)))
    mod = _ada(c_rows, w_ada[0], b_ada[0])
    mod_s = mod
    mod_p = mod[n_sample:n_sample + batch].reshape(batch, 1, N_MOD * d)

    x_p = x_prompt.reshape(n_prompt, d)
    x_s = x_sample.reshape(n_sample, d)

    h1 = _norm_mod(x_p, x_s, g_norm1[0], mod_p, mod_s, 1, 0, n_prompt, n_sample, seq)
    proj, peer_u_b = _matmul(h1, w_in[0], F32, n_prompt, "in_proj", side_tables=(peer_u[0],))

    lb = jnp.cumsum(jax.nn.softmax(hgrn_lb_logits.astype(F32), axis=0), axis=0)[0].reshape(1, hw)
    g_ho = g_hgrn_out[0].reshape(1, hw)
    a_p, st_p = _hgrn_prompt(proj, lb, g_ho, batch, seq, heads, dk, n_prompt)
    a_s, st_s = _hgrn_sample(proj, lb, g_ho, state_hgrn, heads, dk, n_prompt, dt)

    tril = jnp.tril(jnp.ones((gc, gc), F32))
    w_sp = w_spatial[0]
    blk = jnp.arange(gc) // dt
    w_samp = jnp.tile(w_sp[:, :dt, :dt], (1, gc // dt, gc // dt)) * (blk[:, None] == blk[None, :])
    w_st = jnp.stack([w_sp * tril, w_samp * tril]).astype(BF16)
    bias_full = jnp.repeat(b_spatial[0].T, gw // groups, axis=1)
    bias_st = jnp.stack([bias_full, jnp.tile(bias_full[:dt], (gc // dt, 1))])
    gu_blk = 4 * hw // gw
    assert gu_blk * gw == 4 * hw
    bm, v_s = _gmlp(proj, w_st, bias_st, g_gmlp_v[0], n_prompt, gu_blk, gu_blk + 1)

    ga_off = 4 * hw + 2 * gw
    mix = _branches(a_p, a_s, bm, w_branch_a[0].astype(BF16), w_branch_b[0].astype(BF16), proj,
                    ga_off, ga_off + d, n_prompt)
    x1, peer_v_b = _out_proj(mix, w_out[0].astype(BF16), x_p, x_s, mod_p, mod_s, 2, n_prompt, seq, peer_v[0])

    h2 = _norm_mod(x1, x1, g_norm2[0], mod_p, mod_s, 4, 3, n_prompt, n_sample, seq)
    qp = _matmul(h2, w_peer_q[0], F32, n_prompt, "peer_query")
    maps = _peer_topk(qp, peer_sub_keys[0].astype(BF16), n_prompt)
    peer_out = _peer_main(h2, peer_u_b, peer_v_b, maps, n_prompt)
    y_p, y_s = _final(x1, peer_out, mod_p, mod_s, 5, g_final, n_prompt, seq)

    state_p = jnp.swapaxes(st_p, -1, -2)
    return (y_p.reshape(batch, seq, d), y_s.reshape(db, dt, d), state_p, st_s,
            v_s.reshape(1, db, dt, gw))
```

```python
import functools
import math

import numpy as np
import jax
import jax.numpy as jnp
from jax import lax
from jax.experimental import pallas as pl
from jax.experimental.pallas import tpu as pltpu

EPS = 1e-6
N_MOD = 6
PEER_TOPK = 16
LANE_V7X = 128
VMEM_LIMIT_V7X = 60 * 1024 * 1024
F32 = jnp.float32
BF16 = jnp.bfloat16
NEG_INF = float("-inf")


def _params(*sem):
    return pltpu.CompilerParams(dimension_semantics=sem, vmem_limit_bytes=VMEM_LIMIT_V7X)


def _pow2_tile(target, *sizes):
    g = 0
    for s in sizes:
        g = math.gcd(g, s)
    t = 1
    while t * 2 <= target and g % (t * 2) == 0:
        t *= 2
    return t


def _silu(x):
    return x * jax.nn.sigmoid(x)


def _ada_kernel(c_ref, w_ref, b_ref, o_ref, s_scr):
    @pl.when(pl.program_id(0) == 0)
    def _():
        s_scr[...] = _silu(c_ref[...]).astype(BF16)

    o_ref[...] = jnp.dot(s_scr[...], w_ref[...].astype(BF16), preferred_element_type=F32) + b_ref[...]


def _ada(c_rows, w_ada, b_ada):
    m, d = c_rows.shape
    n = w_ada.shape[1]
    tn = _pow2_tile(512, n)
    return pl.pallas_call(
        _ada_kernel,
        grid=(n // tn,),
        in_specs=[pl.BlockSpec((m, d), lambda j: (0, 0)),
                  pl.BlockSpec((d, tn), lambda j: (0, j)),
                  pl.BlockSpec((1, tn), lambda j: (0, j))],
        out_specs=pl.BlockSpec((m, tn), lambda j: (0, j)),
        out_shape=jax.ShapeDtypeStruct((m, n), F32),
        scratch_shapes=[pltpu.VMEM((m, d), BF16)],
        compiler_params=_params("arbitrary"),
        name="ada_mod",
    )(c_rows, w_ada, b_ada.reshape(1, n))


def _norm_mod_kernel(xp_ref, xs_ref, g_ref, scp_ref, shp_ref, scs_ref, shs_ref, o_ref, *, n_prompt_tiles):
    i = pl.program_id(0)

    def normed(x):
        return x * lax.rsqrt(jnp.mean(x * x, axis=-1, keepdims=True) + EPS) * g_ref[...]

    @pl.when(i < n_prompt_tiles)
    def _():
        o_ref[...] = (normed(xp_ref[...]) * (1.0 + scp_ref[0]) + shp_ref[0]).astype(o_ref.dtype)

    @pl.when(i >= n_prompt_tiles)
    def _():
        o_ref[...] = (normed(xs_ref[...]) * (1.0 + scs_ref[...]) + shs_ref[...]).astype(o_ref.dtype)


def _row_sources(x_p, x_s, n_prompt, tm):
    npt = n_prompt // tm
    s_off = npt if x_s is x_p else 0
    return (lambda i: (jnp.minimum(i, npt - 1), 0)), (lambda i: (s_off + jnp.maximum(i - npt, 0), 0))


def _norm_mod(x_p, x_s, g, mod_p, mod_s, k_scale, k_shift, n_prompt, n_sample, seq):
    d = x_p.shape[1]
    n = n_prompt + n_sample
    tm = _pow2_tile(256, seq, n_sample)
    npt = n_prompt // tm
    per_b = seq // tm
    p_map, s_map = _row_sources(x_p, x_s, n_prompt, tm)

    def pidx(k):
        return lambda i: (jnp.minimum(i, npt - 1) // per_b, 0, k)

    def sidx(k):
        return lambda i: (jnp.maximum(i - npt, 0), k)

    return pl.pallas_call(
        functools.partial(_norm_mod_kernel, n_prompt_tiles=npt),
        grid=(n // tm,),
        in_specs=[pl.BlockSpec((tm, d), p_map),
                  pl.BlockSpec((tm, d), s_map),
                  pl.BlockSpec((1, d), lambda i: (0, 0)),
                  pl.BlockSpec((1, 1, d), pidx(k_scale)),
                  pl.BlockSpec((1, 1, d), pidx(k_shift)),
                  pl.BlockSpec((tm, d), sidx(k_scale)),
                  pl.BlockSpec((tm, d), sidx(k_shift))],
        out_specs=pl.BlockSpec((tm, d), lambda i: (i, 0)),
        out_shape=jax.ShapeDtypeStruct((n, d), BF16),
        compiler_params=_params("arbitrary"),
        name="norm_mod",
    )(x_p, x_s, g.reshape(1, d), mod_p, mod_p, mod_s, mod_s)


def _mm_kernel(a_ref, w_ref, *rest, n_side):
    side_in, o_ref, side_out, wb_scr = rest[:n_side], rest[n_side], rest[n_side + 1:2 * n_side + 1], rest[-1]

    @pl.when(pl.program_id(1) == 0)
    def _():
        wb_scr[...] = w_ref[...].astype(BF16)

    o_ref[...] = jnp.dot(a_ref[...], wb_scr[...], preferred_element_type=F32).astype(o_ref.dtype)
    for src, dst in zip(side_in, side_out):
        dst[...] = src[...].astype(dst.dtype)


def _side_cast_specs(side_tables, nj, ni):
    specs, shapes = [], []
    for t in side_tables:
        rows = t.shape[0]
        rb = rows // _pow2_tile(nj * ni, rows)
        last = rows // rb - 1
        specs.append(pl.BlockSpec((rb, t.shape[1]), lambda j, i, last=last: (jnp.minimum(j * ni + i, last), 0)))
        shapes.append(jax.ShapeDtypeStruct(t.shape, BF16))
    return specs, shapes


def _matmul(a, w, out_dtype, n_prompt, name, side_tables=()):
    n, k = a.shape
    nc = w.shape[1]
    tm = _pow2_tile(512, n_prompt, n - n_prompt)
    tn = _pow2_tile(1024, nc)
    ni = n // tm
    side_specs, side_shapes = _side_cast_specs(side_tables, nc // tn, ni)
    outs = pl.pallas_call(
        functools.partial(_mm_kernel, n_side=len(side_tables)),
        grid=(nc // tn, ni),
        in_specs=[pl.BlockSpec((tm, k), lambda j, i: (i, 0)),
                  pl.BlockSpec((k, tn), lambda j, i: (0, j))] + side_specs,
        out_specs=[pl.BlockSpec((tm, tn), lambda j, i: (i, j))] + side_specs,
        out_shape=[jax.ShapeDtypeStruct((n, nc), out_dtype)] + side_shapes,
        scratch_shapes=[pltpu.VMEM((k, tn), BF16)],
        compiler_params=_params("arbitrary", "arbitrary"),
        name=name,
    )(a, w, *side_tables)
    return outs[0] if not side_tables else outs


def _hgrn_tables(c):
    nl = int(math.log2(c))
    r = np.arange(c)[:, None]
    j = np.arange(c)[None, :]
    mats = []
    lvl = np.full((c, c), -1, np.int32)
    for l in range(nl):
        m = c >> (l + 1)
        mid = (r // (2 * m)) * 2 * m + m
        upper = r >= mid
        t = np.where(upper, (j >= mid) & (j <= r), (j > r) & (j <= mid - 1))
        mats.append(t)
        same = (r // (2 * m)) == (j // (2 * m))
        lvl = np.where(same & upper & (j < mid), l, lvl)
    lvl = np.where(r == j, nl, lvl)
    mats.append(j <= r)
    mats.append(j > r)
    t_all = np.concatenate(mats, axis=0).astype(np.float32)
    t_all = np.concatenate([t_all, t_all], axis=1)
    return jnp.asarray(t_all, BF16), jnp.asarray(lvl), nl


def _hgrn_prompt_kernel(q_ref, f_ref, v_ref, og_ref, lb_ref, g_ref, tall_ref, lvl_ref,
                        a_ref, st_ref, s_scr, *, chunk, n_levels, n_chunks, heads_per_step, dk):
    c = chunk
    s_scr[...] = jnp.zeros_like(s_scr)
    lb = lb_ref[...]
    lvl = lvl_ref[...]
    rows = lax.broadcasted_iota(jnp.int32, (c, q_ref.shape[1]), 0)
    nt = (((1,), (1,)), ((), ()))

    def body(ci, carry):
        sl = pl.ds(pl.multiple_of(ci * c, c), c)
        qs = _silu(q_ref[sl, :])
        f = lb + (1.0 - lb) * jax.nn.sigmoid(f_ref[sl, :])
        k = 1.0 - f
        lf = jnp.log(f)
        v = v_ref[sl, :].astype(BF16)
        lf_hi = lf.astype(BF16)
        lf_lo = (lf - lf_hi.astype(F32)).astype(BF16)
        ex = jnp.dot(tall_ref[...], jnp.concatenate([lf_hi, lf_lo], axis=0), preferred_element_type=F32)
        qs_b = qs.astype(BF16)
        k_b = k.astype(BF16)
        ys = []
        for l in range(n_levels):
            m = c >> (l + 1)
            e_l = jnp.exp(ex[l * c:(l + 1) * c, :])
            ys.append((e_l * jnp.where((rows & m) != 0, qs, k)).astype(BF16))
        bcum = ex[n_levels * c:(n_levels + 1) * c, :]
        brev = ex[(n_levels + 1) * c:(n_levels + 2) * c, :]
        qh = (qs * jnp.exp(bcum)).astype(BF16)
        kh = (k * jnp.exp(brev)).astype(BF16)
        decay = jnp.exp(bcum[c - 1:c, :])
        outs = []
        for hh in range(heads_per_step):
            hs = slice(hh * dk, (hh + 1) * dk)
            scores = jnp.where(lvl == n_levels,
                               lax.dot_general(qs_b[:, hs], k_b[:, hs], nt, preferred_element_type=F32), 0.0)
            for l in range(n_levels):
                y = ys[l][:, hs]
                p = lax.dot_general(y, y, nt, preferred_element_type=F32)
                scores = scores + jnp.where(lvl == l, p, 0.0)
            st = s_scr[hh]
            o = jnp.dot(scores.astype(BF16), v[:, hs], preferred_element_type=F32)
            o = o + lax.dot_general(qh[:, hs], st.astype(BF16), nt, preferred_element_type=F32)
            kv_t = lax.dot_general(v[:, hs], kh[:, hs], (((0,), (0,)), ((), ())), preferred_element_type=F32)
            s_scr[hh] = st * decay[:, hs] + kv_t
            outs.append(o * lax.rsqrt(jnp.mean(o * o, axis=-1, keepdims=True) + EPS))
        on = jnp.concatenate(outs, axis=1) if heads_per_step > 1 else outs[0]
        a_ref[sl, :] = (on * g_ref[...] * _silu(og_ref[sl, :])).astype(a_ref.dtype)
        return carry

    lax.fori_loop(0, n_chunks, body, 0, unroll=4)
    st_ref[0, 0] = s_scr[...]


def _hgrn_prompt(proj, lb, g_out, batch, seq, heads, dk, n_rows):
    c = _pow2_tile(128, seq)
    hp = _pow2_tile(4, heads)
    t_all, lvl, nl = _hgrn_tables(c)
    w = hp * dk
    nh = heads // hp
    col = lambda off: (lambda b, h: (b, off + h))
    kern = functools.partial(_hgrn_prompt_kernel, chunk=c, n_levels=nl, n_chunks=seq // c,
                             heads_per_step=hp, dk=dk)
    return pl.pallas_call(
        kern,
        grid=(batch, nh),
        in_specs=[pl.BlockSpec((seq, w), col(0)),
                  pl.BlockSpec((seq, w), col(nh)),
                  pl.BlockSpec((seq, w), col(2 * nh)),
                  pl.BlockSpec((seq, w), col(3 * nh)),
                  pl.BlockSpec((1, w), lambda b, h: (0, h)),
                  pl.BlockSpec((1, w), lambda b, h: (0, h)),
                  pl.BlockSpec(t_all.shape, lambda b, h: (0, 0)),
                  pl.BlockSpec(lvl.shape, lambda b, h: (0, 0))],
        out_specs=[pl.BlockSpec((seq, w), lambda b, h: (b, h)),
                   pl.BlockSpec((1, 1, hp, dk, dk), lambda b, h: (0, b, h, 0, 0))],
        out_shape=[jax.ShapeDtypeStruct((n_rows, heads * dk), BF16),
                   jax.ShapeDtypeStruct((1, batch, heads, dk, dk), F32)],
        scratch_shapes=[pltpu.VMEM((hp, dk, dk), F32)],
        compiler_params=_params("arbitrary", "arbitrary"),
        name="hgrn_prompt",
    )(proj, proj, proj, proj, lb, g_out, t_all, lvl)


def _hgrn_sample_kernel(q_ref, f_ref, v_ref, og_ref, lb_ref, g_ref, s0_ref, a_ref, s_ref, *, bt, steps):
    sub = 8
    per_tile = sub // steps
    lb = lb_ref[...]
    g = g_ref[...]
    rows = lax.broadcasted_iota(jnp.int32, (sub, q_ref.shape[1]), 0)
    tpos = rows % steps
    grp = rows // steps

    def per_batch(x, t):
        out = x[t:t + 1, :]
        for j in range(1, per_tile):
            out = jnp.where(grp == j, x[j * steps + t:j * steps + t + 1, :], out)
        return out

    def body(p, carry):
        sl = pl.ds(pl.multiple_of(p * sub, sub), sub)
        qs = _silu(q_ref[sl, :])
        f = lb + (1.0 - lb) * jax.nn.sigmoid(f_ref[sl, :])
        k = 1.0 - f
        v = v_ref[sl, :]
        bmat = jnp.log(f)
        shift = 1
        while shift < steps:
            bmat = jnp.where(tpos >= shift, bmat + pltpu.roll(bmat, shift, axis=0), bmat)
            shift *= 2
        blast = per_batch(bmat, steps - 1)
        o = jnp.zeros_like(qs)
        for t in range(steps):
            x = jnp.where(tpos >= t, qs * per_batch(k, t) * jnp.exp(jnp.minimum(bmat - per_batch(bmat, t), 0.0)), 0.0)
            o = o + jnp.sum(x, axis=-1, keepdims=True) * per_batch(v, t)
        qh = (qs * jnp.exp(bmat)).astype(BF16)
        kh = k * jnp.exp(blast - bmat)
        vb = v.astype(BF16)
        for j in range(per_tile):
            b = p * per_tile + j
            s0 = s0_ref[0, b, 0]
            mine = grp == j
            o = o + jnp.where(mine, jnp.dot(qh, s0.astype(BF16), preferred_element_type=F32), 0.0)
            kv = lax.dot_general(jnp.where(mine, kh, 0.0).astype(BF16), vb, (((0,), (0,)), ((), ())),
                                 preferred_element_type=F32)
            decay_row = jnp.exp(bmat[(j + 1) * steps - 1:(j + 1) * steps, :])
            decay_col = jnp.broadcast_to(decay_row, s0.shape).T
            s_ref[0, b, 0] = s0 * decay_col + kv
        on = o * lax.rsqrt(jnp.mean(o * o, axis=-1, keepdims=True) + EPS)
        a_ref[sl, :] = on * g * _silu(og_ref[sl, :])
        return carry

    lax.fori_loop(0, bt // per_tile, body, 0, unroll=4)


def _hgrn_sample(proj, lb, g_out, s0, heads, dk, n_prompt, steps):
    db = s0.shape[1]
    bt = _pow2_tile(32, db)
    rb = bt * steps
    assert 8 % steps == 0 and n_prompt % rb == 0
    col = lambda off: (lambda j, h: (n_prompt // rb + j, off + h))
    kern = functools.partial(_hgrn_sample_kernel, bt=bt, steps=steps)
    return pl.pallas_call(
        kern,
        grid=(db // bt, heads),
        in_specs=[pl.BlockSpec((rb, dk), col(0)),
                  pl.BlockSpec((rb, dk), col(heads)),
                  pl.BlockSpec((rb, dk), col(2 * heads)),
                  pl.BlockSpec((rb, dk), col(3 * heads)),
                  pl.BlockSpec((1, dk), lambda j, h: (0, h)),
                  pl.BlockSpec((1, dk), lambda j, h: (0, h)),
                  pl.BlockSpec((1, bt, 1, dk, dk), lambda j, h: (0, j, h, 0, 0))],
        out_specs=[pl.BlockSpec((rb, dk), lambda j, h: (j, h)),
                   pl.BlockSpec((1, bt, 1, dk, dk), lambda j, h: (0, j, h, 0, 0))],
        out_shape=[jax.ShapeDtypeStruct((db * steps, heads * dk), F32),
                   jax.ShapeDtypeStruct(s0.shape, F32)],
        compiler_params=_params("arbitrary", "arbitrary"),
        name="hgrn_sample",
    )(proj, proj, proj, proj, lb, g_out, s0)


def _gmlp_kernel(gu_ref, gv_ref, w_ref, bias_ref, g_ref, o_ref, v_ref, *, groups, gd):
    u = jax.nn.gelu(gu_ref[...])
    vv = jax.nn.gelu(gv_ref[...])
    r = lax.rsqrt(jnp.mean(vv * vv, axis=-1, keepdims=True) + EPS)
    v = vv * r * g_ref[...]
    v_ref[...] = v
    for gi in range(groups):
        sl = slice(gi * gd, (gi + 1) * gd)
        z = jnp.dot(w_ref[0, gi], v[:, sl].astype(BF16), preferred_element_type=F32) + bias_ref[0, :, sl]
        o_ref[:, sl] = (u[:, sl] * z).astype(o_ref.dtype)


def _gmlp(proj, w_st, bias_st, g_v, n_prompt, gu_blk, gv_blk):
    n = proj.shape[0]
    _, groups, c, _ = w_st.shape
    gw = bias_st.shape[2]
    npc = n_prompt // c
    sel = lambda i: jnp.where(i >= npc, 1, 0)
    kern = functools.partial(_gmlp_kernel, groups=groups, gd=gw // groups)
    return pl.pallas_call(
        kern,
        grid=(n // c,),
        in_specs=[pl.BlockSpec((c, gw), lambda i: (i, gu_blk)),
                  pl.BlockSpec((c, gw), lambda i: (i, gv_blk)),
                  pl.BlockSpec((1, groups, c, c), lambda i: (sel(i), 0, 0, 0)),
                  pl.BlockSpec((1, c, gw), lambda i: (sel(i), 0, 0)),
                  pl.BlockSpec((1, gw), lambda i: (0, 0))],
        out_specs=[pl.BlockSpec((c, gw), lambda i: (i, 0)),
                   pl.BlockSpec((c, gw), lambda i: (jnp.maximum(i - npc, 0), 0))],
        out_shape=[jax.ShapeDtypeStruct((n, gw), BF16),
                   jax.ShapeDtypeStruct((n - n_prompt, gw), F32)],
        compiler_params=_params("arbitrary"),
        name="chunk_mlp",
    )(proj, proj, w_st, bias_st, g_v.reshape(1, gw))


def _branch_kernel(ap_ref, as_ref, b_ref, wa_ref, wb_ref, ga_ref, gb_ref, o_ref, *, n_prompt_tiles):
    a = jnp.where(pl.program_id(1) < n_prompt_tiles, ap_ref[...], as_ref[...].astype(BF16))
    ya = jnp.dot(a, wa_ref[...], preferred_element_type=F32)
    yb = jnp.dot(b_ref[...], wb_ref[...], preferred_element_type=F32)
    o_ref[...] = (jax.nn.sigmoid(ga_ref[...]) * ya + jax.nn.sigmoid(gb_ref[...]) * yb).astype(o_ref.dtype)


def _branches(a_p, a_s, bm, wa, wb, proj, ga_off, gb_off, n_prompt):
    n, kb = bm.shape
    ka = a_p.shape[1]
    d = wa.shape[1]
    tm = _pow2_tile(512, n_prompt, n - n_prompt)
    tn = _pow2_tile(1024, d, ga_off, gb_off)
    npt = n_prompt // tm
    return pl.pallas_call(
        functools.partial(_branch_kernel, n_prompt_tiles=npt),
        grid=(d // tn, n // tm),
        in_specs=[pl.BlockSpec((tm, ka), lambda j, i: (jnp.minimum(i, npt - 1), 0)),
                  pl.BlockSpec((tm, ka), lambda j, i: (jnp.maximum(i - npt, 0), 0)),
                  pl.BlockSpec((tm, kb), lambda j, i: (i, 0)),
                  pl.BlockSpec((ka, tn), lambda j, i: (0, j)),
                  pl.BlockSpec((kb, tn), lambda j, i: (0, j)),
                  pl.BlockSpec((tm, tn), lambda j, i: (i, ga_off // tn + j)),
                  pl.BlockSpec((tm, tn), lambda j, i: (i, gb_off // tn + j))],
        out_specs=pl.BlockSpec((tm, tn), lambda j, i: (i, j)),
        out_shape=jax.ShapeDtypeStruct((n, d), BF16),
        compiler_params=_params("arbitrary", "arbitrary"),
        name="branches",
    )(a_p, a_s, bm, wa, wb, proj, proj)


def _out_kernel(m_ref, w_ref, xp_ref, xs_ref, gp_ref, gs_ref, side_ref, o_ref, side_out_ref, *, n_prompt_tiles):
    i = pl.program_id(1)
    y = jnp.dot(m_ref[...], w_ref[...], preferred_element_type=F32)
    side_out_ref[...] = side_ref[...].astype(side_out_ref.dtype)

    @pl.when(i < n_prompt_tiles)
    def _():
        o_ref[...] = xp_ref[...] + gp_ref[0] * y

    @pl.when(i >= n_prompt_tiles)
    def _():
        o_ref[...] = xs_ref[...] + gs_ref[...] * y


def _out_proj(mix, w, x_p, x_s, mod_p, mod_s, k_gate, n_prompt, seq, side_table):
    n, d = mix.shape
    tm = _pow2_tile(512, seq, n - n_prompt)
    tn = _pow2_tile(1024, d)
    npt = n_prompt // tm
    per_b = seq // tm
    nj = d // tn
    side_specs, side_shapes = _side_cast_specs((side_table,), nj, n // tm)
    return pl.pallas_call(
        functools.partial(_out_kernel, n_prompt_tiles=npt),
        grid=(nj, n // tm),
        in_specs=[pl.BlockSpec((tm, d), lambda j, i: (i, 0)),
                  pl.BlockSpec((d, tn), lambda j, i: (0, j)),
                  pl.BlockSpec((tm, tn), lambda j, i: (jnp.minimum(i, npt - 1), j)),
                  pl.BlockSpec((tm, tn), lambda j, i: (jnp.maximum(i - npt, 0), j)),
                  pl.BlockSpec((1, 1, tn), lambda j, i: (jnp.minimum(i, npt - 1) // per_b, 0, k_gate * nj + j)),
                  pl.BlockSpec((tm, tn), lambda j, i: (jnp.maximum(i - npt, 0), k_gate * nj + j))] + side_specs,
        out_specs=[pl.BlockSpec((tm, tn), lambda j, i: (i, j))] + side_specs,
        out_shape=[jax.ShapeDtypeStruct((n, d), F32)] + side_shapes,
        compiler_params=_params("arbitrary", "arbitrary"),
        name="out_proj",
    )(mix, w, x_p, x_s, mod_p, mod_s, side_table)


def _cand_pairs(k):
    return [(i, j) for i in range(k) for j in range(k) if (i + 1) * (j + 1) <= k]


def _arg_rounds(s, iota, k, exact, with_weights=False):
    rank = jnp.full(s.shape, float(k), F32)
    s_in = s
    vals = []
    big = float(s.shape[0])
    for i in range(k):
        m = jnp.max(s, axis=0, keepdims=True)
        if exact:
            idx = jnp.min(jnp.where(s == m, iota, big), axis=0, keepdims=True)
            sel = iota == idx
        else:
            sel = s == m
        rank = jnp.where(sel, float(i), rank)
        s = jnp.where(sel, NEG_INF, s)
        vals.append(m)
    weight = jnp.where(rank < float(k), jnp.exp(s_in - vals[0]), 0.0) if with_weights else None
    return vals, rank, weight


def _peer_gates(s1, s2, jsum, cand_scr, cols, pairs, k, exact):
    n_iota = lax.broadcasted_iota(jnp.int32, s1.shape, 0).astype(F32)
    a, r1, ea = _arg_rounds(s1, n_iota, k, exact, with_weights=True)
    b, r2, eb = _arg_rounds(s2, n_iota, k, exact, with_weights=True)

    n_cand = cand_scr.shape[0]
    cand_scr[len(pairs):, cols] = jnp.full((n_cand - len(pairs), s1.shape[1]), NEG_INF, F32)
    for p, (i, j) in enumerate(pairs):
        cand_scr[p:p + 1, cols] = a[i] + b[j]
    c0 = cand_scr[:, cols]
    p_iota = lax.broadcasted_iota(jnp.int32, c0.shape, 0).astype(F32)
    _, prank, _ = _arg_rounds(c0, p_iota, k, exact)
    taken = prank < float(k)
    w = jnp.where(taken, jnp.exp(c0 - (a[0] + b[0])), 0.0)
    inv_z = 1.0 / jnp.sum(w, axis=0, keepdims=True)
    ones = jnp.where(taken, 1.0, 0.0)
    jcnt = jnp.dot(jsum, ones.astype(BF16), preferred_element_type=F32)

    eaz = ea * inv_z
    jd = jnp.zeros(s1.shape, F32)
    for i in range(k):
        jd = jnp.where(r1 == float(i), jcnt[i:i + 1, :], jd)

    def excess(rank):
        return jnp.abs(jnp.sum(jnp.where(rank < float(k), 1.0, 0.0), axis=0, keepdims=True) - float(k))

    return eaz, jd, eb, r2, excess(r1) + excess(r2) + excess(prank)


def _peer_topk_kernel(q_ref, sk_ref, jsum_ref, eaz_ref, jd_ref, eb_ref, r2_ref, cand_scr, *, half, pairs, k):
    nt = (((1,), (1,)), ((), ()))
    q = q_ref[...].astype(BF16)
    s1 = lax.dot_general(sk_ref[0, 0], q[:, :half], nt, preferred_element_type=F32)
    s2 = lax.dot_general(sk_ref[0, 1], q[:, half:], nt, preferred_element_type=F32)
    jsum = jsum_ref[...]
    n_blocks = s1.shape[1] // LANE_V7X

    def run(exact):
        bad = None
        for cb in range(n_blocks):
            cols = slice(cb * LANE_V7X, (cb + 1) * LANE_V7X)
            eaz, jd, eb, r2, dev = _peer_gates(s1[:, cols], s2[:, cols], jsum, cand_scr, cols, pairs, k, exact)
            eaz_ref[0, cb] = eaz
            jd_ref[0, cb] = jd
            eb_ref[0, cb] = eb.astype(eb_ref.dtype)
            r2_ref[0, cb] = r2.astype(r2_ref.dtype)
            bad = dev if bad is None else bad + dev
        return jnp.max(bad)

    miscount = run(exact=False)

    @pl.when(miscount > 0.5)
    def _():
        run(exact=True)


def _peer_topk(qp, sub_keys_b, n_prompt):
    n = qp.shape[0]
    heads, _, n_keys, half = sub_keys_b.shape
    tt = _pow2_tile(512, n_prompt, n - n_prompt)
    pairs = _cand_pairs(PEER_TOPK)
    n_cand = -(-len(pairs) // 64) * 64
    jsum = np.zeros((PEER_TOPK, n_cand), np.float32)
    for p, (i, _) in enumerate(pairs):
        jsum[i, p] = 1.0
    kern = functools.partial(_peer_topk_kernel, half=half, pairs=pairs, k=PEER_TOPK)
    maps = jax.ShapeDtypeStruct((heads, n // LANE_V7X, n_keys, LANE_V7X), F32)
    maps_b = jax.ShapeDtypeStruct((heads, n // LANE_V7X, n_keys, LANE_V7X), BF16)
    mspec = pl.BlockSpec((1, tt // LANE_V7X, n_keys, LANE_V7X), lambda i, h: (h, i, 0, 0))
    return pl.pallas_call(
        kern,
        grid=(n // tt, heads),
        in_specs=[pl.BlockSpec((tt, 2 * half), lambda i, h: (i, h)),
                  pl.BlockSpec((1, 2, n_keys, half), lambda i, h: (h, 0, 0, 0)),
                  pl.BlockSpec(jsum.shape, lambda i, h: (0, 0))],
        out_specs=[mspec, mspec, mspec, mspec],
        out_shape=[maps, maps, maps_b, maps_b],
        scratch_shapes=[pltpu.VMEM((n_cand, tt), F32)],
        compiler_params=_params("arbitrary", "arbitrary"),
        name="peer_topk",
    )(qp, sub_keys_b, jnp.asarray(jsum, BF16))


def _peer_main_kernel(h_ref, u_ref, v_ref, eaz_ref, jd_ref, eb_ref, r2_ref, o_ref,
                      ga0_scr, ga1_scr, g0_scr, g1_scr, ht_scr, *, heads, n_keys, rows_per_step, n_steps):
    e = pl.program_id(1)
    n_tb = h_ref.shape[0] // LANE_V7X
    blocks = [(r, tb) for r in range(rows_per_step) for tb in range(n_tb)]
    pair = 2 if rows_per_step % 2 == 0 else 1
    gate_items = [(r0, tb) for r0 in range(0, rows_per_step, pair) for tb in range(n_tb)]
    n_slices = math.gcd(len(gate_items), 16)

    def gates(chunk, g_scr, part=None):
        for r0, tb in (gate_items if part is None else gate_items[part::n_slices]):
            acc = [None] * pair
            for h in range(heads):
                r2 = r2_ref[h, tb]
                eb = eb_ref[h, tb]
                for kk in range(pair):
                    n1 = chunk * rows_per_step + r0 + kk
                    ea = eaz_ref[h, tb, pl.ds(n1, 1), :].astype(BF16)
                    jd = jd_ref[h, tb, pl.ds(n1, 1), :].astype(BF16)
                    t = ea * jnp.where(r2 < jd, eb, jnp.zeros((), BF16))
                    acc[kk] = t if acc[kk] is None else acc[kk] + t
            for kk in range(pair):
                g_scr[(r0 + kk) * n_tb + tb] = acc[kk]

    @pl.when(e == 0)
    def _():
        o_ref[...] = jnp.zeros_like(o_ref)
        ga1_scr[...] = jnp.zeros_like(ga1_scr)
        gates(0, g0_scr)
        ht_scr[...] = h_ref[...].T

    def stage(ga_cur, ga_prev, g_cur, g_next):
        act_t = jnp.dot(u_ref[...], ht_scr[...], preferred_element_type=F32)
        m_slices = math.gcd(len(blocks), 16)
        dq = o_ref.shape[1] // m_slices
        for q in range(m_slices):
            qs = slice(q * dq, (q + 1) * dq)
            o_ref[:, qs] += jnp.dot(ga_prev[...], v_ref[:, qs], preferred_element_type=F32)
            for r, tb in blocks[q::m_slices]:
                sl = slice(r * n_keys, (r + 1) * n_keys)
                ts = slice(tb * LANE_V7X, (tb + 1) * LANE_V7X)
                ga_cur[ts, sl] = (g_cur[r * n_tb + tb] * jax.nn.gelu(act_t[sl, ts].astype(BF16))).T
            if q % (m_slices // n_slices) == 0:
                gates(jnp.minimum(e + 1, n_steps - 1), g_next, part=q // (m_slices // n_slices))

    @pl.when(e % 2 == 0)
    def _():
        stage(ga0_scr, ga1_scr, g0_scr, g1_scr)

    @pl.when(e % 2 == 1)
    def _():
        stage(ga1_scr, ga0_scr, g1_scr, g0_scr)


def _peer_main(h2, u_b, v_b, maps, n_prompt):
    n, d = h2.shape
    n_exp = u_b.shape[0]
    heads, _, n_keys, lane = maps[0].shape
    tt = _pow2_tile(512, n_prompt, n - n_prompt)
    ec = _pow2_tile(512, n_exp)
    rps = ec // n_keys
    n_steps = n_exp // ec
    kern = functools.partial(_peer_main_kernel, heads=heads, n_keys=n_keys, rows_per_step=rps, n_steps=n_steps)
    once = pl.Buffered(1)
    mspec = pl.BlockSpec((heads, tt // lane, n_keys, lane), lambda i, e: (0, i, 0, 0), pipeline_mode=once)
    return pl.pallas_call(
        kern,
        grid=(n // tt, n_steps + 1),
        in_specs=[pl.BlockSpec((tt, d), lambda i, e: (i, 0), pipeline_mode=once),
                  pl.BlockSpec((ec, d), lambda i, e: (jnp.minimum(e, n_steps - 1), 0)),
                  pl.BlockSpec((ec, d), lambda i, e: (jnp.maximum(e - 1, 0), 0)),
                  mspec, mspec, mspec, mspec],
        out_specs=pl.BlockSpec((tt, d), lambda i, e: (i, 0), pipeline_mode=once),
        out_shape=jax.ShapeDtypeStruct((n, d), F32),
        scratch_shapes=[pltpu.VMEM((tt, ec), BF16), pltpu.VMEM((tt, ec), BF16),
                        pltpu.VMEM((rps * tt // lane, n_keys, lane), BF16),
                        pltpu.VMEM((rps * tt // lane, n_keys, lane), BF16),
                        pltpu.VMEM((d, tt), BF16)],
        compiler_params=_params("arbitrary", "arbitrary"),
        name="peer_main",
    )(h2, u_b, v_b, *maps)


def _final_kernel(x_ref, p_ref, gp_ref, gs_ref, g_ref, yp_ref, ys_ref, *, n_prompt_tiles):
    i = pl.program_id(0)

    def norm(x):
        return x * lax.rsqrt(jnp.mean(x * x, axis=-1, keepdims=True) + EPS) * g_ref[...]

    @pl.when(i < n_prompt_tiles)
    def _():
        yp_ref[...] = norm(x_ref[...] + gp_ref[0] * p_ref[...])

    @pl.when(i >= n_prompt_tiles)
    def _():
        ys_ref[...] = norm(x_ref[...] + gs_ref[...] * p_ref[...])


def _final(x1, peer_out, mod_p, mod_s, k_gate, g_final, n_prompt, seq):
    n, d = x1.shape
    tm = _pow2_tile(256, seq, n - n_prompt)
    npt = n_prompt // tm
    per_b = seq // tm
    return pl.pallas_call(
        functools.partial(_final_kernel, n_prompt_tiles=npt),
        grid=(n // tm,),
        in_specs=[pl.BlockSpec((tm, d), lambda i: (i, 0)),
                  pl.BlockSpec((tm, d), lambda i: (i, 0)),
                  pl.BlockSpec((1, 1, d), lambda i: (jnp.minimum(i, npt - 1) // per_b, 0, k_gate)),
                  pl.BlockSpec((tm, d), lambda i: (jnp.maximum(i - npt, 0), k_gate)),
                  pl.BlockSpec((1, d), lambda i: (0, 0))],
        out_specs=[pl.BlockSpec((tm, d), lambda i: (jnp.minimum(i, npt - 1), 0)),
                   pl.BlockSpec((tm, d), lambda i: (jnp.maximum(i - npt, 0), 0))],
        out_shape=[jax.ShapeDtypeStruct((n_prompt, d), F32),
                   jax.ShapeDtypeStruct((n - n_prompt, d), F32)],
        compiler_params=_params("arbitrary"),
        name="final_norm",
    )(x1, peer_out, mod_p, mod_s, g_final.reshape(1, d))


def kernel(x_prompt, x_sample, c_prompt, c_sample, state_hgrn, w_ada, b_ada, g_norm1, w_in, hgrn_lb_logits,
           g_hgrn_out, g_gmlp_v, w_spatial, b_spatial, w_branch_a, w_branch_b, w_out, g_norm2, w_peer_q,
           peer_sub_keys, peer_u, peer_v, g_final):
    batch, seq, d = x_prompt.shape
    db, dt, _ = x_sample.shape
    depth, _, heads, dk, dv = state_hgrn.shape
    assert depth == 1 and dk == dv == LANE_V7X
    hw = heads * dk
    groups, gc = w_spatial.shape[1], w_spatial.shape[2]
    gw = w_branch_b.shape[1]
    assert gc == LANE_V7X and gw // groups == LANE_V7X and dt <= gc and gc % dt == 0
    n_prompt, n_sample = batch * seq, db * dt
    n = n_prompt + n_sample

    c_rows = jnp.concatenate([jnp.repeat(c_sample, dt, axis=0), c_prompt], axis=0)
    c_rows = jnp.pad(c_rows, ((0, (-c_rows.shape[0]) % 16), (0, 0)))
    mod = _ada(c_rows, w_ada[0], b_ada[0])
    mod_s = mod
    mod_p = mod[n_sample:n_sample + batch].reshape(batch, 1, N_MOD * d)

    x_p = x_prompt.reshape(n_prompt, d)
    x_s = x_sample.reshape(n_sample, d)

    h1 = _norm_mod(x_p, x_s, g_norm1[0], mod_p, mod_s, 1, 0, n_prompt, n_sample, seq)
    proj, peer_u_b = _matmul(h1, w_in[0], F32, n_prompt, "in_proj", side_tables=(peer_u[0],))

    lb = jnp.cumsum(jax.nn.softmax(hgrn_lb_logits.astype(F32), axis=0), axis=0)[0].reshape(1, hw)
    g_ho = g_hgrn_out[0].reshape(1, hw)
    a_p, st_p = _hgrn_prompt(proj, lb, g_ho, batch, seq, heads, dk, n_prompt)
    a_s, st_s = _hgrn_sample(proj, lb, g_ho, state_hgrn, heads, dk, n_prompt, dt)

    tril = jnp.tril(jnp.ones((gc, gc), F32))
    w_sp = w_spatial[0]
    blk = jnp.arange(gc) // dt
    w_samp = jnp.tile(w_sp[:, :dt, :dt], (1, gc // dt, gc // dt)) * (blk[:, None] == blk[None, :])
    w_st = jnp.stack([w_sp * tril, w_samp * tril]).astype(BF16)
    bias_full = jnp.repeat(b_spatial[0].T, gw // groups, axis=1)
    bias_st = jnp.stack([bias_full, jnp.tile(bias_full[:dt], (gc // dt, 1))])
    gu_blk = 4 * hw // gw
    assert gu_blk * gw == 4 * hw
    bm, v_s = _gmlp(proj, w_st, bias_st, g_gmlp_v[0], n_prompt, gu_blk, gu_blk + 1)

    ga_off = 4 * hw + 2 * gw
    mix = _branches(a_p, a_s, bm, w_branch_a[0].astype(BF16), w_branch_b[0].astype(BF16), proj,
                    ga_off, ga_off + d, n_prompt)
    x1, peer_v_b = _out_proj(mix, w_out[0].astype(BF16), x_p, x_s, mod_p, mod_s, 2, n_prompt, seq, peer_v[0])

    h2 = _norm_mod(x1, x1, g_norm2[0], mod_p, mod_s, 4, 3, n_prompt, n_sample, seq)
    qp = _matmul(h2, w_peer_q[0], F32, n_prompt, "peer_query")
    maps = _peer_topk(qp, peer_sub_keys[0].astype(BF16), n_prompt)
    peer_out = _peer_main(h2, peer_u_b, peer_v_b, maps, n_prompt)
    y_p, y_s = _final(x1, peer_out, mod_p, mod_s, 5, g_final, n_prompt, seq)

    state_p = jnp.swapaxes(st_p, -1, -2)
    return (y_p.reshape(batch, seq, d), y_s.reshape(db, dt, d), state_p, st_s,
            v_s.reshape(1, db, dt, gw))
```

```python
import functools
import math

import numpy as np
import jax
import jax.numpy as jnp
from jax import lax
from jax.experimental import pallas as pl
from jax.experimental.pallas import tpu as pltpu

EPS = 1e-6
N_MOD = 6
PEER_TOPK = 16
LANE_V7X = 128
SUBLANE_V7X = 8
BF16_TILE_ROWS_V7X = 16
MAX_VALUE_SLICES = 16
VMEM_LIMIT_V7X = 60 * 1024 * 1024
F32 = jnp.float32
BF16 = jnp.bfloat16
NEG_INF = float("-inf")


def _params(*sem):
    return pltpu.CompilerParams(dimension_semantics=sem, vmem_limit_bytes=VMEM_LIMIT_V7X)


def _pow2_tile(target, *sizes):
    g = 0
    for s in sizes:
        g = math.gcd(g, s)
    t = 1
    while t * 2 <= target and g % (t * 2) == 0:
        t *= 2
    return t


def _silu(x):
    return x * jax.nn.sigmoid(x)


def _ada_kernel(c_ref, w_ref, b_ref, o_ref, s_scr):
    @pl.when(pl.program_id(0) == 0)
    def _():
        s_scr[...] = _silu(c_ref[...]).astype(BF16)

    o_ref[...] = jnp.dot(s_scr[...], w_ref[...].astype(BF16), preferred_element_type=F32) + b_ref[...]


def _ada(c_rows, w_ada, b_ada):
    m, d = c_rows.shape
    n = w_ada.shape[1]
    tn = _pow2_tile(512, n)
    return pl.pallas_call(
        _ada_kernel,
        grid=(n // tn,),
        in_specs=[pl.BlockSpec((m, d), lambda j: (0, 0)),
                  pl.BlockSpec((d, tn), lambda j: (0, j)),
                  pl.BlockSpec((1, tn), lambda j: (0, j))],
        out_specs=pl.BlockSpec((m, tn), lambda j: (0, j)),
        out_shape=jax.ShapeDtypeStruct((m, n), F32),
        scratch_shapes=[pltpu.VMEM((m, d), BF16)],
        compiler_params=_params("arbitrary"),
        name="ada_mod",
    )(c_rows, w_ada, b_ada.reshape(1, n))


def _norm_mod_kernel(xp_ref, xs_ref, g_ref, scp_ref, shp_ref, scs_ref, shs_ref, o_ref, *, n_prompt_tiles):
    i = pl.program_id(0)

    def normed(x):
        return x * lax.rsqrt(jnp.mean(x * x, axis=-1, keepdims=True) + EPS) * g_ref[...]

    @pl.when(i < n_prompt_tiles)
    def _():
        o_ref[...] = (normed(xp_ref[...]) * (1.0 + scp_ref[0]) + shp_ref[0]).astype(o_ref.dtype)

    @pl.when(i >= n_prompt_tiles)
    def _():
        o_ref[...] = (normed(xs_ref[...]) * (1.0 + scs_ref[...]) + shs_ref[...]).astype(o_ref.dtype)


def _row_sources(x_p, x_s, n_prompt, tm):
    npt = n_prompt // tm
    s_off = npt if x_s is x_p else 0
    return (lambda i: (jnp.minimum(i, npt - 1), 0)), (lambda i: (s_off + jnp.maximum(i - npt, 0), 0))


def _norm_mod(x_p, x_s, g, mod_p, mod_s, k_scale, k_shift, n_prompt, n_sample, seq):
    d = x_p.shape[1]
    n = n_prompt + n_sample
    tm = _pow2_tile(256, seq, n_sample)
    npt = n_prompt // tm
    per_b = seq // tm
    p_map, s_map = _row_sources(x_p, x_s, n_prompt, tm)

    def pidx(k):
        return lambda i: (jnp.minimum(i, npt - 1) // per_b, 0, k)

    def sidx(k):
        return lambda i: (jnp.maximum(i - npt, 0), k)

    return pl.pallas_call(
        functools.partial(_norm_mod_kernel, n_prompt_tiles=npt),
        grid=(n // tm,),
        in_specs=[pl.BlockSpec((tm, d), p_map),
                  pl.BlockSpec((tm, d), s_map),
                  pl.BlockSpec((1, d), lambda i: (0, 0)),
                  pl.BlockSpec((1, 1, d), pidx(k_scale)),
                  pl.BlockSpec((1, 1, d), pidx(k_shift)),
                  pl.BlockSpec((tm, d), sidx(k_scale)),
                  pl.BlockSpec((tm, d), sidx(k_shift))],
        out_specs=pl.BlockSpec((tm, d), lambda i: (i, 0)),
        out_shape=jax.ShapeDtypeStruct((n, d), BF16),
        compiler_params=_params("arbitrary"),
        name="norm_mod",
    )(x_p, x_s, g.reshape(1, d), mod_p, mod_p, mod_s, mod_s)


def _mm_kernel(a_ref, w_ref, *rest, n_side):
    side_in, o_ref, side_out, wb_scr = rest[:n_side], rest[n_side], rest[n_side + 1:2 * n_side + 1], rest[-1]

    @pl.when(pl.program_id(1) == 0)
    def _():
        wb_scr[...] = w_ref[...].astype(BF16)

    o_ref[...] = jnp.dot(a_ref[...], wb_scr[...], preferred_element_type=F32).astype(o_ref.dtype)
    for src, dst in zip(side_in, side_out):
        dst[...] = src[...].astype(dst.dtype)


def _side_cast_specs(side_tables, nj, ni):
    specs, shapes = [], []
    for t in side_tables:
        rows = t.shape[0]
        rb = rows // _pow2_tile(nj * ni, rows)
        last = rows // rb - 1
        specs.append(pl.BlockSpec((rb, t.shape[1]), lambda j, i, last=last: (jnp.minimum(j * ni + i, last), 0)))
        shapes.append(jax.ShapeDtypeStruct(t.shape, BF16))
    return specs, shapes


def _matmul(a, w, out_dtype, n_prompt, name, side_tables=()):
    n, k = a.shape
    nc = w.shape[1]
    tm = _pow2_tile(512, n_prompt, n - n_prompt)
    tn = _pow2_tile(1024, nc)
    ni = n // tm
    side_specs, side_shapes = _side_cast_specs(side_tables, nc // tn, ni)
    outs = pl.pallas_call(
        functools.partial(_mm_kernel, n_side=len(side_tables)),
        grid=(nc // tn, ni),
        in_specs=[pl.BlockSpec((tm, k), lambda j, i: (i, 0)),
                  pl.BlockSpec((k, tn), lambda j, i: (0, j))] + side_specs,
        out_specs=[pl.BlockSpec((tm, tn), lambda j, i: (i, j))] + side_specs,
        out_shape=[jax.ShapeDtypeStruct((n, nc), out_dtype)] + side_shapes,
        scratch_shapes=[pltpu.VMEM((k, tn), BF16)],
        compiler_params=_params("arbitrary", "arbitrary"),
        name=name,
    )(a, w, *side_tables)
    return outs[0] if not side_tables else outs


def _hgrn_tables(c):
    nl = int(math.log2(c))
    r = np.arange(c)[:, None]
    j = np.arange(c)[None, :]
    mats = []
    lvl = np.full((c, c), -1, np.int32)
    for l in range(nl):
        m = c >> (l + 1)
        mid = (r // (2 * m)) * 2 * m + m
        upper = r >= mid
        t = np.where(upper, (j >= mid) & (j <= r), (j > r) & (j <= mid - 1))
        mats.append(t)
        same = (r // (2 * m)) == (j // (2 * m))
        lvl = np.where(same & upper & (j < mid), l, lvl)
    lvl = np.where(r == j, nl, lvl)
    mats.append(j <= r)
    mats.append(j > r)
    t_all = np.concatenate(mats, axis=0).astype(np.float32)
    t_all = np.concatenate([t_all, t_all], axis=1)
    return jnp.asarray(t_all, BF16), jnp.asarray(lvl), nl


def _hgrn_prompt_kernel(q_ref, f_ref, v_ref, og_ref, lb_ref, g_ref, tall_ref, lvl_ref,
                        a_ref, st_ref, s_scr, *, chunk, n_levels, n_chunks, heads_per_step, dk):
    c = chunk
    s_scr[...] = jnp.zeros_like(s_scr)
    lb = lb_ref[...]
    lvl = lvl_ref[...]
    rows = lax.broadcasted_iota(jnp.int32, (c, q_ref.shape[1]), 0)
    nt = (((1,), (1,)), ((), ()))

    def body(ci, carry):
        sl = pl.ds(pl.multiple_of(ci * c, c), c)
        qs = _silu(q_ref[sl, :])
        f = lb + (1.0 - lb) * jax.nn.sigmoid(f_ref[sl, :])
        k = 1.0 - f
        lf = jnp.log(f)
        v = v_ref[sl, :].astype(BF16)
        lf_hi = lf.astype(BF16)
        lf_lo = (lf - lf_hi.astype(F32)).astype(BF16)
        ex = jnp.dot(tall_ref[...], jnp.concatenate([lf_hi, lf_lo], axis=0), preferred_element_type=F32)
        qs_b = qs.astype(BF16)
        k_b = k.astype(BF16)
        ys = []
        for l in range(n_levels):
            m = c >> (l + 1)
            e_l = jnp.exp(ex[l * c:(l + 1) * c, :])
            ys.append((e_l * jnp.where((rows & m) != 0, qs, k)).astype(BF16))
        bcum = ex[n_levels * c:(n_levels + 1) * c, :]
        brev = ex[(n_levels + 1) * c:(n_levels + 2) * c, :]
        qh = (qs * jnp.exp(bcum)).astype(BF16)
        kh = (k * jnp.exp(brev)).astype(BF16)
        decay = jnp.exp(bcum[c - 1:c, :])
        outs = []
        for hh in range(heads_per_step):
            hs = slice(hh * dk, (hh + 1) * dk)
            scores = jnp.where(lvl == n_levels,
                               lax.dot_general(qs_b[:, hs], k_b[:, hs], nt, preferred_element_type=F32), 0.0)
            for l in range(n_levels):
                y = ys[l][:, hs]
                p = lax.dot_general(y, y, nt, preferred_element_type=F32)
                scores = scores + jnp.where(lvl == l, p, 0.0)
            st = s_scr[hh]
            o = jnp.dot(scores.astype(BF16), v[:, hs], preferred_element_type=F32)
            o = o + lax.dot_general(qh[:, hs], st.astype(BF16), nt, preferred_element_type=F32)
            kv_t = lax.dot_general(v[:, hs], kh[:, hs], (((0,), (0,)), ((), ())), preferred_element_type=F32)
            s_scr[hh] = st * decay[:, hs] + kv_t
            outs.append(o * lax.rsqrt(jnp.mean(o * o, axis=-1, keepdims=True) + EPS))
        on = jnp.concatenate(outs, axis=1) if heads_per_step > 1 else outs[0]
        a_ref[sl, :] = (on * g_ref[...] * _silu(og_ref[sl, :])).astype(a_ref.dtype)
        return carry

    lax.fori_loop(0, n_chunks, body, 0, unroll=8)
    st_ref[0, 0] = s_scr[...]


def _hgrn_prompt(proj, lb, g_out, batch, seq, heads, dk, n_rows):
    c = _pow2_tile(128, seq)
    hp = _pow2_tile(4, heads)
    t_all, lvl, nl = _hgrn_tables(c)
    w = hp * dk
    nh = heads // hp
    col = lambda off: (lambda b, h: (b, off + h))
    kern = functools.partial(_hgrn_prompt_kernel, chunk=c, n_levels=nl, n_chunks=seq // c,
                             heads_per_step=hp, dk=dk)
    return pl.pallas_call(
        kern,
        grid=(batch, nh),
        in_specs=[pl.BlockSpec((seq, w), col(0)),
                  pl.BlockSpec((seq, w), col(nh)),
                  pl.BlockSpec((seq, w), col(2 * nh)),
                  pl.BlockSpec((seq, w), col(3 * nh)),
                  pl.BlockSpec((1, w), lambda b, h: (0, h)),
                  pl.BlockSpec((1, w), lambda b, h: (0, h)),
                  pl.BlockSpec(t_all.shape, lambda b, h: (0, 0)),
                  pl.BlockSpec(lvl.shape, lambda b, h: (0, 0))],
        out_specs=[pl.BlockSpec((seq, w), lambda b, h: (b, h)),
                   pl.BlockSpec((1, 1, hp, dk, dk), lambda b, h: (0, b, h, 0, 0))],
        out_shape=[jax.ShapeDtypeStruct((n_rows, heads * dk), BF16),
                   jax.ShapeDtypeStruct((1, batch, heads, dk, dk), F32)],
        scratch_shapes=[pltpu.VMEM((hp, dk, dk), F32)],
        compiler_params=_params("arbitrary", "arbitrary"),
        name="hgrn_prompt",
    )(proj, proj, proj, proj, lb, g_out, t_all, lvl)


def _hgrn_sample_kernel(q_ref, f_ref, v_ref, og_ref, lb_ref, g_ref, s0_ref, a_ref, s_ref, *, bt, steps):
    sub = SUBLANE_V7X
    per_tile = sub // steps
    lb = lb_ref[...]
    g = g_ref[...]
    rows = lax.broadcasted_iota(jnp.int32, (sub, q_ref.shape[1]), 0)
    tpos = rows % steps
    grp = rows // steps

    def per_batch(x, t):
        out = x[t:t + 1, :]
        for j in range(1, per_tile):
            out = jnp.where(grp == j, x[j * steps + t:j * steps + t + 1, :], out)
        return out

    def body(p, carry):
        sl = pl.ds(pl.multiple_of(p * sub, sub), sub)
        qs = _silu(q_ref[sl, :])
        f = lb + (1.0 - lb) * jax.nn.sigmoid(f_ref[sl, :])
        k = 1.0 - f
        v = v_ref[sl, :]
        bmat = jnp.log(f)
        shift = 1
        while shift < steps:
            bmat = jnp.where(tpos >= shift, bmat + pltpu.roll(bmat, shift, axis=0), bmat)
            shift *= 2
        blast = per_batch(bmat, steps - 1)
        o = jnp.zeros_like(qs)
        for t in range(steps):
            x = jnp.where(tpos >= t, qs * per_batch(k, t) * jnp.exp(jnp.minimum(bmat - per_batch(bmat, t), 0.0)), 0.0)
            o = o + jnp.sum(x, axis=-1, keepdims=True) * per_batch(v, t)
        qh = (qs * jnp.exp(bmat)).astype(BF16)
        kh = k * jnp.exp(blast - bmat)
        vb = v.astype(BF16)
        for j in range(per_tile):
            b = p * per_tile + j
            s0 = s0_ref[0, b, 0]
            mine = grp == j
            o = o + jnp.where(mine, jnp.dot(qh, s0.astype(BF16), preferred_element_type=F32), 0.0)
            kv = lax.dot_general(jnp.where(mine, kh, 0.0).astype(BF16), vb, (((0,), (0,)), ((), ())),
                                 preferred_element_type=F32)
            decay_row = jnp.exp(bmat[(j + 1) * steps - 1:(j + 1) * steps, :])
            decay_col = jnp.broadcast_to(decay_row, s0.shape).T
            s_ref[0, b, 0] = s0 * decay_col + kv
        on = o * lax.rsqrt(jnp.mean(o * o, axis=-1, keepdims=True) + EPS)
        a_ref[sl, :] = on * g * _silu(og_ref[sl, :])
        return carry

    lax.fori_loop(0, bt // per_tile, body, 0, unroll=4)


def _hgrn_sample(proj, lb, g_out, s0, heads, dk, n_prompt, steps):
    db = s0.shape[1]
    bt = _pow2_tile(32, db)
    rb = bt * steps
    assert 8 % steps == 0 and n_prompt % rb == 0
    col = lambda off: (lambda j, h: (n_prompt // rb + j, off + h))
    kern = functools.partial(_hgrn_sample_kernel, bt=bt, steps=steps)
    return pl.pallas_call(
        kern,
        grid=(db // bt, heads),
        in_specs=[pl.BlockSpec((rb, dk), col(0)),
                  pl.BlockSpec((rb, dk), col(heads)),
                  pl.BlockSpec((rb, dk), col(2 * heads)),
                  pl.BlockSpec((rb, dk), col(3 * heads)),
                  pl.BlockSpec((1, dk), lambda j, h: (0, h)),
                  pl.BlockSpec((1, dk), lambda j, h: (0, h)),
                  pl.BlockSpec((1, bt, 1, dk, dk), lambda j, h: (0, j, h, 0, 0))],
        out_specs=[pl.BlockSpec((rb, dk), lambda j, h: (j, h)),
                   pl.BlockSpec((1, bt, 1, dk, dk), lambda j, h: (0, j, h, 0, 0))],
        out_shape=[jax.ShapeDtypeStruct((db * steps, heads * dk), F32),
                   jax.ShapeDtypeStruct(s0.shape, F32)],
        compiler_params=_params("arbitrary", "arbitrary"),
        name="hgrn_sample",
    )(proj, proj, proj, proj, lb, g_out, s0)


def _gmlp_kernel(gu_ref, gv_ref, w_ref, bias_ref, g_ref, o_ref, v_ref, *, groups, gd):
    u = jax.nn.gelu(gu_ref[...])
    vv = jax.nn.gelu(gv_ref[...])
    r = lax.rsqrt(jnp.mean(vv * vv, axis=-1, keepdims=True) + EPS)
    v = vv * r * g_ref[...]
    v_ref[...] = v
    for gi in range(groups):
        sl = slice(gi * gd, (gi + 1) * gd)
        z = jnp.dot(w_ref[0, gi], v[:, sl].astype(BF16), preferred_element_type=F32) + bias_ref[0, :, sl]
        o_ref[:, sl] = (u[:, sl] * z).astype(o_ref.dtype)


def _gmlp(proj, w_st, bias_st, g_v, n_prompt, gu_blk, gv_blk):
    n = proj.shape[0]
    _, groups, c, _ = w_st.shape
    gw = bias_st.shape[2]
    npc = n_prompt // c
    sel = lambda i: jnp.where(i >= npc, 1, 0)
    kern = functools.partial(_gmlp_kernel, groups=groups, gd=gw // groups)
    return pl.pallas_call(
        kern,
        grid=(n // c,),
        in_specs=[pl.BlockSpec((c, gw), lambda i: (i, gu_blk)),
                  pl.BlockSpec((c, gw), lambda i: (i, gv_blk)),
                  pl.BlockSpec((1, groups, c, c), lambda i: (sel(i), 0, 0, 0)),
                  pl.BlockSpec((1, c, gw), lambda i: (sel(i), 0, 0)),
                  pl.BlockSpec((1, gw), lambda i: (0, 0))],
        out_specs=[pl.BlockSpec((c, gw), lambda i: (i, 0)),
                   pl.BlockSpec((c, gw), lambda i: (jnp.maximum(i - npc, 0), 0))],
        out_shape=[jax.ShapeDtypeStruct((n, gw), BF16),
                   jax.ShapeDtypeStruct((n - n_prompt, gw), F32)],
        compiler_params=_params("arbitrary"),
        name="chunk_mlp",
    )(proj, proj, w_st, bias_st, g_v.reshape(1, gw))


def _branch_kernel(ap_ref, as_ref, b_ref, wa_ref, wb_ref, ga_ref, gb_ref, o_ref, *, n_prompt_tiles):
    a = jnp.where(pl.program_id(1) < n_prompt_tiles, ap_ref[...], as_ref[...].astype(BF16))
    ya = jnp.dot(a, wa_ref[...], preferred_element_type=F32)
    yb = jnp.dot(b_ref[...], wb_ref[...], preferred_element_type=F32)
    o_ref[...] = (jax.nn.sigmoid(ga_ref[...]) * ya + jax.nn.sigmoid(gb_ref[...]) * yb).astype(o_ref.dtype)


def _branches(a_p, a_s, bm, wa, wb, proj, ga_off, gb_off, n_prompt):
    n, kb = bm.shape
    ka = a_p.shape[1]
    d = wa.shape[1]
    tm = _pow2_tile(512, n_prompt, n - n_prompt)
    tn = _pow2_tile(1024, d, ga_off, gb_off)
    npt = n_prompt // tm
    return pl.pallas_call(
        functools.partial(_branch_kernel, n_prompt_tiles=npt),
        grid=(d // tn, n // tm),
        in_specs=[pl.BlockSpec((tm, ka), lambda j, i: (jnp.minimum(i, npt - 1), 0)),
                  pl.BlockSpec((tm, ka), lambda j, i: (jnp.maximum(i - npt, 0), 0)),
                  pl.BlockSpec((tm, kb), lambda j, i: (i, 0)),
                  pl.BlockSpec((ka, tn), lambda j, i: (0, j)),
                  pl.BlockSpec((kb, tn), lambda j, i: (0, j)),
                  pl.BlockSpec((tm, tn), lambda j, i: (i, ga_off // tn + j)),
                  pl.BlockSpec((tm, tn), lambda j, i: (i, gb_off // tn + j))],
        out_specs=pl.BlockSpec((tm, tn), lambda j, i: (i, j)),
        out_shape=jax.ShapeDtypeStruct((n, d), BF16),
        compiler_params=_params("arbitrary", "arbitrary"),
        name="branches",
    )(a_p, a_s, bm, wa, wb, proj, proj)


def _out_kernel(m_ref, w_ref, xp_ref, xs_ref, gp_ref, gs_ref, side_ref, o_ref, side_out_ref, *, n_prompt_tiles):
    i = pl.program_id(1)
    y = jnp.dot(m_ref[...], w_ref[...], preferred_element_type=F32)
    side_out_ref[...] = side_ref[...].astype(side_out_ref.dtype)

    @pl.when(i < n_prompt_tiles)
    def _():
        o_ref[...] = xp_ref[...] + gp_ref[0] * y

    @pl.when(i >= n_prompt_tiles)
    def _():
        o_ref[...] = xs_ref[...] + gs_ref[...] * y


def _out_proj(mix, w, x_p, x_s, mod_p, mod_s, k_gate, n_prompt, seq, side_table):
    n, d = mix.shape
    tm = _pow2_tile(512, seq, n - n_prompt)
    tn = _pow2_tile(1024, d)
    npt = n_prompt // tm
    per_b = seq // tm
    nj = d // tn
    side_specs, side_shapes = _side_cast_specs((side_table,), nj, n // tm)
    return pl.pallas_call(
        functools.partial(_out_kernel, n_prompt_tiles=npt),
        grid=(nj, n // tm),
        in_specs=[pl.BlockSpec((tm, d), lambda j, i: (i, 0)),
                  pl.BlockSpec((d, tn), lambda j, i: (0, j)),
                  pl.BlockSpec((tm, tn), lambda j, i: (jnp.minimum(i, npt - 1), j)),
                  pl.BlockSpec((tm, tn), lambda j, i: (jnp.maximum(i - npt, 0), j)),
                  pl.BlockSpec((1, 1, tn), lambda j, i: (jnp.minimum(i, npt - 1) // per_b, 0, k_gate * nj + j)),
                  pl.BlockSpec((tm, tn), lambda j, i: (jnp.maximum(i - npt, 0), k_gate * nj + j))] + side_specs,
        out_specs=[pl.BlockSpec((tm, tn), lambda j, i: (i, j))] + side_specs,
        out_shape=[jax.ShapeDtypeStruct((n, d), F32)] + side_shapes,
        compiler_params=_params("arbitrary", "arbitrary"),
        name="out_proj",
    )(mix, w, x_p, x_s, mod_p, mod_s, side_table)


def _cand_pairs(k):
    return [(i, j) for i in range(k) for j in range(k) if (i + 1) * (j + 1) <= k]


def _arg_rounds(s, iota, k, exact, with_weights=False):
    rank = jnp.full(s.shape, float(k), F32)
    s_in = s
    vals = []
    big = float(s.shape[0])
    for i in range(k):
        m = jnp.max(s, axis=0, keepdims=True)
        if exact:
            idx = jnp.min(jnp.where(s == m, iota, big), axis=0, keepdims=True)
            sel = iota == idx
        else:
            sel = s == m
        rank = jnp.where(sel, float(i), rank)
        s = jnp.where(sel, NEG_INF, s)
        vals.append(m)
    weight = jnp.where(rank < float(k), jnp.exp(s_in - vals[0]), 0.0) if with_weights else None
    return vals, rank, weight


def _peer_gates(s1, s2, jsum, cand_scr, cols, pairs, k, exact):
    n_iota = lax.broadcasted_iota(jnp.int32, s1.shape, 0).astype(F32)
    a, r1, ea = _arg_rounds(s1, n_iota, k, exact, with_weights=True)
    b, r2, eb = _arg_rounds(s2, n_iota, k, exact, with_weights=True)

    n_cand = cand_scr.shape[0]
    cand_scr[len(pairs):, cols] = jnp.full((n_cand - len(pairs), s1.shape[1]), NEG_INF, F32)
    for p, (i, j) in enumerate(pairs):
        cand_scr[p:p + 1, cols] = a[i] + b[j]
    c0 = cand_scr[:, cols]
    p_iota = lax.broadcasted_iota(jnp.int32, c0.shape, 0).astype(F32)
    _, prank, _ = _arg_rounds(c0, p_iota, k, exact)
    taken = prank < float(k)
    w = jnp.where(taken, jnp.exp(c0 - (a[0] + b[0])), 0.0)
    inv_z = 1.0 / jnp.sum(w, axis=0, keepdims=True)
    ones = jnp.where(taken, 1.0, 0.0)
    jcnt = jnp.dot(jsum, ones.astype(BF16), preferred_element_type=F32)

    eaz = ea * inv_z
    jd = jnp.zeros(s1.shape, F32)
    for i in range(k):
        jd = jnp.where(r1 == float(i), jcnt[i:i + 1, :], jd)

    def excess(rank):
        return jnp.abs(jnp.sum(jnp.where(rank < float(k), 1.0, 0.0), axis=0, keepdims=True) - float(k))

    return eaz, jd, eb, r2, excess(r1) + excess(r2) + excess(prank)


def _peer_topk_kernel(q_ref, sk_ref, jsum_ref, eaz_ref, jd_ref, eb_ref, r2_ref, cand_scr, *, half, pairs, k):
    nt = (((1,), (1,)), ((), ()))
    q = q_ref[...].astype(BF16)
    s1 = lax.dot_general(sk_ref[0, 0], q[:, :half], nt, preferred_element_type=F32)
    s2 = lax.dot_general(sk_ref[0, 1], q[:, half:], nt, preferred_element_type=F32)
    jsum = jsum_ref[...]
    n_blocks = s1.shape[1] // LANE_V7X

    def run(exact):
        bad = None
        for cb in range(n_blocks):
            cols = slice(cb * LANE_V7X, (cb + 1) * LANE_V7X)
            eaz, jd, eb, r2, dev = _peer_gates(s1[:, cols], s2[:, cols], jsum, cand_scr, cols, pairs, k, exact)
            eaz_ref[0, cb] = eaz
            jd_ref[0, cb] = jd
            eb_ref[0, cb] = eb.astype(eb_ref.dtype)
            r2_ref[0, cb] = r2.astype(r2_ref.dtype)
            bad = dev if bad is None else bad + dev
        return jnp.max(bad)

    miscount = run(exact=False)

    @pl.when(miscount > 0.5)
    def _():
        run(exact=True)


def _peer_topk(qp, sub_keys_b, n_prompt):
    n = qp.shape[0]
    heads, _, n_keys, half = sub_keys_b.shape
    tt = _pow2_tile(512, n_prompt, n - n_prompt)
    pairs = _cand_pairs(PEER_TOPK)
    n_cand = -(-len(pairs) // 64) * 64
    jsum = np.zeros((PEER_TOPK, n_cand), np.float32)
    for p, (i, _) in enumerate(pairs):
        jsum[i, p] = 1.0
    kern = functools.partial(_peer_topk_kernel, half=half, pairs=pairs, k=PEER_TOPK)
    maps = jax.ShapeDtypeStruct((heads, n // LANE_V7X, n_keys, LANE_V7X), F32)
    maps_b = jax.ShapeDtypeStruct((heads, n // LANE_V7X, n_keys, LANE_V7X), BF16)
    mspec = pl.BlockSpec((1, tt // LANE_V7X, n_keys, LANE_V7X), lambda i, h: (h, i, 0, 0))
    return pl.pallas_call(
        kern,
        grid=(n // tt, heads),
        in_specs=[pl.BlockSpec((tt, 2 * half), lambda i, h: (i, h)),
                  pl.BlockSpec((1, 2, n_keys, half), lambda i, h: (h, 0, 0, 0)),
                  pl.BlockSpec(jsum.shape, lambda i, h: (0, 0))],
        out_specs=[mspec, mspec, mspec, mspec],
        out_shape=[maps, maps, maps_b, maps_b],
        scratch_shapes=[pltpu.VMEM((n_cand, tt), F32)],
        compiler_params=_params("arbitrary", "arbitrary"),
        name="peer_topk",
    )(qp, sub_keys_b, jnp.asarray(jsum, BF16))


def _peer_main_kernel(h_ref, u_ref, v_ref, eaz_ref, jd_ref, eb_ref, r2_ref, o_ref,
                      ga0_scr, ga1_scr, g0_scr, g1_scr, ht_scr, *, heads, n_keys, rows_per_step, n_steps):
    e = pl.program_id(1)
    n_tb = h_ref.shape[0] // LANE_V7X
    blocks = [(r, tb) for r in range(rows_per_step) for tb in range(n_tb)]
    pair = 2 if rows_per_step % 2 == 0 else 1
    gate_items = [(r0, tb) for r0 in range(0, rows_per_step, pair) for tb in range(n_tb)]
    n_slices = math.gcd(len(gate_items), MAX_VALUE_SLICES)

    def gates(chunk, g_scr, part=None):
        for r0, tb in (gate_items if part is None else gate_items[part::n_slices]):
            acc = [None] * pair
            for h in range(heads):
                r2 = r2_ref[h, tb]
                eb = eb_ref[h, tb]
                for kk in range(pair):
                    n1 = chunk * rows_per_step + r0 + kk
                    ea = eaz_ref[h, tb, pl.ds(n1, 1), :].astype(BF16)
                    jd = jd_ref[h, tb, pl.ds(n1, 1), :].astype(BF16)
                    t = ea * jnp.where(r2 < jd, eb, jnp.zeros((), BF16))
                    acc[kk] = t if acc[kk] is None else acc[kk] + t
            for kk in range(pair):
                g_scr[(r0 + kk) * n_tb + tb] = acc[kk]

    @pl.when(e == 0)
    def _():
        o_ref[...] = jnp.zeros_like(o_ref)
        ga1_scr[...] = jnp.zeros_like(ga1_scr)
        gates(0, g0_scr)
        ht_scr[...] = h_ref[...].T

    def stage(ga_cur, ga_prev, g_cur, g_next):
        act_t = jnp.dot(u_ref[...], ht_scr[...], preferred_element_type=F32)
        m_slices = math.gcd(len(blocks), MAX_VALUE_SLICES)
        dq = o_ref.shape[1] // m_slices
        for q in range(m_slices):
            qs = slice(q * dq, (q + 1) * dq)
            o_ref[:, qs] += jnp.dot(ga_prev[...], v_ref[:, qs], preferred_element_type=F32)
            for r, tb in blocks[q::m_slices]:
                sl = slice(r * n_keys, (r + 1) * n_keys)
                ts = slice(tb * LANE_V7X, (tb + 1) * LANE_V7X)
                ga_cur[ts, sl] = (g_cur[r * n_tb + tb] * jax.nn.gelu(act_t[sl, ts].astype(BF16))).T
            if q % (m_slices // n_slices) == 0:
                gates(jnp.minimum(e + 1, n_steps - 1), g_next, part=q // (m_slices // n_slices))

    @pl.when(e % 2 == 0)
    def _():
        stage(ga0_scr, ga1_scr, g0_scr, g1_scr)

    @pl.when(e % 2 == 1)
    def _():
        stage(ga1_scr, ga0_scr, g1_scr, g0_scr)


def _peer_main(h2, u_b, v_b, maps, n_prompt):
    n, d = h2.shape
    n_exp = u_b.shape[0]
    heads, _, n_keys, lane = maps[0].shape
    tt = _pow2_tile(512, n_prompt, n - n_prompt)
    ec = _pow2_tile(512, n_exp)
    rps = ec // n_keys
    n_steps = n_exp // ec
    kern = functools.partial(_peer_main_kernel, heads=heads, n_keys=n_keys, rows_per_step=rps, n_steps=n_steps)
    once = pl.Buffered(1)
    mspec = pl.BlockSpec((heads, tt // lane, n_keys, lane), lambda i, e: (0, i, 0, 0), pipeline_mode=once)
    return pl.pallas_call(
        kern,
        grid=(n // tt, n_steps + 1),
        in_specs=[pl.BlockSpec((tt, d), lambda i, e: (i, 0), pipeline_mode=once),
                  pl.BlockSpec((ec, d), lambda i, e: (jnp.minimum(e, n_steps - 1), 0)),
                  pl.BlockSpec((ec, d), lambda i, e: (jnp.maximum(e - 1, 0), 0)),
                  mspec, mspec, mspec, mspec],
        out_specs=pl.BlockSpec((tt, d), lambda i, e: (i, 0), pipeline_mode=once),
        out_shape=jax.ShapeDtypeStruct((n, d), F32),
        scratch_shapes=[pltpu.VMEM((tt, ec), BF16), pltpu.VMEM((tt, ec), BF16),
                        pltpu.VMEM((rps * tt // lane, n_keys, lane), BF16),
                        pltpu.VMEM((rps * tt // lane, n_keys, lane), BF16),
                        pltpu.VMEM((d, tt), BF16)],
        compiler_params=_params("arbitrary", "arbitrary"),
        name="peer_main",
    )(h2, u_b, v_b, *maps)


def _final_kernel(x_ref, p_ref, gp_ref, gs_ref, g_ref, yp_ref, ys_ref, *, n_prompt_tiles):
    i = pl.program_id(0)

    def norm(x):
        return x * lax.rsqrt(jnp.mean(x * x, axis=-1, keepdims=True) + EPS) * g_ref[...]

    @pl.when(i < n_prompt_tiles)
    def _():
        yp_ref[...] = norm(x_ref[...] + gp_ref[0] * p_ref[...])

    @pl.when(i >= n_prompt_tiles)
    def _():
        ys_ref[...] = norm(x_ref[...] + gs_ref[...] * p_ref[...])


def _final(x1, peer_out, mod_p, mod_s, k_gate, g_final, n_prompt, seq):
    n, d = x1.shape
    tm = _pow2_tile(256, seq, n - n_prompt)
    npt = n_prompt // tm
    per_b = seq // tm
    return pl.pallas_call(
        functools.partial(_final_kernel, n_prompt_tiles=npt),
        grid=(n // tm,),
        in_specs=[pl.BlockSpec((tm, d), lambda i: (i, 0)),
                  pl.BlockSpec((tm, d), lambda i: (i, 0)),
                  pl.BlockSpec((1, 1, d), lambda i: (jnp.minimum(i, npt - 1) // per_b, 0, k_gate)),
                  pl.BlockSpec((tm, d), lambda i: (jnp.maximum(i - npt, 0), k_gate)),
                  pl.BlockSpec((1, d), lambda i: (0, 0))],
        out_specs=[pl.BlockSpec((tm, d), lambda i: (jnp.minimum(i, npt - 1), 0)),
                   pl.BlockSpec((tm, d), lambda i: (jnp.maximum(i - npt, 0), 0))],
        out_shape=[jax.ShapeDtypeStruct((n_prompt, d), F32),
                   jax.ShapeDtypeStruct((n - n_prompt, d), F32)],
        compiler_params=_params("arbitrary"),
        name="final_norm",
    )(x1, peer_out, mod_p, mod_s, g_final.reshape(1, d))


def kernel(x_prompt, x_sample, c_prompt, c_sample, state_hgrn, w_ada, b_ada, g_norm1, w_in, hgrn_lb_logits,
           g_hgrn_out, g_gmlp_v, w_spatial, b_spatial, w_branch_a, w_branch_b, w_out, g_norm2, w_peer_q,
           peer_sub_keys, peer_u, peer_v, g_final):
    batch, seq, d = x_prompt.shape
    db, dt, _ = x_sample.shape
    depth, _, heads, dk, dv = state_hgrn.shape
    assert depth == 1 and dk == dv == LANE_V7X
    hw = heads * dk
    groups, gc = w_spatial.shape[1], w_spatial.shape[2]
    gw = w_branch_b.shape[1]
    assert gc == LANE_V7X and gw // groups == LANE_V7X and dt <= gc and gc % dt == 0
    n_prompt, n_sample = batch * seq, db * dt
    n = n_prompt + n_sample

    c_rows = jnp.concatenate([jnp.repeat(c_sample, dt, axis=0), c_prompt], axis=0)
    c_rows = jnp.pad(c_rows, ((0, (-c_rows.shape[0]) % BF16_TILE_ROWS_V7X), (0, 0)))
    mod = _ada(c_rows, w_ada[0], b_ada[0])
    mod_s = mod
    mod_p = mod[n_sample:n_sample + batch].reshape(batch, 1, N_MOD * d)

    x_p = x_prompt.reshape(n_prompt, d)
    x_s = x_sample.reshape(n_sample, d)

    h1 = _norm_mod(x_p, x_s, g_norm1[0], mod_p, mod_s, 1, 0, n_prompt, n_sample, seq)
    proj, peer_u_b = _matmul(h1, w_in[0], F32, n_prompt, "in_proj", side_tables=(peer_u[0],))

    lb = jnp.cumsum(jax.nn.softmax(hgrn_lb_logits.astype(F32), axis=0), axis=0)[0].reshape(1, hw)
    g_ho = g_hgrn_out[0].reshape(1, hw)
    a_p, st_p = _hgrn_prompt(proj, lb, g_ho, batch, seq, heads, dk, n_prompt)
    a_s, st_s = _hgrn_sample(proj, lb, g_ho, state_hgrn, heads, dk, n_prompt, dt)

    tril = jnp.tril(jnp.ones((gc, gc), F32))
    w_sp = w_spatial[0]
    blk = jnp.arange(gc) // dt
    w_samp = jnp.tile(w_sp[:, :dt, :dt], (1, gc // dt, gc // dt)) * (blk[:, None] == blk[None, :])
    w_st = jnp.stack([w_sp * tril, w_samp * tril]).astype(BF16)
    bias_full = jnp.repeat(b_spatial[0].T, gw // groups, axis=1)
    bias_st = jnp.stack([bias_full, jnp.tile(bias_full[:dt], (gc // dt, 1))])
    gu_blk = 4 * hw // gw
    assert gu_blk * gw == 4 * hw
    bm, v_s = _gmlp(proj, w_st, bias_st, g_gmlp_v[0], n_prompt, gu_blk, gu_blk + 1)

    ga_off = 4 * hw + 2 * gw
    mix = _branches(a_p, a_s, bm, w_branch_a[0].astype(BF16), w_branch_b[0].astype(BF16), proj,
                    ga_off, ga_off + d, n_prompt)
    x1, peer_v_b = _out_proj(mix, w_out[0].astype(BF16), x_p, x_s, mod_p, mod_s, 2, n_prompt, seq, peer_v[0])

    h2 = _norm_mod(x1, x1, g_norm2[0], mod_p, mod_s, 4, 3, n_prompt, n_sample, seq)
    qp = _matmul(h2, w_peer_q[0], F32, n_prompt, "peer_query")
    maps = _peer_topk(qp, peer_sub_keys[0].astype(BF16), n_prompt)
    peer_out = _peer_main(h2, peer_u_b, peer_v_b, maps, n_prompt)
    y_p, y_s = _final(x1, peer_out, mod_p, mod_s, 5, g_final, n_prompt, seq)

    state_p = jnp.swapaxes(st_p, -1, -2)
    return (y_p.reshape(batch, seq, d), y_s.reshape(db, dt, d), state_p, st_s,
            v_s.reshape(1, db, dt, gw))
```

```python
import functools
import math

import numpy as np
import jax
import jax.numpy as jnp
from jax import lax
from jax.experimental import pallas as pl
from jax.experimental.pallas import tpu as pltpu

EPS = 1e-6
N_MOD = 6
PEER_TOPK = 16
LANE_V7X = 128
SUBLANE_V7X = 8
BF16_TILE_ROWS_V7X = 16
MAX_VALUE_SLICES = 16
VMEM_LIMIT_V7X = 60 * 1024 * 1024
F32 = jnp.float32
BF16 = jnp.bfloat16
NEG_INF = float("-inf")


def _params(*sem):
    return pltpu.CompilerParams(dimension_semantics=sem, vmem_limit_bytes=VMEM_LIMIT_V7X)


def _pow2_tile(target, *sizes):
    g = 0
    for s in sizes:
        g = math.gcd(g, s)
    t = 1
    while t * 2 <= target and g % (t * 2) == 0:
        t *= 2
    return t


def _silu(x):
    return x * jax.nn.sigmoid(x)


def _ada_kernel(c_ref, w_ref, b_ref, o_ref, s_scr):
    @pl.when(pl.program_id(0) == 0)
    def _():
        s_scr[...] = _silu(c_ref[...]).astype(BF16)

    o_ref[...] = jnp.dot(s_scr[...], w_ref[...].astype(BF16), preferred_element_type=F32) + b_ref[...]


def _ada(c_rows, w_ada, b_ada):
    m, d = c_rows.shape
    n = w_ada.shape[1]
    tn = _pow2_tile(512, n)
    return pl.pallas_call(
        _ada_kernel,
        grid=(n // tn,),
        in_specs=[pl.BlockSpec((m, d), lambda j: (0, 0)),
                  pl.BlockSpec((d, tn), lambda j: (0, j)),
                  pl.BlockSpec((1, tn), lambda j: (0, j))],
        out_specs=pl.BlockSpec((m, tn), lambda j: (0, j)),
        out_shape=jax.ShapeDtypeStruct((m, n), F32),
        scratch_shapes=[pltpu.VMEM((m, d), BF16)],
        compiler_params=_params("arbitrary"),
        name="ada_mod",
    )(c_rows, w_ada, b_ada.reshape(1, n))


def _norm_mod_kernel(xp_ref, xs_ref, g_ref, scp_ref, shp_ref, scs_ref, shs_ref, o_ref, *, n_prompt_tiles):
    i = pl.program_id(0)

    def normed(x):
        return x * lax.rsqrt(jnp.mean(x * x, axis=-1, keepdims=True) + EPS) * g_ref[...]

    @pl.when(i < n_prompt_tiles)
    def _():
        o_ref[...] = (normed(xp_ref[...]) * (1.0 + scp_ref[0]) + shp_ref[0]).astype(o_ref.dtype)

    @pl.when(i >= n_prompt_tiles)
    def _():
        o_ref[...] = (normed(xs_ref[...]) * (1.0 + scs_ref[...]) + shs_ref[...]).astype(o_ref.dtype)


def _row_sources(x_p, x_s, n_prompt, tm):
    npt = n_prompt // tm
    s_off = npt if x_s is x_p else 0
    return (lambda i: (jnp.minimum(i, npt - 1), 0)), (lambda i: (s_off + jnp.maximum(i - npt, 0), 0))


def _norm_mod(x_p, x_s, g, mod_p, mod_s, k_scale, k_shift, n_prompt, n_sample, seq):
    d = x_p.shape[1]
    n = n_prompt + n_sample
    tm = _pow2_tile(256, seq, n_sample)
    npt = n_prompt // tm
    per_b = seq // tm
    p_map, s_map = _row_sources(x_p, x_s, n_prompt, tm)

    def pidx(k):
        return lambda i: (jnp.minimum(i, npt - 1) // per_b, 0, k)

    def sidx(k):
        return lambda i: (jnp.maximum(i - npt, 0), k)

    return pl.pallas_call(
        functools.partial(_norm_mod_kernel, n_prompt_tiles=npt),
        grid=(n // tm,),
        in_specs=[pl.BlockSpec((tm, d), p_map),
                  pl.BlockSpec((tm, d), s_map),
                  pl.BlockSpec((1, d), lambda i: (0, 0)),
                  pl.BlockSpec((1, 1, d), pidx(k_scale)),
                  pl.BlockSpec((1, 1, d), pidx(k_shift)),
                  pl.BlockSpec((tm, d), sidx(k_scale)),
                  pl.BlockSpec((tm, d), sidx(k_shift))],
        out_specs=pl.BlockSpec((tm, d), lambda i: (i, 0)),
        out_shape=jax.ShapeDtypeStruct((n, d), BF16),
        compiler_params=_params("arbitrary"),
        name="norm_mod",
    )(x_p, x_s, g.reshape(1, d), mod_p, mod_p, mod_s, mod_s)


def _mm_kernel(a_ref, w_ref, *rest, n_side):
    side_in, o_ref, side_out, wb_scr = rest[:n_side], rest[n_side], rest[n_side + 1:2 * n_side + 1], rest[-1]

    @pl.when(pl.program_id(1) == 0)
    def _():
        wb_scr[...] = w_ref[...].astype(BF16)

    o_ref[...] = jnp.dot(a_ref[...], wb_scr[...], preferred_element_type=F32).astype(o_ref.dtype)
    for src, dst in zip(side_in, side_out):
        dst[...] = src[...].astype(dst.dtype)


def _side_cast_specs(side_tables, nj, ni):
    specs, shapes = [], []
    for t in side_tables:
        rows = t.shape[0]
        rb = rows // _pow2_tile(nj * ni, rows)
        last = rows // rb - 1
        specs.append(pl.BlockSpec((rb, t.shape[1]), lambda j, i, last=last: (jnp.minimum(j * ni + i, last), 0)))
        shapes.append(jax.ShapeDtypeStruct(t.shape, BF16))
    return specs, shapes


def _matmul(a, w, out_dtype, n_prompt, name, side_tables=()):
    n, k = a.shape
    nc = w.shape[1]
    tm = _pow2_tile(512, n_prompt, n - n_prompt)
    tn = _pow2_tile(1024, nc)
    ni = n // tm
    side_specs, side_shapes = _side_cast_specs(side_tables, nc // tn, ni)
    outs = pl.pallas_call(
        functools.partial(_mm_kernel, n_side=len(side_tables)),
        grid=(nc // tn, ni),
        in_specs=[pl.BlockSpec((tm, k), lambda j, i: (i, 0)),
                  pl.BlockSpec((k, tn), lambda j, i: (0, j))] + side_specs,
        out_specs=[pl.BlockSpec((tm, tn), lambda j, i: (i, j))] + side_specs,
        out_shape=[jax.ShapeDtypeStruct((n, nc), out_dtype)] + side_shapes,
        scratch_shapes=[pltpu.VMEM((k, tn), BF16)],
        compiler_params=_params("arbitrary", "arbitrary"),
        name=name,
    )(a, w, *side_tables)
    return outs[0] if not side_tables else outs


def _hgrn_tables(c):
    nl = int(math.log2(c))
    r = np.arange(c)[:, None]
    j = np.arange(c)[None, :]
    mats = []
    lvl = np.full((c, c), -1, np.int32)
    for l in range(nl):
        m = c >> (l + 1)
        mid = (r // (2 * m)) * 2 * m + m
        upper = r >= mid
        t = np.where(upper, (j >= mid) & (j <= r), (j > r) & (j <= mid - 1))
        mats.append(t)
        same = (r // (2 * m)) == (j // (2 * m))
        lvl = np.where(same & upper & (j < mid), l, lvl)
    lvl = np.where(r == j, nl, lvl)
    mats.append(j <= r)
    mats.append(j > r)
    t_all = np.concatenate(mats, axis=0).astype(np.float32)
    t_all = np.concatenate([t_all, t_all], axis=1)
    return jnp.asarray(t_all, BF16), jnp.asarray(lvl), nl


def _hgrn_prompt_kernel(q_ref, f_ref, v_ref, og_ref, lb_ref, g_ref, tall_ref, lvl_ref,
                        a_ref, st_ref, s_scr, *, chunk, n_levels, n_chunks, heads_per_step, dk):
    c = chunk
    s_scr[...] = jnp.zeros_like(s_scr)
    lb = lb_ref[...]
    lvl = lvl_ref[...]
    rows = lax.broadcasted_iota(jnp.int32, (c, q_ref.shape[1]), 0)
    nt = (((1,), (1,)), ((), ()))

    def body(ci, carry):
        sl = pl.ds(pl.multiple_of(ci * c, c), c)
        qs = _silu(q_ref[sl, :])
        f = lb + (1.0 - lb) * jax.nn.sigmoid(f_ref[sl, :])
        k = 1.0 - f
        lf = jnp.log(f)
        v = v_ref[sl, :].astype(BF16)
        lf_hi = lf.astype(BF16)
        lf_lo = (lf - lf_hi.astype(F32)).astype(BF16)
        ex = jnp.dot(tall_ref[...], jnp.concatenate([lf_hi, lf_lo], axis=0), preferred_element_type=F32)
        qs_b = qs.astype(BF16)
        k_b = k.astype(BF16)
        ys = []
        for l in range(n_levels):
            m = c >> (l + 1)
            e_l = jnp.exp(ex[l * c:(l + 1) * c, :])
            ys.append((e_l * jnp.where((rows & m) != 0, qs, k)).astype(BF16))
        bcum = ex[n_levels * c:(n_levels + 1) * c, :]
        brev = ex[(n_levels + 1) * c:(n_levels + 2) * c, :]
        qh = (qs * jnp.exp(bcum)).astype(BF16)
        kh = (k * jnp.exp(brev)).astype(BF16)
        decay = jnp.exp(bcum[c - 1:c, :])
        outs = []
        for hh in range(heads_per_step):
            hs = slice(hh * dk, (hh + 1) * dk)
            scores = jnp.where(lvl == n_levels,
                               lax.dot_general(qs_b[:, hs], k_b[:, hs], nt, preferred_element_type=F32), 0.0)
            for l in range(n_levels):
                y = ys[l][:, hs]
                p = lax.dot_general(y, y, nt, preferred_element_type=F32)
                scores = scores + jnp.where(lvl == l, p, 0.0)
            st = s_scr[hh]
            o = jnp.dot(scores.astype(BF16), v[:, hs], preferred_element_type=F32)
            o = o + lax.dot_general(qh[:, hs], st.astype(BF16), nt, preferred_element_type=F32)
            kv_t = lax.dot_general(v[:, hs], kh[:, hs], (((0,), (0,)), ((), ())), preferred_element_type=F32)
            s_scr[hh] = st * decay[:, hs] + kv_t
            outs.append(o * lax.rsqrt(jnp.mean(o * o, axis=-1, keepdims=True) + EPS))
        on = jnp.concatenate(outs, axis=1) if heads_per_step > 1 else outs[0]
        a_ref[sl, :] = (on * g_ref[...] * _silu(og_ref[sl, :])).astype(a_ref.dtype)
        return carry

    lax.fori_loop(0, n_chunks, body, 0, unroll=8)
    st_ref[0, 0] = s_scr[...]


def _hgrn_prompt(proj, lb, g_out, batch, seq, heads, dk, n_rows):
    c = _pow2_tile(128, seq)
    hp = _pow2_tile(4, heads)
    t_all, lvl, nl = _hgrn_tables(c)
    w = hp * dk
    nh = heads // hp
    col = lambda off: (lambda b, h: (b, off + h))
    kern = functools.partial(_hgrn_prompt_kernel, chunk=c, n_levels=nl, n_chunks=seq // c,
                             heads_per_step=hp, dk=dk)
    return pl.pallas_call(
        kern,
        grid=(batch, nh),
        in_specs=[pl.BlockSpec((seq, w), col(0)),
                  pl.BlockSpec((seq, w), col(nh)),
                  pl.BlockSpec((seq, w), col(2 * nh)),
                  pl.BlockSpec((seq, w), col(3 * nh)),
                  pl.BlockSpec((1, w), lambda b, h: (0, h)),
                  pl.BlockSpec((1, w), lambda b, h: (0, h)),
                  pl.BlockSpec(t_all.shape, lambda b, h: (0, 0)),
                  pl.BlockSpec(lvl.shape, lambda b, h: (0, 0))],
        out_specs=[pl.BlockSpec((seq, w), lambda b, h: (b, h)),
                   pl.BlockSpec((1, 1, hp, dk, dk), lambda b, h: (0, b, h, 0, 0))],
        out_shape=[jax.ShapeDtypeStruct((n_rows, heads * dk), BF16),
                   jax.ShapeDtypeStruct((1, batch, heads, dk, dk), F32)],
        scratch_shapes=[pltpu.VMEM((hp, dk, dk), F32)],
        compiler_params=_params("arbitrary", "arbitrary"),
        name="hgrn_prompt",
    )(proj, proj, proj, proj, lb, g_out, t_all, lvl)


def _hgrn_sample_kernel(q_ref, f_ref, v_ref, og_ref, lb_ref, g_ref, s0_ref, a_ref, s_ref, *, bt, steps):
    sub = SUBLANE_V7X
    per_tile = sub // steps
    lb = lb_ref[...]
    g = g_ref[...]
    rows = lax.broadcasted_iota(jnp.int32, (sub, q_ref.shape[1]), 0)
    tpos = rows % steps
    grp = rows // steps

    def per_batch(x, t):
        out = x[t:t + 1, :]
        for j in range(1, per_tile):
            out = jnp.where(grp == j, x[j * steps + t:j * steps + t + 1, :], out)
        return out

    def body(p, carry):
        sl = pl.ds(pl.multiple_of(p * sub, sub), sub)
        qs = _silu(q_ref[sl, :])
        f = lb + (1.0 - lb) * jax.nn.sigmoid(f_ref[sl, :])
        k = 1.0 - f
        v = v_ref[sl, :]
        bmat = jnp.log(f)
        shift = 1
        while shift < steps:
            bmat = jnp.where(tpos >= shift, bmat + pltpu.roll(bmat, shift, axis=0), bmat)
            shift *= 2
        blast = per_batch(bmat, steps - 1)
        o = jnp.zeros_like(qs)
        for t in range(steps):
            x = jnp.where(tpos >= t, qs * per_batch(k, t) * jnp.exp(jnp.minimum(bmat - per_batch(bmat, t), 0.0)), 0.0)
            o = o + jnp.sum(x, axis=-1, keepdims=True) * per_batch(v, t)
        qh = (qs * jnp.exp(bmat)).astype(BF16)
        kh = k * jnp.exp(blast - bmat)
        vb = v.astype(BF16)
        for j in range(per_tile):
            b = p * per_tile + j
            s0 = s0_ref[0, b, 0]
            mine = grp == j
            o = o + jnp.where(mine, jnp.dot(qh, s0.astype(BF16), preferred_element_type=F32), 0.0)
            kv = lax.dot_general(jnp.where(mine, kh, 0.0).astype(BF16), vb, (((0,), (0,)), ((), ())),
                                 preferred_element_type=F32)
            decay_row = jnp.exp(bmat[(j + 1) * steps - 1:(j + 1) * steps, :])
            decay_col = jnp.broadcast_to(decay_row, s0.shape).T
            s_ref[0, b, 0] = s0 * decay_col + kv
        on = o * lax.rsqrt(jnp.mean(o * o, axis=-1, keepdims=True) + EPS)
        a_ref[sl, :] = on * g * _silu(og_ref[sl, :])
        return carry

    lax.fori_loop(0, bt // per_tile, body, 0, unroll=4)


def _hgrn_sample(proj, lb, g_out, s0, heads, dk, n_prompt, steps):
    db = s0.shape[1]
    bt = _pow2_tile(32, db)
    rb = bt * steps
    assert 8 % steps == 0 and n_prompt % rb == 0
    col = lambda off: (lambda j, h: (n_prompt // rb + j, off + h))
    kern = functools.partial(_hgrn_sample_kernel, bt=bt, steps=steps)
    return pl.pallas_call(
        kern,
        grid=(db // bt, heads),
        in_specs=[pl.BlockSpec((rb, dk), col(0)),
                  pl.BlockSpec((rb, dk), col(heads)),
                  pl.BlockSpec((rb, dk), col(2 * heads)),
                  pl.BlockSpec((rb, dk), col(3 * heads)),
                  pl.BlockSpec((1, dk), lambda j, h: (0, h)),
                  pl.BlockSpec((1, dk), lambda j, h: (0, h)),
                  pl.BlockSpec((1, bt, 1, dk, dk), lambda j, h: (0, j, h, 0, 0))],
        out_specs=[pl.BlockSpec((rb, dk), lambda j, h: (j, h)),
                   pl.BlockSpec((1, bt, 1, dk, dk), lambda j, h: (0, j, h, 0, 0))],
        out_shape=[jax.ShapeDtypeStruct((db * steps, heads * dk), F32),
                   jax.ShapeDtypeStruct(s0.shape, F32)],
        compiler_params=_params("arbitrary", "arbitrary"),
        name="hgrn_sample",
    )(proj, proj, proj, proj, lb, g_out, s0)


def _gmlp_kernel(gu_ref, gv_ref, w_ref, bias_ref, g_ref, o_ref, v_ref, *, groups, gd):
    u = jax.nn.gelu(gu_ref[...])
    vv = jax.nn.gelu(gv_ref[...])
    r = lax.rsqrt(jnp.mean(vv * vv, axis=-1, keepdims=True) + EPS)
    v = vv * r * g_ref[...]
    v_ref[...] = v
    for gi in range(groups):
        sl = slice(gi * gd, (gi + 1) * gd)
        z = jnp.dot(w_ref[0, gi], v[:, sl].astype(BF16), preferred_element_type=F32) + bias_ref[0, :, sl]
        o_ref[:, sl] = (u[:, sl] * z).astype(o_ref.dtype)


def _gmlp(proj, w_st, bias_st, g_v, n_prompt, gu_blk, gv_blk):
    n = proj.shape[0]
    _, groups, c, _ = w_st.shape
    gw = bias_st.shape[2]
    npc = n_prompt // c
    sel = lambda i: jnp.where(i >= npc, 1, 0)
    kern = functools.partial(_gmlp_kernel, groups=groups, gd=gw // groups)
    return pl.pallas_call(
        kern,
        grid=(n // c,),
        in_specs=[pl.BlockSpec((c, gw), lambda i: (i, gu_blk)),
                  pl.BlockSpec((c, gw), lambda i: (i, gv_blk)),
                  pl.BlockSpec((1, groups, c, c), lambda i: (sel(i), 0, 0, 0)),
                  pl.BlockSpec((1, c, gw), lambda i: (sel(i), 0, 0)),
                  pl.BlockSpec((1, gw), lambda i: (0, 0))],
        out_specs=[pl.BlockSpec((c, gw), lambda i: (i, 0)),
                   pl.BlockSpec((c, gw), lambda i: (jnp.maximum(i - npc, 0), 0))],
        out_shape=[jax.ShapeDtypeStruct((n, gw), BF16),
                   jax.ShapeDtypeStruct((n - n_prompt, gw), F32)],
        compiler_params=_params("arbitrary"),
        name="chunk_mlp",
    )(proj, proj, w_st, bias_st, g_v.reshape(1, gw))


def _branch_kernel(ap_ref, as_ref, b_ref, wa_ref, wb_ref, ga_ref, gb_ref, o_ref, *, n_prompt_tiles):
    a = jnp.where(pl.program_id(1) < n_prompt_tiles, ap_ref[...], as_ref[...].astype(BF16))
    ya = jnp.dot(a, wa_ref[...], preferred_element_type=F32)
    yb = jnp.dot(b_ref[...], wb_ref[...], preferred_element_type=F32)
    o_ref[...] = (jax.nn.sigmoid(ga_ref[...]) * ya + jax.nn.sigmoid(gb_ref[...]) * yb).astype(o_ref.dtype)


def _branches(a_p, a_s, bm, wa, wb, proj, ga_off, gb_off, n_prompt):
    n, kb = bm.shape
    ka = a_p.shape[1]
    d = wa.shape[1]
    tm = _pow2_tile(512, n_prompt, n - n_prompt)
    tn = _pow2_tile(1024, d, ga_off, gb_off)
    npt = n_prompt // tm
    return pl.pallas_call(
        functools.partial(_branch_kernel, n_prompt_tiles=npt),
        grid=(d // tn, n // tm),
        in_specs=[pl.BlockSpec((tm, ka), lambda j, i: (jnp.minimum(i, npt - 1), 0)),
                  pl.BlockSpec((tm, ka), lambda j, i: (jnp.maximum(i - npt, 0), 0)),
                  pl.BlockSpec((tm, kb), lambda j, i: (i, 0)),
                  pl.BlockSpec((ka, tn), lambda j, i: (0, j)),
                  pl.BlockSpec((kb, tn), lambda j, i: (0, j)),
                  pl.BlockSpec((tm, tn), lambda j, i: (i, ga_off // tn + j)),
                  pl.BlockSpec((tm, tn), lambda j, i: (i, gb_off // tn + j))],
        out_specs=pl.BlockSpec((tm, tn), lambda j, i: (i, j)),
        out_shape=jax.ShapeDtypeStruct((n, d), BF16),
        compiler_params=_params("arbitrary", "arbitrary"),
        name="branches",
    )(a_p, a_s, bm, wa, wb, proj, proj)


def _out_kernel(m_ref, w_ref, xp_ref, xs_ref, gp_ref, gs_ref, side_ref, o_ref, side_out_ref, *, n_prompt_tiles):
    i = pl.program_id(1)
    y = jnp.dot(m_ref[...], w_ref[...], preferred_element_type=F32)
    side_out_ref[...] = side_ref[...].astype(side_out_ref.dtype)

    @pl.when(i < n_prompt_tiles)
    def _():
        o_ref[...] = xp_ref[...] + gp_ref[0] * y

    @pl.when(i >= n_prompt_tiles)
    def _():
        o_ref[...] = xs_ref[...] + gs_ref[...] * y


def _out_proj(mix, w, x_p, x_s, mod_p, mod_s, k_gate, n_prompt, seq, side_table):
    n, d = mix.shape
    tm = _pow2_tile(512, seq, n - n_prompt)
    tn = _pow2_tile(1024, d)
    npt = n_prompt // tm
    per_b = seq // tm
    nj = d // tn
    side_specs, side_shapes = _side_cast_specs((side_table,), nj, n // tm)
    return pl.pallas_call(
        functools.partial(_out_kernel, n_prompt_tiles=npt),
        grid=(nj, n // tm),
        in_specs=[pl.BlockSpec((tm, d), lambda j, i: (i, 0)),
                  pl.BlockSpec((d, tn), lambda j, i: (0, j)),
                  pl.BlockSpec((tm, tn), lambda j, i: (jnp.minimum(i, npt - 1), j)),
                  pl.BlockSpec((tm, tn), lambda j, i: (jnp.maximum(i - npt, 0), j)),
                  pl.BlockSpec((1, 1, tn), lambda j, i: (jnp.minimum(i, npt - 1) // per_b, 0, k_gate * nj + j)),
                  pl.BlockSpec((tm, tn), lambda j, i: (jnp.maximum(i - npt, 0), k_gate * nj + j))] + side_specs,
        out_specs=[pl.BlockSpec((tm, tn), lambda j, i: (i, j))] + side_specs,
        out_shape=[jax.ShapeDtypeStruct((n, d), F32)] + side_shapes,
        compiler_params=_params("arbitrary", "arbitrary"),
        name="out_proj",
    )(mix, w, x_p, x_s, mod_p, mod_s, side_table)


def _cand_pairs(k):
    return [(i, j) for i in range(k) for j in range(k) if (i + 1) * (j + 1) <= k]


def _arg_rounds(s, iota, k, exact, with_weights=False):
    rank = jnp.full(s.shape, float(k), F32)
    s_in = s
    vals = []
    big = float(s.shape[0])
    for i in range(k):
        m = jnp.max(s, axis=0, keepdims=True)
        if exact:
            idx = jnp.min(jnp.where(s == m, iota, big), axis=0, keepdims=True)
            sel = iota == idx
        else:
            sel = s == m
        rank = jnp.where(sel, float(i), rank)
        s = jnp.where(sel, NEG_INF, s)
        vals.append(m)
    weight = jnp.where(rank < float(k), jnp.exp(s_in - vals[0]), 0.0) if with_weights else None
    return vals, rank, weight


def _peer_gates(s1, s2, jsum, cand_scr, cols, pairs, k, exact):
    n_iota = lax.broadcasted_iota(jnp.int32, s1.shape, 0).astype(F32)
    a, r1, ea = _arg_rounds(s1, n_iota, k, exact, with_weights=True)
    b, r2, eb = _arg_rounds(s2, n_iota, k, exact, with_weights=True)

    n_cand = cand_scr.shape[0]
    cand_scr[len(pairs):, cols] = jnp.full((n_cand - len(pairs), s1.shape[1]), NEG_INF, F32)
    for p, (i, j) in enumerate(pairs):
        cand_scr[p:p + 1, cols] = a[i] + b[j]
    c0 = cand_scr[:, cols]
    p_iota = lax.broadcasted_iota(jnp.int32, c0.shape, 0).astype(F32)
    _, prank, _ = _arg_rounds(c0, p_iota, k, exact)
    taken = prank < float(k)
    w = jnp.where(taken, jnp.exp(c0 - (a[0] + b[0])), 0.0)
    inv_z = 1.0 / jnp.sum(w, axis=0, keepdims=True)
    ones = jnp.where(taken, 1.0, 0.0)
    jcnt = jnp.dot(jsum, ones.astype(BF16), preferred_element_type=F32)

    eaz = ea * inv_z
    jd = jnp.zeros(s1.shape, F32)
    for i in range(k):
        jd = jnp.where(r1 == float(i), jcnt[i:i + 1, :], jd)

    def excess(rank):
        return jnp.abs(jnp.sum(jnp.where(rank < float(k), 1.0, 0.0), axis=0, keepdims=True) - float(k))

    return eaz, jd, eb, r2, excess(r1) + excess(r2) + excess(prank)


def _peer_topk_kernel(q_ref, sk_ref, jsum_ref, eaz_ref, jd_ref, eb_ref, r2_ref, cand_scr, *, half, pairs, k):
    nt = (((1,), (1,)), ((), ()))
    q = q_ref[...].astype(BF16)
    s1 = lax.dot_general(sk_ref[0, 0], q[:, :half], nt, preferred_element_type=F32)
    s2 = lax.dot_general(sk_ref[0, 1], q[:, half:], nt, preferred_element_type=F32)
    jsum = jsum_ref[...]
    n_blocks = s1.shape[1] // LANE_V7X

    def run(exact):
        bad = None
        for cb in range(n_blocks):
            cols = slice(cb * LANE_V7X, (cb + 1) * LANE_V7X)
            eaz, jd, eb, r2, dev = _peer_gates(s1[:, cols], s2[:, cols], jsum, cand_scr, cols, pairs, k, exact)
            eaz_ref[0, cb] = eaz
            jd_ref[0, cb] = jd
            eb_ref[0, cb] = eb.astype(eb_ref.dtype)
            r2_ref[0, cb] = r2.astype(r2_ref.dtype)
            bad = dev if bad is None else bad + dev
        return jnp.max(bad)

    miscount = run(exact=False)

    @pl.when(miscount > 0.5)
    def _():
        run(exact=True)


def _peer_topk(qp, sub_keys_b, n_prompt):
    n = qp.shape[0]
    heads, _, n_keys, half = sub_keys_b.shape
    tt = _pow2_tile(512, n_prompt, n - n_prompt)
    pairs = _cand_pairs(PEER_TOPK)
    n_cand = -(-len(pairs) // 64) * 64
    jsum = np.zeros((PEER_TOPK, n_cand), np.float32)
    for p, (i, _) in enumerate(pairs):
        jsum[i, p] = 1.0
    kern = functools.partial(_peer_topk_kernel, half=half, pairs=pairs, k=PEER_TOPK)
    maps = jax.ShapeDtypeStruct((heads, n // LANE_V7X, n_keys, LANE_V7X), F32)
    maps_b = jax.ShapeDtypeStruct((heads, n // LANE_V7X, n_keys, LANE_V7X), BF16)
    mspec = pl.BlockSpec((1, tt // LANE_V7X, n_keys, LANE_V7X), lambda i, h: (h, i, 0, 0))
    return pl.pallas_call(
        kern,
        grid=(n // tt, heads),
        in_specs=[pl.BlockSpec((tt, 2 * half), lambda i, h: (i, h)),
                  pl.BlockSpec((1, 2, n_keys, half), lambda i, h: (h, 0, 0, 0)),
                  pl.BlockSpec(jsum.shape, lambda i, h: (0, 0))],
        out_specs=[mspec, mspec, mspec, mspec],
        out_shape=[maps, maps, maps_b, maps_b],
        scratch_shapes=[pltpu.VMEM((n_cand, tt), F32)],
        compiler_params=_params("arbitrary", "arbitrary"),
        name="peer_topk",
    )(qp, sub_keys_b, jnp.asarray(jsum, BF16))


def _peer_main_kernel(h_ref, u_ref, v_ref, eaz_ref, jd_ref, eb_ref, r2_ref, o_ref,
                      ga0_scr, ga1_scr, g0_scr, g1_scr, ht_scr, *, heads, n_keys, rows_per_step, n_steps):
    e = pl.program_id(1)
    n_tb = h_ref.shape[0] // LANE_V7X
    blocks = [(r, tb) for r in range(rows_per_step) for tb in range(n_tb)]
    pair = 2 if rows_per_step % 2 == 0 else 1
    gate_items = [(r0, tb) for r0 in range(0, rows_per_step, pair) for tb in range(n_tb)]
    n_slices = math.gcd(len(gate_items), MAX_VALUE_SLICES)

    def gates(chunk, g_scr, part=None):
        for r0, tb in (gate_items if part is None else gate_items[part::n_slices]):
            acc = [None] * pair
            for h in range(heads):
                r2 = r2_ref[h, tb]
                eb = eb_ref[h, tb]
                for kk in range(pair):
                    n1 = chunk * rows_per_step + r0 + kk
                    ea = eaz_ref[h, tb, pl.ds(n1, 1), :].astype(BF16)
                    jd = jd_ref[h, tb, pl.ds(n1, 1), :].astype(BF16)
                    t = ea * (jnp.clip(jd - r2, 0.0, 1.0).astype(BF16) * eb)
                    acc[kk] = t if acc[kk] is None else acc[kk] + t
            for kk in range(pair):
                g_scr[(r0 + kk) * n_tb + tb] = acc[kk]

    @pl.when(e == 0)
    def _():
        o_ref[...] = jnp.zeros_like(o_ref)
        ga1_scr[...] = jnp.zeros_like(ga1_scr)
        gates(0, g0_scr)
        ht_scr[...] = h_ref[...].T

    def stage(ga_cur, ga_prev, g_cur, g_next):
        act_t = jnp.dot(u_ref[...], ht_scr[...], preferred_element_type=F32)
        m_slices = math.gcd(len(blocks), MAX_VALUE_SLICES)
        dq = o_ref.shape[1] // m_slices
        for q in range(m_slices):
            qs = slice(q * dq, (q + 1) * dq)
            o_ref[:, qs] += jnp.dot(ga_prev[...], v_ref[:, qs], preferred_element_type=F32)
            for r, tb in blocks[q::m_slices]:
                sl = slice(r * n_keys, (r + 1) * n_keys)
                ts = slice(tb * LANE_V7X, (tb + 1) * LANE_V7X)
                ga_cur[ts, sl] = (g_cur[r * n_tb + tb] * jax.nn.gelu(act_t[sl, ts].astype(BF16))).T
            if q % (m_slices // n_slices) == 0:
                gates(jnp.minimum(e + 1, n_steps - 1), g_next, part=q // (m_slices // n_slices))

    @pl.when(e % 2 == 0)
    def _():
        stage(ga0_scr, ga1_scr, g0_scr, g1_scr)

    @pl.when(e % 2 == 1)
    def _():
        stage(ga1_scr, ga0_scr, g1_scr, g0_scr)


def _peer_main(h2, u_b, v_b, maps, n_prompt):
    n, d = h2.shape
    n_exp = u_b.shape[0]
    heads, _, n_keys, lane = maps[0].shape
    tt = _pow2_tile(512, n_prompt, n - n_prompt)
    ec = _pow2_tile(512, n_exp)
    rps = ec // n_keys
    n_steps = n_exp // ec
    kern = functools.partial(_peer_main_kernel, heads=heads, n_keys=n_keys, rows_per_step=rps, n_steps=n_steps)
    once = pl.Buffered(1)
    mspec = pl.BlockSpec((heads, tt // lane, n_keys, lane), lambda i, e: (0, i, 0, 0), pipeline_mode=once)
    return pl.pallas_call(
        kern,
        grid=(n // tt, n_steps + 1),
        in_specs=[pl.BlockSpec((tt, d), lambda i, e: (i, 0), pipeline_mode=once),
                  pl.BlockSpec((ec, d), lambda i, e: (jnp.minimum(e, n_steps - 1), 0)),
                  pl.BlockSpec((ec, d), lambda i, e: (jnp.maximum(e - 1, 0), 0)),
                  mspec, mspec, mspec, mspec],
        out_specs=pl.BlockSpec((tt, d), lambda i, e: (i, 0), pipeline_mode=once),
        out_shape=jax.ShapeDtypeStruct((n, d), F32),
        scratch_shapes=[pltpu.VMEM((tt, ec), BF16), pltpu.VMEM((tt, ec), BF16),
                        pltpu.VMEM((rps * tt // lane, n_keys, lane), BF16),
                        pltpu.VMEM((rps * tt // lane, n_keys, lane), BF16),
                        pltpu.VMEM((d, tt), BF16)],
        compiler_params=_params("arbitrary", "arbitrary"),
        name="peer_main",
    )(h2, u_b, v_b, *maps)


def _final_kernel(x_ref, p_ref, gp_ref, gs_ref, g_ref, yp_ref, ys_ref, *, n_prompt_tiles):
    i = pl.program_id(0)

    def norm(x):
        return x * lax.rsqrt(jnp.mean(x * x, axis=-1, keepdims=True) + EPS) * g_ref[...]

    @pl.when(i < n_prompt_tiles)
    def _():
        yp_ref[...] = norm(x_ref[...] + gp_ref[0] * p_ref[...])

    @pl.when(i >= n_prompt_tiles)
    def _():
        ys_ref[...] = norm(x_ref[...] + gs_ref[...] * p_ref[...])


def _final(x1, peer_out, mod_p, mod_s, k_gate, g_final, n_prompt, seq):
    n, d = x1.shape
    tm = _pow2_tile(256, seq, n - n_prompt)
    npt = n_prompt // tm
    per_b = seq // tm
    return pl.pallas_call(
        functools.partial(_final_kernel, n_prompt_tiles=npt),
        grid=(n // tm,),
        in_specs=[pl.BlockSpec((tm, d), lambda i: (i, 0)),
                  pl.BlockSpec((tm, d), lambda i: (i, 0)),
                  pl.BlockSpec((1, 1, d), lambda i: (jnp.minimum(i, npt - 1) // per_b, 0, k_gate)),
                  pl.BlockSpec((tm, d), lambda i: (jnp.maximum(i - npt, 0), k_gate)),
                  pl.BlockSpec((1, d), lambda i: (0, 0))],
        out_specs=[pl.BlockSpec((tm, d), lambda i: (jnp.minimum(i, npt - 1), 0)),
                   pl.BlockSpec((tm, d), lambda i: (jnp.maximum(i - npt, 0), 0))],
        out_shape=[jax.ShapeDtypeStruct((n_prompt, d), F32),
                   jax.ShapeDtypeStruct((n - n_prompt, d), F32)],
        compiler_params=_params("arbitrary"),
        name="final_norm",
    )(x1, peer_out, mod_p, mod_s, g_final.reshape(1, d))


def kernel(x_prompt, x_sample, c_prompt, c_sample, state_hgrn, w_ada, b_ada, g_norm1, w_in, hgrn_lb_logits,
           g_hgrn_out, g_gmlp_v, w_spatial, b_spatial, w_branch_a, w_branch_b, w_out, g_norm2, w_peer_q,
           peer_sub_keys, peer_u, peer_v, g_final):
    batch, seq, d = x_prompt.shape
    db, dt, _ = x_sample.shape
    depth, _, heads, dk, dv = state_hgrn.shape
    assert depth == 1 and dk == dv == LANE_V7X
    hw = heads * dk
    groups, gc = w_spatial.shape[1], w_spatial.shape[2]
    gw = w_branch_b.shape[1]
    assert gc == LANE_V7X and gw // groups == LANE_V7X and dt <= gc and gc % dt == 0
    n_prompt, n_sample = batch * seq, db * dt
    n = n_prompt + n_sample

    c_rows = jnp.concatenate([jnp.repeat(c_sample, dt, axis=0), c_prompt], axis=0)
    c_rows = jnp.pad(c_rows, ((0, (-c_rows.shape[0]) % BF16_TILE_ROWS_V7X), (0, 0)))
    mod = _ada(c_rows, w_ada[0], b_ada[0])
    mod_s = mod
    mod_p = mod[n_sample:n_sample + batch].reshape(batch, 1, N_MOD * d)

    x_p = x_prompt.reshape(n_prompt, d)
    x_s = x_sample.reshape(n_sample, d)

    h1 = _norm_mod(x_p, x_s, g_norm1[0], mod_p, mod_s, 1, 0, n_prompt, n_sample, seq)
    proj, peer_u_b = _matmul(h1, w_in[0], F32, n_prompt, "in_proj", side_tables=(peer_u[0],))

    lb = jnp.cumsum(jax.nn.softmax(hgrn_lb_logits.astype(F32), axis=0), axis=0)[0].reshape(1, hw)
    g_ho = g_hgrn_out[0].reshape(1, hw)
    a_p, st_p = _hgrn_prompt(proj, lb, g_ho, batch, seq, heads, dk, n_prompt)
    a_s, st_s = _hgrn_sample(proj, lb, g_ho, state_hgrn, heads, dk, n_prompt, dt)

    tril = jnp.tril(jnp.ones((gc, gc), F32))
    w_sp = w_spatial[0]
    blk = jnp.arange(gc) // dt
    w_samp = jnp.tile(w_sp[:, :dt, :dt], (1, gc // dt, gc // dt)) * (blk[:, None] == blk[None, :])
    w_st = jnp.stack([w_sp * tril, w_samp * tril]).astype(BF16)
    bias_full = jnp.repeat(b_spatial[0].T, gw // groups, axis=1)
    bias_st = jnp.stack([bias_full, jnp.tile(bias_full[:dt], (gc // dt, 1))])
    gu_blk = 4 * hw // gw
    assert gu_blk * gw == 4 * hw
    bm, v_s = _gmlp(proj, w_st, bias_st, g_gmlp_v[0], n_prompt, gu_blk, gu_blk + 1)

    ga_off = 4 * hw + 2 * gw
    mix = _branches(a_p, a_s, bm, w_branch_a[0].astype(BF16), w_branch_b[0].astype(BF16), proj,
                    ga_off, ga_off + d, n_prompt)
    x1, peer_v_b = _out_proj(mix, w_out[0].astype(BF16), x_p, x_s, mod_p, mod_s, 2, n_prompt, seq, peer_v[0])

    h2 = _norm_mod(x1, x1, g_norm2[0], mod_p, mod_s, 4, 3, n_prompt, n_sample, seq)
    qp = _matmul(h2, w_peer_q[0], F32, n_prompt, "peer_query")
    maps = _peer_topk(qp, peer_sub_keys[0].astype(BF16), n_prompt)
    peer_out = _peer_main(h2, peer_u_b, peer_v_b, maps, n_prompt)
    y_p, y_s = _final(x1, peer_out, mod_p, mod_s, 5, g_final, n_prompt, seq)

    state_p = jnp.swapaxes(st_p, -1, -2)
    return (y_p.reshape(batch, seq, d), y_s.reshape(db, dt, d), state_p, st_s,
            v_s.reshape(1, db, dt, gw))
```
